```python
import jax, jax.numpy as jnp
from jax import lax
import numpy as np

D_MODEL = 1024
BATCH = 16
SEQ = 256
DEPTH = 4
DEC_BATCH = 2
DEC_SEQ = 1024
PAST_LEN = 256

GRID_W = 64
N_MIXERS = 3
N_CONV_LAYERS = (DEPTH + 2) // 3
N_ATTN_LAYERS = (DEPTH + 1) // 3
N_NA_LAYERS = DEPTH // 3
HEAD_DIM = 64
N_HEADS = D_MODEL // HEAD_DIM
N_KV_HEADS = 4
N_NA_HEADS = D_MODEL // HEAD_DIM
QUERY_BLOCK = 128
ROPE_THETA = 10000.0
ROPE_PAIRS = HEAD_DIM // 4
WIN_R = 8
WIN_C = 16
CONV_WIDTH = 3
N_EXPERTS = 32
TOP_K = 4
D_FF = D_MODEL
SWIGLU_LIMIT = 7.0
SWIGLU_ALPHA = 1.702
DEEPNORM_ALPHA = (2 * DEPTH) ** 0.25
DEEPNORM_BETA = (8 * DEPTH) ** -0.25
LN_EPS = 1e-5
RMS_EPS = 1e-6

kernel_name = 'hybrid_diffusion_conv_gqa_natten_moe_step'


def layer_norm(x, g, b):
    xf = x.astype(jnp.float32)
    mu = jnp.mean(xf, -1, keepdims=True)
    var = jnp.mean(jnp.square(xf - mu), -1, keepdims=True)
    y = (xf - mu) * lax.rsqrt(var + LN_EPS)
    return (y * g.astype(jnp.float32) + b.astype(jnp.float32)).astype(x.dtype)


def rms_norm_heads(x, g):
    xf = x.astype(jnp.float32)
    y = xf * lax.rsqrt(jnp.mean(jnp.square(xf), -1, keepdims=True) + RMS_EPS)
    return (y * g.astype(jnp.float32)).astype(x.dtype)


def adaln_params(cond, w_mod, b_mod):
    mod = jax.nn.silu(cond) @ w_mod + b_mod
    return [m[:, None, :] for m in jnp.split(mod, 6, axis=-1)]


def axial_rope(x):
    n = x.shape[1]
    t = jnp.arange(n, dtype=jnp.int32)
    row = (t // GRID_W).astype(jnp.float32)
    col = (t % GRID_W).astype(jnp.float32)
    inv = ROPE_THETA ** (-jnp.arange(ROPE_PAIRS, dtype=jnp.float32) / ROPE_PAIRS)
    xf = x.astype(jnp.float32)

    def rot(xh, pos):
        ang = pos[:, None] * inv[None, :]
        cos = jnp.cos(ang)[None, :, None, :]
        sin = jnp.sin(ang)[None, :, None, :]
        x1, x2 = xh[..., :ROPE_PAIRS], xh[..., ROPE_PAIRS:]
        return jnp.concatenate([x1 * cos - x2 * sin, x1 * sin + x2 * cos], -1)

    half = HEAD_DIM // 2
    out = jnp.concatenate([rot(xf[..., :half], row), rot(xf[..., half:], col)], -1)
    return out.astype(x.dtype)


def blocked_attention(q, k, v):
    b, s, h, dh = q.shape
    g = k.shape[2]
    rep = h // g
    nb = s // QUERY_BLOCK
    qb = q.reshape(b, nb, QUERY_BLOCK, g, rep, dh).transpose(1, 0, 2, 3, 4, 5)
    scale = dh ** -0.5

    def one_block(qblk):
        sc = jnp.einsum('bqgrd,bkgd->bgrqk', qblk, k).astype(jnp.float32) * scale
        p = jax.nn.softmax(sc, axis=-1).astype(v.dtype)
        return jnp.einsum('bgrqk,bkgd->bqgrd', p, v)

    o = lax.map(one_block, qb)
    return o.transpose(1, 0, 2, 3, 4, 5).reshape(b, s, h * dh)


def short_conv_mixer(h, w_in, conv_w, conv_b, w_out):
    gb, gc, xv = jnp.split(h @ w_in, 3, axis=-1)
    u = gc * xv
    pad = CONV_WIDTH // 2
    u = lax.conv_general_dilated(u, conv_w[:, None, :], window_strides=(1,), padding=((pad, pad),),
                                 dimension_numbers=('NWC', 'WIO', 'NWC'),
                                 feature_group_count=u.shape[-1]) + conv_b
    return (gb * u) @ w_out


def gqa_project(h, w_qkv, q_norm, k_norm):
    b, n, _ = h.shape
    q, k, v = jnp.split(h @ w_qkv, [N_HEADS * HEAD_DIM, (N_HEADS + N_KV_HEADS) * HEAD_DIM], axis=-1)
    q = rms_norm_heads(q.reshape(b, n, N_HEADS, HEAD_DIM), q_norm)
    k = rms_norm_heads(k.reshape(b, n, N_KV_HEADS, HEAD_DIM), k_norm)
    v = v.reshape(b, n, N_KV_HEADS, HEAD_DIM)
    return q, k, v


def na_project(h, w_qkv):
    b, n, _ = h.shape
    q, k, v = jnp.split(h @ w_qkv, 3, axis=-1)
    return (q.reshape(b, n, N_NA_HEADS, HEAD_DIM), k.reshape(b, n, N_NA_HEADS, HEAD_DIM),
            v.reshape(b, n, N_NA_HEADS, HEAD_DIM))


def neighbourhood_attention(q, k, v, k_ctx, v_ctx, rpb):
    b, n, h, dh = q.shape
    rows = n // GRID_W
    wr = min(WIN_R, rows)
    t_ctx = k_ctx.shape[1]
    scale = dh ** -0.5
    cols = jnp.arange(GRID_W, dtype=jnp.int32)
    col_start = jnp.clip(cols - WIN_C // 2, 0, GRID_W - WIN_C)
    key_cols = col_start[:, None] + jnp.arange(WIN_C, dtype=jnp.int32)[None, :]
    col_off = key_cols - cols[:, None] + (WIN_C - 1)

    def one_row(r):
        row_start = jnp.clip(r - wr // 2, 0, rows - wr)
        key_rows = row_start + jnp.arange(wr, dtype=jnp.int32)
        idx = (key_rows[None, :, None] * GRID_W + key_cols[:, None, :]).reshape(GRID_W, wr * WIN_C)
        row_off = key_rows - r + (WIN_R - 1)
        bias = rpb[:, row_off[None, :, None], col_off[:, None, :]].reshape(h, GRID_W, wr * WIN_C)
        q_r = lax.dynamic_slice_in_dim(q, r * GRID_W, GRID_W, axis=1)
        k_g = jnp.take(k, idx, axis=1)
        v_g = jnp.take(v, idx, axis=1)
        s_loc = jnp.einsum('bqhd,bqkhd->bhqk', q_r, k_g).astype(jnp.float32) * scale \
            + bias.astype(jnp.float32)[None]
        s_ctx = jnp.einsum('bqhd,bkhd->bhqk', q_r, k_ctx).astype(jnp.float32) * scale
        p = jax.nn.softmax(jnp.concatenate([s_ctx, s_loc], -1), axis=-1).astype(v.dtype)
        return (jnp.einsum('bhqk,bkhd->bqhd', p[..., :t_ctx], v_ctx)
                + jnp.einsum('bhqk,bqkhd->bqhd', p[..., t_ctx:], v_g))

    o = lax.map(one_row, jnp.arange(rows, dtype=jnp.int32))
    return o.transpose(1, 0, 2, 3, 4).reshape(b, n, h * dh)


def clamped_swiglu(hh):
    gate, up = hh[..., :D_FF], hh[..., D_FF:]
    gate = jnp.minimum(gate, SWIGLU_LIMIT)
    up = jnp.clip(up, -SWIGLU_LIMIT, SWIGLU_LIMIT)
    return gate * jax.nn.sigmoid(SWIGLU_ALPHA * gate) * (up + 1.0)


def moe_ffn(h, router_w, router_b, w1, b1, w2, b2):
    b, n, d = h.shape
    xt = h.reshape(b * n, d)
    logits = (xt @ router_w + router_b).astype(jnp.float32)
    top_vals, top_idx = lax.top_k(logits, TOP_K)
    top_w = jax.nn.softmax(top_vals, axis=-1)
    combine = jnp.einsum('tk,tke->te', top_w,
                         jax.nn.one_hot(top_idx, N_EXPERTS, dtype=jnp.float32)).astype(h.dtype)
    out = jnp.zeros_like(xt)
    for e in range(N_EXPERTS):
        y = clamped_swiglu(xt @ w1[e] + b1[e]) @ w2[e] + b2[e]
        out = out + combine[:, e:e + 1] * y
    return out.reshape(b, n, d)


def setup_inputs(seed: int = 0) -> dict:
    key = jax.random.key(seed)
    ks = iter(jax.random.split(key, 32))

    def nrm(shape, scale):
        return jax.random.normal(next(ks), shape, jnp.float32) * scale

    d = D_MODEL
    qkv_w = (N_HEADS + 2 * N_KV_HEADS) * HEAD_DIM
    return {
        'x_prompt': nrm((BATCH, SEQ, d), 1.0),
        'x_sample': nrm((DEC_BATCH, DEC_SEQ, d), 1.0),
        'cache_k_attn': nrm((DEC_BATCH, N_ATTN_LAYERS, PAST_LEN, N_KV_HEADS, HEAD_DIM), 1.0),
        'cache_v_attn': nrm((DEC_BATCH, N_ATTN_LAYERS, PAST_LEN, N_KV_HEADS, HEAD_DIM), 1.0),
        'cache_k_na': nrm((DEC_BATCH, N_NA_LAYERS, PAST_LEN, N_NA_HEADS, HEAD_DIM), 1.0),
        'cache_v_na': nrm((DEC_BATCH, N_NA_LAYERS, PAST_LEN, N_NA_HEADS, HEAD_DIM), 1.0),
        'c': nrm((DEC_BATCH, d), 1.0),
        'c_ctx': nrm((d,), 1.0),
        'w_mod': nrm((DEPTH, d, 6 * d), 0.5 * d ** -0.5),
        'b_mod': nrm((DEPTH, 6 * d), 0.02),
        'ln1_g': 1.0 + nrm((DEPTH, d), 0.02),
        'ln1_b': nrm((DEPTH, d), 0.02),
        'ln2_g': 1.0 + nrm((DEPTH, d), 0.02),
        'ln2_b': nrm((DEPTH, d), 0.02),
        'conv_w_in': nrm((N_CONV_LAYERS, d, 3 * d), d ** -0.5),
        'conv_w': nrm((N_CONV_LAYERS, CONV_WIDTH, d), CONV_WIDTH ** -0.5),
        'conv_b': nrm((N_CONV_LAYERS, d), 0.02),
        'conv_w_out': nrm((N_CONV_LAYERS, d, d), DEEPNORM_BETA * d ** -0.5),
        'attn_w_qkv': nrm((N_ATTN_LAYERS, d, qkv_w), d ** -0.5),
        'attn_q_norm': 1.0 + nrm((N_ATTN_LAYERS, HEAD_DIM), 0.02),
        'attn_k_norm': 1.0 + nrm((N_ATTN_LAYERS, HEAD_DIM), 0.02),
        'attn_w_o': nrm((N_ATTN_LAYERS, N_HEADS * HEAD_DIM, d), DEEPNORM_BETA * d ** -0.5),
        'na_w_qkv': nrm((N_NA_LAYERS, d, 3 * d), d ** -0.5),
        'na_rpb': nrm((N_NA_LAYERS, N_NA_HEADS, 2 * WIN_R - 1, 2 * WIN_C - 1), 0.05),
        'na_w_o': nrm((N_NA_LAYERS, d, d), DEEPNORM_BETA * d ** -0.5),
        'router_w': nrm((DEPTH, d, N_EXPERTS), d ** -0.5),
        'router_b': nrm((DEPTH, N_EXPERTS), 0.01),
        'moe_w1': nrm((DEPTH, N_EXPERTS, d, 2 * D_FF), d ** -0.5),
        'moe_b1': nrm((DEPTH, N_EXPERTS, 2 * D_FF), 0.02),
        'moe_w2': nrm((DEPTH, N_EXPERTS, D_FF, d), DEEPNORM_BETA * D_FF ** -0.5),
        'moe_b2': nrm((DEPTH, N_EXPERTS, d), 0.02),
    }


def reference(x_prompt, x_sample, cache_k_attn, cache_v_attn, cache_k_na, cache_v_na, c, c_ctx,
              w_mod, b_mod, ln1_g, ln1_b, ln2_g, ln2_b,
              conv_w_in, conv_w, conv_b, conv_w_out,
              attn_w_qkv, attn_q_norm, attn_k_norm, attn_w_o,
              na_w_qkv, na_rpb, na_w_o,
              router_w, router_b, moe_w1, moe_b1, moe_w2, moe_b2):
    xp = x_prompt
    xs = x_sample
    k_attn_list, v_attn_list, k_na_list, v_na_list = [], [], [], []
    for l in range(DEPTH):
        kind, j = l % N_MIXERS, l // N_MIXERS
        sh1p, sc1p, g1p, sh2p, sc2p, g2p = adaln_params(c_ctx[None, :], w_mod[l], b_mod[l])
        sh1s, sc1s, g1s, sh2s, sc2s, g2s = adaln_params(c, w_mod[l], b_mod[l])
        hp = xp * (1 + sc1p) + sh1p
        hs = xs * (1 + sc1s) + sh1s
        if kind == 0:
            op = short_conv_mixer(hp, conv_w_in[j], conv_w[j], conv_b[j], conv_w_out[j])
            os_ = short_conv_mixer(hs, conv_w_in[j], conv_w[j], conv_b[j], conv_w_out[j])
        elif kind == 1:
            qp, kp, vp = gqa_project(hp, attn_w_qkv[j], attn_q_norm[j], attn_k_norm[j])
            k_attn_list.append(kp)
            v_attn_list.append(vp)
            op = blocked_attention(qp, kp, vp) @ attn_w_o[j]
            qs, ks_, vs = gqa_project(hs, attn_w_qkv[j], attn_q_norm[j], attn_k_norm[j])
            qs, ks_ = axial_rope(qs), axial_rope(ks_)
            k_all = jnp.concatenate([cache_k_attn[:, j], ks_], axis=1)
            v_all = jnp.concatenate([cache_v_attn[:, j], vs], axis=1)
            os_ = blocked_attention(qs, k_all, v_all) @ attn_w_o[j]
        else:
            qp, kp, vp = na_project(hp, na_w_qkv[j])
            k_na_list.append(kp)
            v_na_list.append(vp)
            op = blocked_attention(qp, kp, vp) @ na_w_o[j]
            qs, ks_, vs = na_project(hs, na_w_qkv[j])
            os_ = neighbourhood_attention(qs, ks_, vs, cache_k_na[:, j], cache_v_na[:, j],
                                          na_rpb[j]) @ na_w_o[j]
        xp = layer_norm(DEEPNORM_ALPHA * xp + g1p * op, ln1_g[l], ln1_b[l])
        xs = layer_norm(DEEPNORM_ALPHA * xs + g1s * os_, ln1_g[l], ln1_b[l])
        fp = moe_ffn(xp * (1 + sc2p) + sh2p, router_w[l], router_b[l],
                     moe_w1[l], moe_b1[l], moe_w2[l], moe_b2[l])
        fs = moe_ffn(xs * (1 + sc2s) + sh2s, router_w[l], router_b[l],
                     moe_w1[l], moe_b1[l], moe_w2[l], moe_b2[l])
        xp = layer_norm(DEEPNORM_ALPHA * xp + g2p * fp, ln2_g[l], ln2_b[l])
        xs = layer_norm(DEEPNORM_ALPHA * xs + g2s * fs, ln2_g[l], ln2_b[l])
    y_prompt = xp
    y_sample = xs
    new_k_attn = jnp.stack(k_attn_list, axis=1)
    new_v_attn = jnp.stack(v_attn_list, axis=1)
    new_k_na = jnp.stack(k_na_list, axis=1)
    new_v_na = jnp.stack(v_na_list, axis=1)
    return (y_prompt, y_sample, new_k_attn, new_v_attn, new_k_na, new_v_na)
```

```python
import functools

import numpy as np
import jax
import jax.numpy as jnp
from jax import lax
from jax.experimental import pallas as pl
from jax.experimental.pallas import tpu as pltpu

D_MODEL = 1024
BATCH = 16
SEQ = 256
DEPTH = 4
DEC_BATCH = 2
DEC_SEQ = 1024
PAST_LEN = 256
GRID_W = 64
HEAD_DIM = 64
N_HEADS = 16
N_KV_HEADS = 4
ROPE_THETA = 10000.0
ROPE_PAIRS = HEAD_DIM // 4
WIN_R = 8
WIN_C = 16
N_EXPERTS = 32
TOP_K = 4
D_FF = D_MODEL
SWIGLU_LIMIT = 7.0
SWIGLU_ALPHA = 1.702
DEEPNORM_ALPHA = (2 * DEPTH) ** 0.25
LN_EPS = 1e-5
RMS_EPS = 1e-6

F32 = jnp.float32
BF16 = jnp.bfloat16
I32 = jnp.int32
HIGHEST = lax.Precision.HIGHEST

N_PROMPT = BATCH * SEQ
N_SAMPLE = DEC_BATCH * DEC_SEQ
N_TOK = N_PROMPT + N_SAMPLE
N_ASSIGN = N_TOK * TOP_K

LANES = 128
SUBLANES = 8
PACKED_W = D_MODEL // 2
PACK_ROWS = PACKED_W // LANES
OUT_ROWS = D_MODEL // LANES

ROW_TILE = 256
N_ROW_TILES = (N_ASSIGN + N_EXPERTS * (ROW_TILE - 1)) // ROW_TILE + 1
N_ROWS = N_ROW_TILES * ROW_TILE
ITEM_ROWS = 1024
ITEM_TILES = ITEM_ROWS // ROW_TILE
MAX_ITEMS = N_EXPERTS + N_ASSIGN // ITEM_ROWS
ITEM_TABLE = 64
SENTINEL = N_ASSIGN
RANK_BITS = 13
GATHER_STRIDE = ROW_TILE + SUBLANES
UNROLL = 8

VMEM_LIMIT = 58 * 1024 * 1024


def _cparams(sem):
    return pltpu.CompilerParams(dimension_semantics=sem, vmem_limit_bytes=VMEM_LIMIT)


def _layer_norm(x, g, b):
    mu = jnp.mean(x, -1, keepdims=True)
    xc = x - mu
    var = jnp.mean(xc * xc, -1, keepdims=True)
    return xc * lax.rsqrt(var + LN_EPS) * g + b


def _mod_row(i, tile):
    n_prompt_tiles = N_PROMPT // tile
    return jnp.where(i < n_prompt_tiles, 0, 1 + (i - n_prompt_tiles) // (DEC_SEQ // tile))


def _mod_spec(l, tile):
    return pl.BlockSpec((None, None, 6, D_MODEL), lambda i, *_: (l, _mod_row(i, tile), 0, 0))


def _full(shape):
    nd = len(shape)
    return pl.BlockSpec(shape, lambda *_: (0,) * nd)


ADALN_TN = 1536


def _adaln_kernel(cond_ref, w_ref, b_ref, o_ref):
    c = cond_ref[...]
    s = c * jax.nn.sigmoid(c)
    o_ref[...] = jnp.dot(s, w_ref[...], precision=HIGHEST, preferred_element_type=F32) + b_ref[...]


def _adaln_all(cond8, w_mod, b_mod):
    n = 6 * D_MODEL
    out = pl.pallas_call(
        _adaln_kernel,
        grid=(DEPTH, n // ADALN_TN),
        in_specs=[
            pl.BlockSpec((8, D_MODEL), lambda l, j: (0, 0)),
            pl.BlockSpec((None, D_MODEL, ADALN_TN), lambda l, j: (l, 0, j)),
            pl.BlockSpec((None, 1, ADALN_TN), lambda l, j: (l, 0, j)),
        ],
        out_specs=pl.BlockSpec((None, 8, ADALN_TN), lambda l, j: (l, 0, j)),
        out_shape=jax.ShapeDtypeStruct((DEPTH, 8, n), F32),
        compiler_params=_cparams(("arbitrary", "arbitrary")),
        name="adaln",
    )(cond8, w_mod, b_mod.reshape(DEPTH, 1, n))
    return out.reshape(DEPTH, 8, 6, D_MODEL)


CONV_TM = 1024


def _conv_kernel(x_ref, mod_ref, win_ref, cw_ref, cb_ref, wout_ref, g_ref, b_ref, o_ref):
    i = pl.program_id(0)
    x = x_ref[...]
    h = (x * (1.0 + mod_ref[1:2, :]) + mod_ref[0:1, :]).astype(BF16)
    gc = jnp.dot(h, win_ref[:, D_MODEL:2 * D_MODEL], preferred_element_type=F32)
    xv = jnp.dot(h, win_ref[:, 2 * D_MODEL:], preferred_element_type=F32)
    u = gc * xv
    seq_len = jnp.where(i < N_PROMPT // CONV_TM, SEQ, DEC_SEQ)
    t = lax.broadcasted_iota(I32, (CONV_TM, 1), 0) & (seq_len - 1)
    u_prev = jnp.where(t == 0, 0.0, pltpu.roll(u, 1, axis=0))
    u_next = jnp.where(t == seq_len - 1, 0.0, pltpu.roll(u, CONV_TM - 1, axis=0))
    y = u_prev * cw_ref[0:1, :] + u * cw_ref[1:2, :] + u_next * cw_ref[2:3, :] + cb_ref[...]
    gb = jnp.dot(h, win_ref[:, :D_MODEL], preferred_element_type=F32)
    v = (gb * y).astype(BF16)
    o = jnp.dot(v, wout_ref[...], preferred_element_type=F32)
    o_ref[...] = _layer_norm(DEEPNORM_ALPHA * x + mod_ref[2:3, :] * o, g_ref[...], b_ref[...])


def _conv_layer(x, mod, l, w_in, cw, cb, w_out, ln_g, ln_b):
    row = pl.BlockSpec((CONV_TM, D_MODEL), lambda i: (i, 0))
    return pl.pallas_call(
        _conv_kernel,
        grid=(N_TOK // CONV_TM,),
        in_specs=[row, _mod_spec(l, CONV_TM), _full((D_MODEL, 3 * D_MODEL)), _full((3, D_MODEL)),
                  _full((1, D_MODEL)), _full((D_MODEL, D_MODEL)), _full((1, D_MODEL)), _full((1, D_MODEL))],
        out_specs=row,
        out_shape=jax.ShapeDtypeStruct((N_TOK, D_MODEL), F32),
        compiler_params=_cparams(("arbitrary",)),
        name="conv_mixer",
    )(x, mod, w_in.astype(BF16), cw, cb.reshape(1, D_MODEL), w_out.astype(BF16),
      ln_g.reshape(1, D_MODEL), ln_b.reshape(1, D_MODEL))


ATTN_CHUNK = 256


def _attn_kernel(*refs, seq_len, tq, n_kv, norm, rope, n_ctx, na, emit_kv):
    refs = list(refs)
    x_ref, mod_ref, wqkv_ref = refs[:3]
    pos = 3
    if norm:
        gq_ref, gk_ref, bd_ref = refs[pos:pos + 3]
        pos += 3
    if rope:
        cos_ref, s1_ref, s2_ref = refs[pos:pos + 3]
        pos += 3
    if n_ctx:
        ck_ref, cv_ref = refs[pos:pos + 2]
        pos += 2
    if na:
        bias_ref = refs[pos]
        pos += 1
    wo_ref, lng_ref, lnb_ref = refs[pos:pos + 3]
    pos += 3
    o_ref = refs[pos]
    pos += 1
    if emit_kv:
        nk_ref, nv_ref = refs[pos:pos + 2]
        pos += 2
    q_scr, k_scr, v_scr, o_scr = refs[pos:pos + 4]

    qt = pl.program_id(1)
    n_qt = seq_len // tq
    wq = N_HEADS * HEAD_DIM
    wk = n_kv * HEAD_DIM
    rep = N_HEADS // n_kv

    def rms(v, g_ref, width):
        sq = v * v
        hi = sq.astype(BF16)
        lo = (sq - hi.astype(F32)).astype(BF16)
        bd = bd_ref[:width, :width]
        ms = jnp.dot(hi, bd, preferred_element_type=F32) + jnp.dot(lo, bd, preferred_element_type=F32)
        return v * lax.rsqrt(ms + RMS_EPS) * g_ref[...]

    def rot(v, rows, width):
        def tab(ref):
            t = ref[rows, :]
            return jnp.concatenate([t] * (width // LANES), axis=1)

        return (v * tab(cos_ref) + pltpu.roll(v, width - ROPE_PAIRS, axis=1) * tab(s1_ref)
                + pltpu.roll(v, ROPE_PAIRS, axis=1) * tab(s2_ref))

    chunk = min(seq_len, ATTN_CHUNK)

    @pl.when(qt == 0)
    def _project():
        def body(ci, carry):
            r0 = pl.multiple_of(ci * chunk, chunk)
            rows = pl.ds(r0, chunk)
            h = (x_ref[rows, :] * (1.0 + mod_ref[1:2, :]) + mod_ref[0:1, :]).astype(BF16)
            q = jnp.dot(h, wqkv_ref[:, :wq], preferred_element_type=F32)
            k = jnp.dot(h, wqkv_ref[:, wq:wq + wk], preferred_element_type=F32)
            v = jnp.dot(h, wqkv_ref[:, wq + wk:], preferred_element_type=F32)
            if norm:
                q = rms(q, gq_ref, wq)
                k = rms(k, gk_ref, wk)
            if emit_kv:
                nk_ref[rows, :] = k
                nv_ref[rows, :] = v
            if rope:
                q = rot(q, rows, wq)
                k = rot(k, rows, wk)
            q_scr[rows, :] = (q * (HEAD_DIM ** -0.5)).astype(BF16)
            k_scr[pl.ds(n_ctx + r0, chunk), :] = k.astype(BF16)
            v_scr[pl.ds(n_ctx + r0, chunk), :] = v.astype(BF16)
            return carry

        lax.fori_loop(0, seq_len // chunk, body, 0)
        if n_ctx:
            k_scr[:n_ctx, :] = ck_ref[...].astype(BF16)
            v_scr[:n_ctx, :] = cv_ref[...].astype(BF16)

    q0 = pl.multiple_of(qt * tq, tq)
    if na:
        row_start = jnp.clip(qt - WIN_R // 2, 0, DEC_SEQ // GRID_W - WIN_R)
        d0 = row_start - qt + (WIN_R - 1)
        k0 = pl.multiple_of(n_ctx + row_start * GRID_W, GRID_W)
    for hd in range(N_HEADS):
        g = hd // rep
        hs = slice(hd * HEAD_DIM, (hd + 1) * HEAD_DIM)
        gs = slice(g * HEAD_DIM, (g + 1) * HEAD_DIM)
        qh = q_scr[pl.ds(q0, tq), hs]
        dn = (((1,), (1,)), ((), ()))
        if na:
            kc, vc = k_scr[:n_ctx, gs], v_scr[:n_ctx, gs]
            kl, vl = k_scr[pl.ds(k0, WIN_R * GRID_W), gs], v_scr[pl.ds(k0, WIN_R * GRID_W), gs]
            bias = jnp.concatenate([bias_ref[hd, pl.ds(d0 + 2 * j, 1)][0] for j in range(WIN_R // 2)], axis=1)
            s = jnp.concatenate([lax.dot_general(qh, kc, dn, preferred_element_type=F32),
                                 lax.dot_general(qh, kl, dn, preferred_element_type=F32) + bias], axis=1)
        else:
            s = lax.dot_general(qh, k_scr[:, gs], dn, preferred_element_type=F32)
        e = jnp.exp(s - jnp.max(s, axis=1, keepdims=True))
        den = jnp.sum(e, axis=1, keepdims=True)
        eb = e.astype(BF16)
        if na:
            oh = (jnp.dot(eb[:, :n_ctx], vc, preferred_element_type=F32)
                  + jnp.dot(eb[:, n_ctx:], vl, preferred_element_type=F32))
        else:
            oh = jnp.dot(eb, v_scr[:, gs], preferred_element_type=F32)
        o_scr[pl.ds(q0, tq), hs] = (oh / den).astype(BF16)

    @pl.when(qt == n_qt - 1)
    def _finish():
        def body(ci, carry):
            rows = pl.ds(pl.multiple_of(ci * chunk, chunk), chunk)
            o = jnp.dot(o_scr[rows, :], wo_ref[...], preferred_element_type=F32)
            o_ref[rows, :] = _layer_norm(DEEPNORM_ALPHA * x_ref[rows, :] + mod_ref[2:3, :] * o,
                                         lng_ref[...], lnb_ref[...])
            return carry

        lax.fori_loop(0, seq_len // chunk, body, 0)


def _attn_call(x, mod, l, latent, w_qkv, w_o, ln_g, ln_b, *, n_seq, seq_len, tq, n_kv, norm_args=None,
               rope_args=None, ctx_args=None, bias=None, emit_kv=False, name="attn"):
    wq = N_HEADS * HEAD_DIM
    wk = n_kv * HEAD_DIM
    n_ctx = PAST_LEN if ctx_args is not None else 0
    n_qt = seq_len // tq
    seq_off = N_PROMPT // seq_len if latent else 0
    seq_spec = pl.BlockSpec((seq_len, D_MODEL), lambda s, t: (s, 0))
    mod_spec = pl.BlockSpec((None, None, 6, D_MODEL), lambda s, t: (l, (1 + s) if latent else 0, 0, 0))
    in_specs = [pl.BlockSpec((seq_len, D_MODEL), lambda s, t: (s + seq_off, 0)), mod_spec,
                pl.BlockSpec((D_MODEL, wq + 2 * wk), lambda s, t: (0, 0))]
    args = [x, mod, w_qkv.astype(BF16)]
    if norm_args is not None:
        gq, gk, bd = norm_args
        in_specs += [pl.BlockSpec((1, wq), lambda s, t: (0, 0)), pl.BlockSpec((1, wk), lambda s, t: (0, 0)),
                     pl.BlockSpec((wq, wq), lambda s, t: (0, 0))]
        args += [gq, gk, bd]
    if rope_args is not None:
        in_specs += [pl.BlockSpec((seq_len, LANES), lambda s, t: (0, 0))] * 3
        args += list(rope_args)
    if ctx_args is not None:
        in_specs += [pl.BlockSpec((None, n_ctx, wk), lambda s, t: (s, 0, 0))] * 2
        args += list(ctx_args)
    if bias is not None:
        in_specs += [pl.BlockSpec(bias.shape, lambda s, t: (0, 0, 0, 0))]
        args += [bias]
    in_specs += [pl.BlockSpec((D_MODEL, D_MODEL), lambda s, t: (0, 0)),
                 pl.BlockSpec((1, D_MODEL), lambda s, t: (0, 0)), pl.BlockSpec((1, D_MODEL), lambda s, t: (0, 0))]
    args += [w_o.astype(BF16), ln_g.reshape(1, D_MODEL), ln_b.reshape(1, D_MODEL)]
    out_specs = [seq_spec]
    out_shape = [jax.ShapeDtypeStruct((n_seq * seq_len, D_MODEL), F32)]
    if emit_kv:
        out_specs += [pl.BlockSpec((seq_len, wk), lambda s, t: (s, 0))] * 2
        out_shape += [jax.ShapeDtypeStruct((n_seq * seq_len, wk), F32)] * 2
    kern = functools.partial(_attn_kernel, seq_len=seq_len, tq=tq, n_kv=n_kv, norm=norm_args is not None,
                             rope=rope_args is not None, n_ctx=n_ctx, na=bias is not None, emit_kv=emit_kv)
    return pl.pallas_call(
        kern,
        grid=(n_seq, n_qt),
        in_specs=in_specs,
        out_specs=out_specs,
        out_shape=out_shape,
        scratch_shapes=[pltpu.VMEM((seq_len, wq), BF16), pltpu.VMEM((n_ctx + seq_len, wk), BF16),
                        pltpu.VMEM((n_ctx + seq_len, wk), BF16), pltpu.VMEM((seq_len, wq), BF16)],
        compiler_params=_cparams(("arbitrary", "arbitrary")),
        name=name,
    )(*args)


def _rope_tables():
    t = np.arange(DEC_SEQ)
    pos = np.stack([t // GRID_W, t % GRID_W], axis=1).astype(np.float64)
    inv = (ROPE_THETA ** (-np.arange(ROPE_PAIRS, dtype=np.float32) / ROPE_PAIRS)).astype(np.float64)
    ang = pos[:, :, None] * inv[None, None, :]
    cos, sin = np.cos(ang), np.sin(ang)
    zero = np.zeros_like(sin)
    cos_t = np.concatenate([cos, cos], axis=2).reshape(DEC_SEQ, HEAD_DIM)
    s1_t = np.concatenate([-sin, zero], axis=2).reshape(DEC_SEQ, HEAD_DIM)
    s2_t = np.concatenate([zero, sin], axis=2).reshape(DEC_SEQ, HEAD_DIM)
    return tuple(jnp.asarray(np.tile(a, (1, LANES // HEAD_DIM)), F32) for a in (cos_t, s1_t, s2_t))


def _block_diag_mean():
    a = np.kron(np.eye(N_HEADS), np.full((HEAD_DIM, HEAD_DIM), 1.0 / HEAD_DIM))
    return jnp.asarray(a, BF16)


def _na_bias_table(rpb):
    cols = np.arange(GRID_W)
    col_start = np.clip(cols - WIN_C // 2, 0, GRID_W - WIN_C)
    kc = np.arange(GRID_W)
    inside = (kc[None, :] >= col_start[:, None]) & (kc[None, :] < col_start[:, None] + WIN_C)
    off = np.clip(kc[None, :] - cols[:, None] + (WIN_C - 1), 0, 2 * WIN_C - 2)
    blocks = jnp.where(jnp.asarray(inside)[None, None], rpb[:, :, off], -1e30)
    return jnp.concatenate([blocks[:, :-1], blocks[:, 1:]], axis=-1)


ROUTER_TM = 256


def _router_kernel(x_ref, mod_ref, rw_ref, rb_ref, tri_ref, hp_ref, code_ref, wt_ref, cnt_ref, carry_ref):
    i = pl.program_id(0)

    @pl.when(i == 0)
    def _init():
        carry_ref[...] = jnp.zeros_like(carry_ref)

    h2 = x_ref[...] * (1.0 + mod_ref[4:5, :]) + mod_ref[3:4, :]
    logits = jnp.dot(h2, rw_ref[...], precision=HIGHEST, preferred_element_type=F32) + rb_ref[...]
    lane = lax.broadcasted_iota(I32, (ROUTER_TM, N_EXPERTS), 1)
    vals, idxs, hots = [], [], []
    cur = logits
    for _ in range(TOP_K):
        m = jnp.max(cur, axis=1, keepdims=True)
        idx = jnp.min(jnp.where(cur == m, lane, N_EXPERTS), axis=1, keepdims=True)
        hot = lane == idx
        vals.append(m)
        idxs.append(idx)
        hots.append(hot)
        cur = jnp.where(hot, -jnp.inf, cur)
    exps = [jnp.exp(v - vals[0]) for v in vals]
    den = (exps[0] + exps[1]) + (exps[2] + exps[3])
    mask = jnp.zeros((ROUTER_TM, N_EXPERTS), F32)
    for hot in hots:
        mask = mask + jnp.where(hot, 1.0, 0.0)
    before = jnp.dot(tri_ref[...], mask.astype(BF16), preferred_element_type=F32) + carry_ref[...]
    lane128 = lax.broadcasted_iota(I32, (ROUTER_TM, LANES), 1)
    code = jnp.zeros((ROUTER_TM, LANES), I32)
    wts = jnp.zeros((ROUTER_TM, LANES), F32)
    for k in range(TOP_K):
        rank = jnp.sum(jnp.where(hots[k], before, 0.0), axis=1, keepdims=True).astype(I32)
        code = jnp.where(lane128 == k, idxs[k] * (1 << RANK_BITS) + rank, code)
        wts = jnp.where(lane128 == k, exps[k] / den, wts)
    code_ref[...] = code
    wt_ref[...] = wts
    carry_ref[...] = carry_ref[...] + jnp.sum(mask, axis=0, keepdims=True)
    cnt_ref[...] = carry_ref[...]
    packed = pltpu.pack_elementwise([h2[:, :PACKED_W], h2[:, PACKED_W:]], packed_dtype=BF16)
    for c in range(PACK_ROWS):
        hp_ref[pl.ds(c, ROUTER_TM, stride=PACK_ROWS), :] = packed[:, c * LANES:(c + 1) * LANES]


def _router(x1, mod, l, rw, rb):
    tri = jnp.asarray(np.tril(np.ones((ROUTER_TM, ROUTER_TM)), -1), BF16)
    return pl.pallas_call(
        _router_kernel,
        grid=(N_TOK // ROUTER_TM,),
        in_specs=[pl.BlockSpec((ROUTER_TM, D_MODEL), lambda i: (i, 0)), _mod_spec(l, ROUTER_TM),
                  _full((D_MODEL, N_EXPERTS)), _full((1, N_EXPERTS)), _full((ROUTER_TM, ROUTER_TM))],
        out_specs=[pl.BlockSpec((ROUTER_TM * PACK_ROWS, LANES), lambda i: (i, 0)),
                   pl.BlockSpec((ROUTER_TM, LANES), lambda i: (i, 0)),
                   pl.BlockSpec((ROUTER_TM, LANES), lambda i: (i, 0)),
                   _full((1, N_EXPERTS))],
        out_shape=[jax.ShapeDtypeStruct((N_TOK * PACK_ROWS, LANES), I32),
                   jax.ShapeDtypeStruct((N_TOK, LANES), I32),
                   jax.ShapeDtypeStruct((N_TOK, LANES), F32),
                   jax.ShapeDtypeStruct((1, N_EXPERTS), F32)],
        scratch_shapes=[pltpu.VMEM((1, N_EXPERTS), F32)],
        compiler_params=_cparams(("arbitrary",)),
        name="router",
    )(x1, mod, rw, rb.reshape(1, N_EXPERTS), tri)


def _perm_kernel(code_ref, cnt_ref, slot_ref, iexp_ref, irow_ref, inr_ref, meta_ref, base_ref):
    def per_expert(e, carry):
        off, item, last_e, last_row = carry
        n = cnt_ref[e]
        base_ref[e] = off
        padded = ((n + (ROW_TILE - 1)) // ROW_TILE) * ROW_TILE

        def pad(r, c):
            slot_ref[off + r] = SENTINEL
            return c

        lax.fori_loop(n, padded, pad, 0)
        n_items = (n + (ITEM_ROWS - 1)) // ITEM_ROWS

        def add_item(c, it):
            iexp_ref[it] = e
            irow_ref[it] = off + c * ITEM_ROWS
            inr_ref[it] = jnp.minimum(ITEM_ROWS, n - c * ITEM_ROWS)
            return it + 1

        item2 = lax.fori_loop(0, n_items, add_item, item)
        last_e = jnp.where(n_items > 0, e, last_e)
        last_row = jnp.where(n_items > 0, off + (n_items - 1) * ITEM_ROWS, last_row)
        return off + padded, item2, last_e, last_row

    zero = jnp.int32(0)
    off, item, last_e, last_row = lax.fori_loop(0, N_EXPERTS, per_expert, (zero, zero, zero, zero))

    def idle(it, c):
        iexp_ref[it] = last_e
        irow_ref[it] = last_row
        inr_ref[it] = 0
        return c

    lax.fori_loop(item, ITEM_TABLE, idle, 0)

    def tail(r, c):
        slot_ref[r] = SENTINEL
        return c

    lax.fori_loop(off, N_ROWS, tail, 0)
    meta_ref[0] = off // ROW_TILE
    meta_ref[1] = item
    for m in range(2, 8):
        meta_ref[m] = 0

    def place(j, c):
        for u in range(UNROLL):
            a = j * UNROLL + u
            cd = code_ref[a]
            slot_ref[base_ref[cd >> RANK_BITS] + (cd & ((1 << RANK_BITS) - 1))] = a
        return c

    lax.fori_loop(0, N_ASSIGN // UNROLL, place, 0)


def _perm(code_flat, counts):
    smem = pl.BlockSpec(memory_space=pltpu.SMEM)
    return pl.pallas_call(
        _perm_kernel,
        in_specs=[smem, smem],
        out_specs=[smem] * 5,
        out_shape=[jax.ShapeDtypeStruct((N_ROWS,), I32), jax.ShapeDtypeStruct((ITEM_TABLE,), I32),
                   jax.ShapeDtypeStruct((ITEM_TABLE,), I32), jax.ShapeDtypeStruct((ITEM_TABLE,), I32),
                   jax.ShapeDtypeStruct((8,), I32)],
        scratch_shapes=[pltpu.SMEM((N_EXPERTS,), I32)],
        name="moe_perm",
    )(code_flat, counts)


def _gather_kernel(slot_ref, meta_ref, hflat_ref, xs_ref, t_ref):
    i = pl.program_id(0)

    @pl.when(i < meta_ref[0])
    def _gather():
        def body(j, c):
            for u in range(UNROLL):
                r = j * UNROLL + u
                tok = jnp.minimum(slot_ref[i * ROW_TILE + r] >> 2, N_TOK - 1)
                src = pl.multiple_of(tok * PACK_ROWS, PACK_ROWS)
                t_ref[pl.ds(r, PACK_ROWS, stride=GATHER_STRIDE), :] = hflat_ref[pl.ds(src, PACK_ROWS), :]
            return c

        lax.fori_loop(0, ROW_TILE // UNROLL, body, 0)
        for c in range(PACK_ROWS):
            xs_ref[:, c * LANES:(c + 1) * LANES] = t_ref[c * GATHER_STRIDE:c * GATHER_STRIDE + ROW_TILE, :]

    @pl.when(i >= meta_ref[0])
    def _unused():
        xs_ref[...] = jnp.zeros_like(xs_ref)


def _gather(slot, meta, hflat):
    return pl.pallas_call(
        _gather_kernel,
        grid_spec=pltpu.PrefetchScalarGridSpec(
            num_scalar_prefetch=2,
            grid=(N_ROW_TILES,),
            in_specs=[pl.BlockSpec((N_TOK * PACK_ROWS, LANES), lambda i, *_: (0, 0))],
            out_specs=pl.BlockSpec((ROW_TILE, PACKED_W), lambda i, *_: (i, 0)),
            scratch_shapes=[pltpu.VMEM((PACK_ROWS * GATHER_STRIDE, LANES), I32)],
        ),
        out_shape=jax.ShapeDtypeStruct((N_ROWS, PACKED_W), I32),
        compiler_params=_cparams(("arbitrary",)),
        name="moe_gather",
    )(slot, meta, hflat)


def _unpack_rows(p):
    lo = pltpu.unpack_elementwise(p, index=0, packed_dtype=BF16, unpacked_dtype=F32)
    hi = pltpu.unpack_elementwise(p, index=1, packed_dtype=BF16, unpacked_dtype=F32)
    return jnp.concatenate([lo.astype(BF16), hi.astype(BF16)], axis=1)


def _expert_kernel(slot_ref, wflat_ref, iexp_ref, irow_ref, inr_ref, x0_ref, x1_ref, x2_ref, x3_ref,
                   w1_ref, w2_ref, b1_ref, b2_ref, f_hbm, acc_ref, g_ref, a_ref, yt_ref, sem):
    it = pl.program_id(0)
    ph = pl.program_id(1)
    n = inr_ref[it]
    row0 = irow_ref[it]
    x_refs = (x0_ref, x1_ref, x2_ref, x3_ref)

    @pl.when((it == 0) & (ph == 0))
    def _zero():
        acc_ref[...] = jnp.zeros_like(acc_ref)

    for k in range(ITEM_TILES):
        rows = slice(k * ROW_TILE, (k + 1) * ROW_TILE)
        active = k * ROW_TILE < n

        @pl.when(active & (ph == 0))
        def _gate(k=k, rows=rows):
            x = _unpack_rows(x_refs[k][...])
            g_ref[rows, :] = (jnp.dot(x, w1_ref[...].astype(BF16), preferred_element_type=F32)
                              + b1_ref[:, :D_FF])

        @pl.when(active & (ph == 1))
        def _up(k=k, rows=rows):
            x = _unpack_rows(x_refs[k][...])
            up = jnp.dot(x, w1_ref[...].astype(BF16), preferred_element_type=F32) + b1_ref[:, D_FF:]
            gate = jnp.minimum(g_ref[rows, :], SWIGLU_LIMIT)
            up = jnp.clip(up, -SWIGLU_LIMIT, SWIGLU_LIMIT)
            a_ref[rows, :] = (gate * jax.nn.sigmoid(SWIGLU_ALPHA * gate) * (up + 1.0)).astype(BF16)

        @pl.when(active & (ph == 2))
        def _down(k=k, rows=rows):
            y = jnp.dot(a_ref[rows, :], w2_ref[...].astype(BF16), preferred_element_type=F32) + b2_ref[...]
            for c in range(OUT_ROWS):
                yt_ref[pl.ds(c, ROW_TILE, stride=OUT_ROWS), :] = y[:, c * LANES:(c + 1) * LANES]
            n_here = jnp.minimum(n - k * ROW_TILE, ROW_TILE)

            def body(j, carry):
                dsts, vals = [], []
                for u in range(UNROLL):
                    r = j * UNROLL + u
                    s = slot_ref[row0 + k * ROW_TILE + r]
                    dst = pl.multiple_of((s >> 2) * OUT_ROWS, OUT_ROWS)
                    src = pl.multiple_of(r * OUT_ROWS, OUT_ROWS)
                    dsts.append(dst)
                    vals.append(acc_ref[pl.ds(dst, OUT_ROWS), :] + wflat_ref[s] * yt_ref[pl.ds(src, OUT_ROWS), :])
                for dst, val in zip(dsts, vals):
                    acc_ref[pl.ds(dst, OUT_ROWS), :] = val
                return carry

            lax.fori_loop(0, (n_here + (UNROLL - 1)) // UNROLL, body, 0)

    @pl.when((it == MAX_ITEMS - 1) & (ph == 2))
    def _flush():
        cp = pltpu.make_async_copy(acc_ref.at[pl.ds(0, N_TOK * OUT_ROWS)], f_hbm, sem)
        cp.start()
        cp.wait()


def _experts(slot, wflat, iexp, irow, inr, xs, w1, b1, w2, b2):
    def x_spec(k):
        return pl.BlockSpec((ROW_TILE, PACKED_W),
                            lambda it, ph, slot, wf, ie, ir, nr: (jnp.minimum(ir[it] // ROW_TILE + k, N_ROW_TILES - 1), 0))

    return pl.pallas_call(
        _expert_kernel,
        grid_spec=pltpu.PrefetchScalarGridSpec(
            num_scalar_prefetch=5,
            grid=(MAX_ITEMS, 3),
            in_specs=[x_spec(0), x_spec(1), x_spec(2), x_spec(3),
                      pl.BlockSpec((None, D_MODEL, D_FF), lambda it, ph, slot, wf, ie, ir, nr: (ie[it], 0, jnp.minimum(ph, 1))),
                      pl.BlockSpec((None, D_FF, D_MODEL), lambda it, ph, slot, wf, ie, ir, nr: (ie[it], 0, 0)),
                      pl.BlockSpec((None, 1, 2 * D_FF), lambda it, ph, slot, wf, ie, ir, nr: (ie[it], 0, 0)),
                      pl.BlockSpec((None, 1, D_MODEL), lambda it, ph, slot, wf, ie, ir, nr: (ie[it], 0, 0))],
            out_specs=pl.BlockSpec(memory_space=pl.ANY),
            scratch_shapes=[pltpu.VMEM(((N_TOK + 1) * OUT_ROWS, LANES), F32),
                            pltpu.VMEM((ITEM_ROWS, D_FF), F32),
                            pltpu.VMEM((ITEM_ROWS, D_FF), BF16),
                            pltpu.VMEM((ROW_TILE * OUT_ROWS, LANES), F32),
                            pltpu.SemaphoreType.DMA(())],
        ),
        out_shape=jax.ShapeDtypeStruct((N_TOK * OUT_ROWS, LANES), F32),
        compiler_params=_cparams(("arbitrary", "arbitrary")),
        name="moe_experts",
    )(slot, wflat, iexp, irow, inr, xs, xs, xs, xs, w1, w2,
      b1.reshape(N_EXPERTS, 1, 2 * D_FF), b2.reshape(N_EXPERTS, 1, D_MODEL))


LN2_TM = 256


def _ln2_kernel(x_ref, f_ref, mod_ref, g_ref, b_ref, o_ref):
    f = jnp.concatenate([f_ref[pl.ds(c, LN2_TM, stride=OUT_ROWS), :] for c in range(OUT_ROWS)], axis=1)
    o_ref[...] = _layer_norm(DEEPNORM_ALPHA * x_ref[...] + mod_ref[5:6, :] * f, g_ref[...], b_ref[...])


def _ln2(x1, f_flat, mod, l, g, b):
    row = pl.BlockSpec((LN2_TM, D_MODEL), lambda i: (i, 0))
    return pl.pallas_call(
        _ln2_kernel,
        grid=(N_TOK // LN2_TM,),
        in_specs=[row, pl.BlockSpec((LN2_TM * OUT_ROWS, LANES), lambda i: (i, 0)), _mod_spec(l, LN2_TM),
                  _full((1, D_MODEL)), _full((1, D_MODEL))],
        out_specs=row,
        out_shape=jax.ShapeDtypeStruct((N_TOK, D_MODEL), F32),
        compiler_params=_cparams(("arbitrary",)),
        name="ln2",
    )(x1, f_flat, mod, g.reshape(1, D_MODEL), b.reshape(1, D_MODEL))


def _moe_layer(x1, mod, l, rw, rb, w1, b1, w2, b2, ln_g, ln_b):
    hflat, code, wts, counts = _router(x1, mod, l, rw, rb)
    code_flat = code[:, :TOP_K].reshape(N_ASSIGN)
    wflat = jnp.concatenate([wts[:, :TOP_K].reshape(N_ASSIGN), jnp.zeros((8,), F32)])
    slot, iexp, irow, inr, meta = _perm(code_flat, counts.astype(I32).reshape(N_EXPERTS))
    xs = _gather(slot, meta, hflat)
    f_flat = _experts(slot, wflat, iexp, irow, inr, xs, w1, b1, w2, b2)
    return _ln2(x1, f_flat, mod, l, ln_g, ln_b)


def kernel(x_prompt, x_sample, cache_k_attn, cache_v_attn, cache_k_na, cache_v_na, c, c_ctx, w_mod, b_mod, ln1_g, ln1_b, ln2_g, ln2_b, conv_w_in, conv_w, conv_b, conv_w_out, attn_w_qkv, attn_q_norm, attn_k_norm, attn_w_o, na_w_qkv, na_rpb, na_w_o, router_w, router_b, moe_w1, moe_b1, moe_w2, moe_b2):
    x = jnp.concatenate([x_prompt.reshape(N_PROMPT, D_MODEL), x_sample.reshape(N_SAMPLE, D_MODEL)], axis=0)
    cond8 = jnp.concatenate([c_ctx[None, :], c, jnp.zeros((8 - 1 - DEC_BATCH, D_MODEL), F32)], axis=0)
    mod = _adaln_all(cond8, w_mod, b_mod)
    new_kv = {}
    for l in range(DEPTH):
        kind, j = l % 3, l // 3
        if kind == 0:
            x1 = _conv_layer(x, mod, l, conv_w_in[j], conv_w[j], conv_b[j], conv_w_out[j], ln1_g[l], ln1_b[l])
        elif kind == 1:
            gq = jnp.tile(attn_q_norm[j], N_HEADS).reshape(1, N_HEADS * HEAD_DIM)
            gk = jnp.tile(attn_k_norm[j], N_KV_HEADS).reshape(1, N_KV_HEADS * HEAD_DIM)
            norm_args = (gq, gk, _block_diag_mean())
            wk = N_KV_HEADS * HEAD_DIM
            xp1, nk, nv = _attn_call(x, mod, l, False, attn_w_qkv[j], attn_w_o[j], ln1_g[l], ln1_b[l],
                                     n_seq=BATCH, seq_len=SEQ, tq=SEQ, n_kv=N_KV_HEADS, norm_args=norm_args,
                                     emit_kv=True, name="gqa_prompt")
            new_kv["k_attn"] = nk.reshape(BATCH, 1, SEQ, N_KV_HEADS, HEAD_DIM)
            new_kv["v_attn"] = nv.reshape(BATCH, 1, SEQ, N_KV_HEADS, HEAD_DIM)
            (xs1,) = _attn_call(x, mod, l, True, attn_w_qkv[j], attn_w_o[j], ln1_g[l], ln1_b[l],
                                n_seq=DEC_BATCH, seq_len=DEC_SEQ, tq=256, n_kv=N_KV_HEADS, norm_args=norm_args,
                                rope_args=_rope_tables(),
                                ctx_args=(cache_k_attn[:, j].reshape(DEC_BATCH, PAST_LEN, wk),
                                          cache_v_attn[:, j].reshape(DEC_BATCH, PAST_LEN, wk)),
                                name="gqa_sample")
            x1 = jnp.concatenate([xp1, xs1], axis=0)
        else:
            wk = N_HEADS * HEAD_DIM
            xp1, nk, nv = _attn_call(x, mod, l, False, na_w_qkv[j], na_w_o[j], ln1_g[l], ln1_b[l],
                                     n_seq=BATCH, seq_len=SEQ, tq=SEQ, n_kv=N_HEADS, emit_kv=True,
                                     name="mha_prompt")
            new_kv["k_na"] = nk.reshape(BATCH, 1, SEQ, N_HEADS, HEAD_DIM)
            new_kv["v_na"] = nv.reshape(BATCH, 1, SEQ, N_HEADS, HEAD_DIM)
            (xs1,) = _attn_call(x, mod, l, True, na_w_qkv[j], na_w_o[j], ln1_g[l], ln1_b[l],
                                n_seq=DEC_BATCH, seq_len=DEC_SEQ, tq=GRID_W, n_kv=N_HEADS,
                                ctx_args=(cache_k_na[:, j].reshape(DEC_BATCH, PAST_LEN, wk),
                                          cache_v_na[:, j].reshape(DEC_BATCH, PAST_LEN, wk)),
                                bias=_na_bias_table(na_rpb[j]), name="na_sample")
            x1 = jnp.concatenate([xp1, xs1], axis=0)
        x = _moe_layer(x1, mod, l, router_w[l], router_b[l], moe_w1[l], moe_b1[l], moe_w2[l], moe_b2[l],
                       ln2_g[l], ln2_b[l])
    y_prompt = x[:N_PROMPT].reshape(BATCH, SEQ, D_MODEL)
    y_sample = x[N_PROMPT:].reshape(DEC_BATCH, DEC_SEQ, D_MODEL)
    return (y_prompt, y_sample, new_kv["k_attn"], new_kv["v_attn"], new_kv["k_na"], new_kv["v_na"])
```

```python
import functools

import numpy as np
import jax
import jax.numpy as jnp
from jax import lax
from jax.experimental import pallas as pl
from jax.experimental.pallas import tpu as pltpu

D_MODEL = 1024
BATCH = 16
SEQ = 256
DEPTH = 4
DEC_BATCH = 2
DEC_SEQ = 1024
PAST_LEN = 256
GRID_W = 64
HEAD_DIM = 64
N_HEADS = 16
N_KV_HEADS = 4
ROPE_THETA = 10000.0
ROPE_PAIRS = HEAD_DIM // 4
WIN_R = 8
WIN_C = 16
N_EXPERTS = 32
TOP_K = 4
D_FF = D_MODEL
SWIGLU_LIMIT = 7.0
SWIGLU_ALPHA = 1.702
DEEPNORM_ALPHA = (2 * DEPTH) ** 0.25
LN_EPS = 1e-5
RMS_EPS = 1e-6

F32 = jnp.float32
BF16 = jnp.bfloat16
I32 = jnp.int32
HIGHEST = lax.Precision.HIGHEST

N_PROMPT = BATCH * SEQ
N_SAMPLE = DEC_BATCH * DEC_SEQ
N_TOK = N_PROMPT + N_SAMPLE
N_ASSIGN = N_TOK * TOP_K

LANES = 128
SUBLANES = 8
PACKED_W = D_MODEL // 2
PACK_ROWS = PACKED_W // LANES
OUT_ROWS = D_MODEL // LANES

ROW_TILE = 256
N_ROW_TILES = (N_ASSIGN + N_EXPERTS * (ROW_TILE - 1)) // ROW_TILE + 1
N_ROWS = N_ROW_TILES * ROW_TILE
ITEM_ROWS = 1024
ITEM_TILES = ITEM_ROWS // ROW_TILE
MAX_ITEMS = N_EXPERTS + N_ASSIGN // ITEM_ROWS
ITEM_TABLE = 64
SENTINEL = N_ASSIGN
RANK_BITS = 13
GATHER_STRIDE = ROW_TILE + SUBLANES
UNROLL = 8

VMEM_LIMIT = 58 * 1024 * 1024


def _cparams(sem):
    return pltpu.CompilerParams(dimension_semantics=sem, vmem_limit_bytes=VMEM_LIMIT)


def _layer_norm(x, g, b):
    mu = jnp.mean(x, -1, keepdims=True)
    xc = x - mu
    var = jnp.mean(xc * xc, -1, keepdims=True)
    return xc * lax.rsqrt(var + LN_EPS) * g + b


def _mod_row(i, tile):
    n_prompt_tiles = N_PROMPT // tile
    return jnp.where(i < n_prompt_tiles, 0, 1 + (i - n_prompt_tiles) // (DEC_SEQ // tile))


def _mod_spec(l, tile):
    return pl.BlockSpec((None, None, 6, D_MODEL), lambda i, *_: (l, _mod_row(i, tile), 0, 0))


def _full(shape):
    nd = len(shape)
    return pl.BlockSpec(shape, lambda *_: (0,) * nd)


ADALN_TN = 1536


def _adaln_kernel(cond_ref, w_ref, b_ref, o_ref):
    c = cond_ref[...]
    s = c * jax.nn.sigmoid(c)
    o_ref[...] = jnp.dot(s, w_ref[...], precision=HIGHEST, preferred_element_type=F32) + b_ref[...]


def _adaln_all(cond8, w_mod, b_mod):
    n = 6 * D_MODEL
    out = pl.pallas_call(
        _adaln_kernel,
        grid=(DEPTH, n // ADALN_TN),
        in_specs=[
            pl.BlockSpec((8, D_MODEL), lambda l, j: (0, 0)),
            pl.BlockSpec((None, D_MODEL, ADALN_TN), lambda l, j: (l, 0, j)),
            pl.BlockSpec((None, 1, ADALN_TN), lambda l, j: (l, 0, j)),
        ],
        out_specs=pl.BlockSpec((None, 8, ADALN_TN), lambda l, j: (l, 0, j)),
        out_shape=jax.ShapeDtypeStruct((DEPTH, 8, n), F32),
        compiler_params=_cparams(("arbitrary", "arbitrary")),
        name="adaln",
    )(cond8, w_mod, b_mod.reshape(DEPTH, 1, n))
    return out.reshape(DEPTH, 8, 6, D_MODEL)


CONV_TM = 1024


def _conv_kernel(x_ref, mod_ref, win_ref, cw_ref, cb_ref, wout_ref, g_ref, b_ref, o_ref):
    i = pl.program_id(0)
    x = x_ref[...]
    h = (x * (1.0 + mod_ref[1:2, :]) + mod_ref[0:1, :]).astype(BF16)
    gc = jnp.dot(h, win_ref[:, D_MODEL:2 * D_MODEL], preferred_element_type=F32)
    xv = jnp.dot(h, win_ref[:, 2 * D_MODEL:], preferred_element_type=F32)
    u = gc * xv
    seq_len = jnp.where(i < N_PROMPT // CONV_TM, SEQ, DEC_SEQ)
    t = lax.broadcasted_iota(I32, (CONV_TM, 1), 0) & (seq_len - 1)
    u_prev = jnp.where(t == 0, 0.0, pltpu.roll(u, 1, axis=0))
    u_next = jnp.where(t == seq_len - 1, 0.0, pltpu.roll(u, CONV_TM - 1, axis=0))
    y = u_prev * cw_ref[0:1, :] + u * cw_ref[1:2, :] + u_next * cw_ref[2:3, :] + cb_ref[...]
    gb = jnp.dot(h, win_ref[:, :D_MODEL], preferred_element_type=F32)
    v = (gb * y).astype(BF16)
    o = jnp.dot(v, wout_ref[...], preferred_element_type=F32)
    o_ref[...] = _layer_norm(DEEPNORM_ALPHA * x + mod_ref[2:3, :] * o, g_ref[...], b_ref[...])


def _conv_layer(x, mod, l, w_in, cw, cb, w_out, ln_g, ln_b):
    row = pl.BlockSpec((CONV_TM, D_MODEL), lambda i: (i, 0))
    return pl.pallas_call(
        _conv_kernel,
        grid=(N_TOK // CONV_TM,),
        in_specs=[row, _mod_spec(l, CONV_TM), _full((D_MODEL, 3 * D_MODEL)), _full((3, D_MODEL)),
                  _full((1, D_MODEL)), _full((D_MODEL, D_MODEL)), _full((1, D_MODEL)), _full((1, D_MODEL))],
        out_specs=row,
        out_shape=jax.ShapeDtypeStruct((N_TOK, D_MODEL), F32),
        compiler_params=_cparams(("arbitrary",)),
        name="conv_mixer",
    )(x, mod, w_in.astype(BF16), cw, cb.reshape(1, D_MODEL), w_out.astype(BF16),
      ln_g.reshape(1, D_MODEL), ln_b.reshape(1, D_MODEL))


ATTN_CHUNK = 256


def _attn_kernel(*refs, seq_len, tq, n_kv, norm, rope, n_ctx, na, emit_kv):
    refs = list(refs)
    x_ref, mod_ref, wqkv_ref = refs[:3]
    pos = 3
    if norm:
        gq_ref, gk_ref, bd_ref = refs[pos:pos + 3]
        pos += 3
    if rope:
        cos_ref, s1_ref, s2_ref = refs[pos:pos + 3]
        pos += 3
    if n_ctx:
        ck_ref, cv_ref = refs[pos:pos + 2]
        pos += 2
    if na:
        bias_ref = refs[pos]
        pos += 1
    wo_ref, lng_ref, lnb_ref = refs[pos:pos + 3]
    pos += 3
    o_ref = refs[pos]
    pos += 1
    if emit_kv:
        nk_ref, nv_ref = refs[pos:pos + 2]
        pos += 2
    q_scr, k_scr, v_scr, o_scr = refs[pos:pos + 4]

    qt = pl.program_id(1)
    n_qt = seq_len // tq
    wq = N_HEADS * HEAD_DIM
    wk = n_kv * HEAD_DIM
    rep = N_HEADS // n_kv

    def rms(v, g_ref, width):
        sq = v * v
        hi = sq.astype(BF16)
        lo = (sq - hi.astype(F32)).astype(BF16)
        bd = bd_ref[:width, :width]
        ms = jnp.dot(hi, bd, preferred_element_type=F32) + jnp.dot(lo, bd, preferred_element_type=F32)
        return v * lax.rsqrt(ms + RMS_EPS) * g_ref[...]

    def rot(v, rows, width):
        def tab(ref):
            t = ref[rows, :]
            return jnp.concatenate([t] * (width // LANES), axis=1)

        return (v * tab(cos_ref) + pltpu.roll(v, width - ROPE_PAIRS, axis=1) * tab(s1_ref)
                + pltpu.roll(v, ROPE_PAIRS, axis=1) * tab(s2_ref))

    chunk = min(seq_len, ATTN_CHUNK)

    @pl.when(qt == 0)
    def _project():
        def body(ci, carry):
            r0 = pl.multiple_of(ci * chunk, chunk)
            rows = pl.ds(r0, chunk)
            h = (x_ref[rows, :] * (1.0 + mod_ref[1:2, :]) + mod_ref[0:1, :]).astype(BF16)
            q = jnp.dot(h, wqkv_ref[:, :wq], preferred_element_type=F32)
            k = jnp.dot(h, wqkv_ref[:, wq:wq + wk], preferred_element_type=F32)
            v = jnp.dot(h, wqkv_ref[:, wq + wk:], preferred_element_type=F32)
            if norm:
                q = rms(q, gq_ref, wq)
                k = rms(k, gk_ref, wk)
            if emit_kv:
                nk_ref[rows, :] = k
                nv_ref[rows, :] = v
            if rope:
                q = rot(q, rows, wq)
                k = rot(k, rows, wk)
            q_scr[rows, :] = (q * (HEAD_DIM ** -0.5)).astype(BF16)
            k_scr[pl.ds(n_ctx + r0, chunk), :] = k.astype(BF16)
            v_scr[pl.ds(n_ctx + r0, chunk), :] = v.astype(BF16)
            return carry

        lax.fori_loop(0, seq_len // chunk, body, 0)
        if n_ctx:
            k_scr[:n_ctx, :] = ck_ref[...].astype(BF16)
            v_scr[:n_ctx, :] = cv_ref[...].astype(BF16)

    q0 = pl.multiple_of(qt * tq, tq)
    if na:
        row_start = jnp.clip(qt - WIN_R // 2, 0, DEC_SEQ // GRID_W - WIN_R)
        d0 = row_start - qt + (WIN_R - 1)
        k0 = pl.multiple_of(n_ctx + row_start * GRID_W, GRID_W)
    for hd in range(N_HEADS):
        g = hd // rep
        hs = slice(hd * HEAD_DIM, (hd + 1) * HEAD_DIM)
        gs = slice(g * HEAD_DIM, (g + 1) * HEAD_DIM)
        qh = q_scr[pl.ds(q0, tq), hs]
        dn = (((1,), (1,)), ((), ()))
        if na:
            kc, vc = k_scr[:n_ctx, gs], v_scr[:n_ctx, gs]
            kl, vl = k_scr[pl.ds(k0, WIN_R * GRID_W), gs], v_scr[pl.ds(k0, WIN_R * GRID_W), gs]
            bias = jnp.concatenate([bias_ref[hd, pl.ds(d0 + 2 * j, 1)][0] for j in range(WIN_R // 2)], axis=1)
            s = jnp.concatenate([lax.dot_general(qh, kc, dn, preferred_element_type=F32),
                                 lax.dot_general(qh, kl, dn, preferred_element_type=F32) + bias], axis=1)
        else:
            s = lax.dot_general(qh, k_scr[:, gs], dn, preferred_element_type=F32)
        e = jnp.exp(s - jnp.max(s, axis=1, keepdims=True))
        den = jnp.sum(e, axis=1, keepdims=True)
        eb = e.astype(BF16)
        if na:
            oh = (jnp.dot(eb[:, :n_ctx], vc, preferred_element_type=F32)
                  + jnp.dot(eb[:, n_ctx:], vl, preferred_element_type=F32))
        else:
            oh = jnp.dot(eb, v_scr[:, gs], preferred_element_type=F32)
        o_scr[pl.ds(q0, tq), hs] = (oh / den).astype(BF16)

    @pl.when(qt == n_qt - 1)
    def _finish():
        def body(ci, carry):
            rows = pl.ds(pl.multiple_of(ci * chunk, chunk), chunk)
            o = jnp.dot(o_scr[rows, :], wo_ref[...], preferred_element_type=F32)
            o_ref[rows, :] = _layer_norm(DEEPNORM_ALPHA * x_ref[rows, :] + mod_ref[2:3, :] * o,
                                         lng_ref[...], lnb_ref[...])
            return carry

        lax.fori_loop(0, seq_len // chunk, body, 0)


def _attn_call(x, mod, l, latent, w_qkv, w_o, ln_g, ln_b, *, n_seq, seq_len, tq, n_kv, norm_args=None,
               rope_args=None, ctx_args=None, bias=None, emit_kv=False, name="attn"):
    wq = N_HEADS * HEAD_DIM
    wk = n_kv * HEAD_DIM
    n_ctx = PAST_LEN if ctx_args is not None else 0
    n_qt = seq_len // tq
    seq_off = N_PROMPT // seq_len if latent else 0
    seq_spec = pl.BlockSpec((seq_len, D_MODEL), lambda s, t: (s, 0))
    mod_spec = pl.BlockSpec((None, None, 6, D_MODEL), lambda s, t: (l, (1 + s) if latent else 0, 0, 0))
    in_specs = [pl.BlockSpec((seq_len, D_MODEL), lambda s, t: (s + seq_off, 0)), mod_spec,
                pl.BlockSpec((D_MODEL, wq + 2 * wk), lambda s, t: (0, 0))]
    args = [x, mod, w_qkv.astype(BF16)]
    if norm_args is not None:
        gq, gk, bd = norm_args
        in_specs += [pl.BlockSpec((1, wq), lambda s, t: (0, 0)), pl.BlockSpec((1, wk), lambda s, t: (0, 0)),
                     pl.BlockSpec((wq, wq), lambda s, t: (0, 0))]
        args += [gq, gk, bd]
    if rope_args is not None:
        in_specs += [pl.BlockSpec((seq_len, LANES), lambda s, t: (0, 0))] * 3
        args += list(rope_args)
    if ctx_args is not None:
        in_specs += [pl.BlockSpec((None, n_ctx, wk), lambda s, t: (s, 0, 0))] * 2
        args += list(ctx_args)
    if bias is not None:
        in_specs += [pl.BlockSpec(bias.shape, lambda s, t: (0, 0, 0, 0))]
        args += [bias]
    in_specs += [pl.BlockSpec((D_MODEL, D_MODEL), lambda s, t: (0, 0)),
                 pl.BlockSpec((1, D_MODEL), lambda s, t: (0, 0)), pl.BlockSpec((1, D_MODEL), lambda s, t: (0, 0))]
    args += [w_o.astype(BF16), ln_g.reshape(1, D_MODEL), ln_b.reshape(1, D_MODEL)]
    out_specs = [seq_spec]
    out_shape = [jax.ShapeDtypeStruct((n_seq * seq_len, D_MODEL), F32)]
    if emit_kv:
        out_specs += [pl.BlockSpec((seq_len, wk), lambda s, t: (s, 0))] * 2
        out_shape += [jax.ShapeDtypeStruct((n_seq * seq_len, wk), F32)] * 2
    kern = functools.partial(_attn_kernel, seq_len=seq_len, tq=tq, n_kv=n_kv, norm=norm_args is not None,
                             rope=rope_args is not None, n_ctx=n_ctx, na=bias is not None, emit_kv=emit_kv)
    return pl.pallas_call(
        kern,
        grid=(n_seq, n_qt),
        in_specs=in_specs,
        out_specs=out_specs,
        out_shape=out_shape,
        scratch_shapes=[pltpu.VMEM((seq_len, wq), BF16), pltpu.VMEM((n_ctx + seq_len, wk), BF16),
                        pltpu.VMEM((n_ctx + seq_len, wk), BF16), pltpu.VMEM((seq_len, wq), BF16)],
        compiler_params=_cparams(("arbitrary", "arbitrary")),
        name=name,
    )(*args)


def _rope_tables():
    t = np.arange(DEC_SEQ)
    pos = np.stack([t // GRID_W, t % GRID_W], axis=1).astype(np.float64)
    inv = (ROPE_THETA ** (-np.arange(ROPE_PAIRS, dtype=np.float32) / ROPE_PAIRS)).astype(np.float64)
    ang = pos[:, :, None] * inv[None, None, :]
    cos, sin = np.cos(ang), np.sin(ang)
    zero = np.zeros_like(sin)
    cos_t = np.concatenate([cos, cos], axis=2).reshape(DEC_SEQ, HEAD_DIM)
    s1_t = np.concatenate([-sin, zero], axis=2).reshape(DEC_SEQ, HEAD_DIM)
    s2_t = np.concatenate([zero, sin], axis=2).reshape(DEC_SEQ, HEAD_DIM)
    return tuple(jnp.asarray(np.tile(a, (1, LANES // HEAD_DIM)), F32) for a in (cos_t, s1_t, s2_t))


def _block_diag_mean():
    a = np.kron(np.eye(N_HEADS), np.full((HEAD_DIM, HEAD_DIM), 1.0 / HEAD_DIM))
    return jnp.asarray(a, BF16)


def _na_bias_table(rpb):
    cols = np.arange(GRID_W)
    col_start = np.clip(cols - WIN_C // 2, 0, GRID_W - WIN_C)
    kc = np.arange(GRID_W)
    inside = (kc[None, :] >= col_start[:, None]) & (kc[None, :] < col_start[:, None] + WIN_C)
    off = np.clip(kc[None, :] - cols[:, None] + (WIN_C - 1), 0, 2 * WIN_C - 2)
    blocks = jnp.where(jnp.asarray(inside)[None, None], rpb[:, :, off], -1e30)
    return jnp.concatenate([blocks[:, :-1], blocks[:, 1:]], axis=-1)


ROUTER_TM = 256


def _router_kernel(x_ref, mod_ref, rw_ref, rb_ref, tri_ref, hp_ref, code_ref, wt_ref, cnt_ref, carry_ref):
    i = pl.program_id(0)

    @pl.when(i == 0)
    def _init():
        carry_ref[...] = jnp.zeros_like(carry_ref)

    h2 = x_ref[...] * (1.0 + mod_ref[4:5, :]) + mod_ref[3:4, :]
    logits = jnp.dot(h2, rw_ref[...], precision=HIGHEST, preferred_element_type=F32) + rb_ref[...]
    lane = lax.broadcasted_iota(I32, (ROUTER_TM, N_EXPERTS), 1)
    vals, idxs, hots = [], [], []
    cur = logits
    for _ in range(TOP_K):
        m = jnp.max(cur, axis=1, keepdims=True)
        idx = jnp.min(jnp.where(cur == m, lane, N_EXPERTS), axis=1, keepdims=True)
        hot = lane == idx
        vals.append(m)
        idxs.append(idx)
        hots.append(hot)
        cur = jnp.where(hot, -jnp.inf, cur)
    exps = [jnp.exp(v - vals[0]) for v in vals]
    den = (exps[0] + exps[1]) + (exps[2] + exps[3])
    mask = jnp.zeros((ROUTER_TM, N_EXPERTS), F32)
    for hot in hots:
        mask = mask + jnp.where(hot, 1.0, 0.0)
    before = jnp.dot(tri_ref[...], mask.astype(BF16), preferred_element_type=F32) + carry_ref[...]
    lane128 = lax.broadcasted_iota(I32, (ROUTER_TM, LANES), 1)
    code = jnp.zeros((ROUTER_TM, LANES), I32)
    wts = jnp.zeros((ROUTER_TM, LANES), F32)
    for k in range(TOP_K):
        rank = jnp.sum(jnp.where(hots[k], before, 0.0), axis=1, keepdims=True).astype(I32)
        code = jnp.where(lane128 == k, idxs[k] * (1 << RANK_BITS) + rank, code)
        wts = jnp.where(lane128 == k, exps[k] / den, wts)
    code_ref[...] = code
    wt_ref[...] = wts
    carry_ref[...] = carry_ref[...] + jnp.sum(mask, axis=0, keepdims=True)
    cnt_ref[...] = carry_ref[...]
    packed = pltpu.pack_elementwise([h2[:, :PACKED_W], h2[:, PACKED_W:]], packed_dtype=BF16)
    for c in range(PACK_ROWS):
        hp_ref[pl.ds(c, ROUTER_TM, stride=PACK_ROWS), :] = packed[:, c * LANES:(c + 1) * LANES]


def _router(x1, mod, l, rw, rb):
    tri = jnp.asarray(np.tril(np.ones((ROUTER_TM, ROUTER_TM)), -1), BF16)
    return pl.pallas_call(
        _router_kernel,
        grid=(N_TOK // ROUTER_TM,),
        in_specs=[pl.BlockSpec((ROUTER_TM, D_MODEL), lambda i: (i, 0)), _mod_spec(l, ROUTER_TM),
                  _full((D_MODEL, N_EXPERTS)), _full((1, N_EXPERTS)), _full((ROUTER_TM, ROUTER_TM))],
        out_specs=[pl.BlockSpec((ROUTER_TM * PACK_ROWS, LANES), lambda i: (i, 0)),
                   pl.BlockSpec((ROUTER_TM, LANES), lambda i: (i, 0)),
                   pl.BlockSpec((ROUTER_TM, LANES), lambda i: (i, 0)),
                   _full((1, N_EXPERTS))],
        out_shape=[jax.ShapeDtypeStruct((N_TOK * PACK_ROWS, LANES), I32),
                   jax.ShapeDtypeStruct((N_TOK, LANES), I32),
                   jax.ShapeDtypeStruct((N_TOK, LANES), F32),
                   jax.ShapeDtypeStruct((1, N_EXPERTS), F32)],
        scratch_shapes=[pltpu.VMEM((1, N_EXPERTS), F32)],
        compiler_params=_cparams(("arbitrary",)),
        name="router",
    )(x1, mod, rw, rb.reshape(1, N_EXPERTS), tri)


def _perm_kernel(code_ref, cnt_ref, slot_ref, iexp_ref, irow_ref, inr_ref, meta_ref, base_ref):
    def per_expert(e, carry):
        off, item, last_e, last_row = carry
        n = cnt_ref[e]
        base_ref[e] = off
        padded = ((n + (ROW_TILE - 1)) // ROW_TILE) * ROW_TILE

        def pad(r, c):
            slot_ref[off + r] = SENTINEL
            return c

        lax.fori_loop(n, padded, pad, 0)
        n_items = (n + (ITEM_ROWS - 1)) // ITEM_ROWS

        def add_item(c, it):
            iexp_ref[it] = e
            irow_ref[it] = off + c * ITEM_ROWS
            inr_ref[it] = jnp.minimum(ITEM_ROWS, n - c * ITEM_ROWS)
            return it + 1

        item2 = lax.fori_loop(0, n_items, add_item, item)
        last_e = jnp.where(n_items > 0, e, last_e)
        last_row = jnp.where(n_items > 0, off + (n_items - 1) * ITEM_ROWS, last_row)
        return off + padded, item2, last_e, last_row

    zero = jnp.int32(0)
    off, item, last_e, last_row = lax.fori_loop(0, N_EXPERTS, per_expert, (zero, zero, zero, zero))

    def idle(it, c):
        iexp_ref[it] = last_e
        irow_ref[it] = last_row
        inr_ref[it] = 0
        return c

    lax.fori_loop(item, ITEM_TABLE, idle, 0)

    def tail(r, c):
        slot_ref[r] = SENTINEL
        return c

    lax.fori_loop(off, N_ROWS, tail, 0)
    meta_ref[0] = off // ROW_TILE
    meta_ref[1] = item
    for m in range(2, 8):
        meta_ref[m] = 0

    def place(j, c):
        for u in range(UNROLL):
            a = j * UNROLL + u
            cd = code_ref[a]
            slot_ref[base_ref[cd >> RANK_BITS] + (cd & ((1 << RANK_BITS) - 1))] = a
        return c

    lax.fori_loop(0, N_ASSIGN // UNROLL, place, 0)


def _perm(code_flat, counts):
    smem = pl.BlockSpec(memory_space=pltpu.SMEM)
    return pl.pallas_call(
        _perm_kernel,
        in_specs=[smem, smem],
        out_specs=[smem] * 5,
        out_shape=[jax.ShapeDtypeStruct((N_ROWS,), I32), jax.ShapeDtypeStruct((ITEM_TABLE,), I32),
                   jax.ShapeDtypeStruct((ITEM_TABLE,), I32), jax.ShapeDtypeStruct((ITEM_TABLE,), I32),
                   jax.ShapeDtypeStruct((8,), I32)],
        scratch_shapes=[pltpu.SMEM((N_EXPERTS,), I32)],
        name="moe_perm",
    )(code_flat, counts)


def _gather_kernel(slot_ref, meta_ref, hflat_ref, xs_ref, t_ref):
    i = pl.program_id(0)

    @pl.when(i < meta_ref[0])
    def _gather():
        def body(j, c):
            for u in range(UNROLL):
                r = j * UNROLL + u
                tok = jnp.minimum(slot_ref[i * ROW_TILE + r] >> 2, N_TOK - 1)
                src = pl.multiple_of(tok * PACK_ROWS, PACK_ROWS)
                t_ref[pl.ds(r, PACK_ROWS, stride=GATHER_STRIDE), :] = hflat_ref[pl.ds(src, PACK_ROWS), :]
            return c

        lax.fori_loop(0, ROW_TILE // UNROLL, body, 0)
        for c in range(PACK_ROWS):
            xs_ref[:, c * LANES:(c + 1) * LANES] = t_ref[c * GATHER_STRIDE:c * GATHER_STRIDE + ROW_TILE, :]

    @pl.when(i >= meta_ref[0])
    def _unused():
        xs_ref[...] = jnp.zeros_like(xs_ref)


def _gather(slot, meta, hflat):
    return pl.pallas_call(
        _gather_kernel,
        grid_spec=pltpu.PrefetchScalarGridSpec(
            num_scalar_prefetch=2,
            grid=(N_ROW_TILES,),
            in_specs=[pl.BlockSpec((N_TOK * PACK_ROWS, LANES), lambda i, *_: (0, 0))],
            out_specs=pl.BlockSpec((ROW_TILE, PACKED_W), lambda i, *_: (i, 0)),
            scratch_shapes=[pltpu.VMEM((PACK_ROWS * GATHER_STRIDE, LANES), I32)],
        ),
        out_shape=jax.ShapeDtypeStruct((N_ROWS, PACKED_W), I32),
        compiler_params=_cparams(("arbitrary",)),
        name="moe_gather",
    )(slot, meta, hflat)


def _unpack_rows(p):
    lo = pltpu.unpack_elementwise(p, index=0, packed_dtype=BF16, unpacked_dtype=F32)
    hi = pltpu.unpack_elementwise(p, index=1, packed_dtype=BF16, unpacked_dtype=F32)
    return jnp.concatenate([lo.astype(BF16), hi.astype(BF16)], axis=1)


def _expert_kernel(slot_ref, wflat_ref, iexp_ref, irow_ref, inr_ref, x0_ref, x1_ref, x2_ref, x3_ref,
                   w1_ref, w2_ref, b1_ref, b2_ref, f_hbm, acc_ref, g_ref, a_ref, yt_ref, sem):
    it = pl.program_id(0)
    ph = pl.program_id(1)
    n = inr_ref[it]
    row0 = irow_ref[it]
    x_refs = (x0_ref, x1_ref, x2_ref, x3_ref)

    @pl.when((it == 0) & (ph == 0))
    def _zero():
        acc_ref[...] = jnp.zeros_like(acc_ref)

    for k in range(ITEM_TILES):
        rows = slice(k * ROW_TILE, (k + 1) * ROW_TILE)
        active = k * ROW_TILE < n

        @pl.when(active & (ph == 0))
        def _gate(k=k, rows=rows):
            x = _unpack_rows(x_refs[k][...])
            g_ref[rows, :] = (jnp.dot(x, w1_ref[...].astype(BF16), preferred_element_type=F32)
                              + b1_ref[:, :D_FF])

        @pl.when(active & (ph == 1))
        def _up(k=k, rows=rows):
            x = _unpack_rows(x_refs[k][...])
            up = jnp.dot(x, w1_ref[...].astype(BF16), preferred_element_type=F32) + b1_ref[:, D_FF:]
            gate = jnp.minimum(g_ref[rows, :], SWIGLU_LIMIT)
            up = jnp.clip(up, -SWIGLU_LIMIT, SWIGLU_LIMIT)
            a_ref[rows, :] = (gate * jax.nn.sigmoid(SWIGLU_ALPHA * gate) * (up + 1.0)).astype(BF16)

        @pl.when(active & (ph == 2))
        def _down(k=k, rows=rows):
            y = jnp.dot(a_ref[rows, :], w2_ref[...].astype(BF16), preferred_element_type=F32) + b2_ref[...]
            for c in range(OUT_ROWS):
                yt_ref[pl.ds(c, ROW_TILE, stride=OUT_ROWS), :] = y[:, c * LANES:(c + 1) * LANES]
            n_here = jnp.minimum(n - k * ROW_TILE, ROW_TILE)

            def body(j, carry):
                dsts, vals = [], []
                for u in range(UNROLL):
                    r = j * UNROLL + u
                    s = slot_ref[row0 + k * ROW_TILE + r]
                    dst = pl.multiple_of((s >> 2) * OUT_ROWS, OUT_ROWS)
                    src = pl.multiple_of(r * OUT_ROWS, OUT_ROWS)
                    dsts.append(dst)
                    vals.append(acc_ref[pl.ds(dst, OUT_ROWS), :] + wflat_ref[s] * yt_ref[pl.ds(src, OUT_ROWS), :])
                for dst, val in zip(dsts, vals):
                    acc_ref[pl.ds(dst, OUT_ROWS), :] = val
                return carry

            lax.fori_loop(0, (n_here + (UNROLL - 1)) // UNROLL, body, 0)

    @pl.when((it == MAX_ITEMS - 1) & (ph == 2))
    def _flush():
        cp = pltpu.make_async_copy(acc_ref.at[pl.ds(0, N_TOK * OUT_ROWS)], f_hbm, sem)
        cp.start()
        cp.wait()


def _experts(slot, wflat, iexp, irow, inr, xs, l, w1, b1, w2, b2):
    def x_spec(k):
        return pl.BlockSpec((ROW_TILE, PACKED_W),
                            lambda it, ph, slot, wf, ie, ir, nr: (jnp.minimum(ir[it] // ROW_TILE + k, N_ROW_TILES - 1), 0))

    return pl.pallas_call(
        _expert_kernel,
        grid_spec=pltpu.PrefetchScalarGridSpec(
            num_scalar_prefetch=5,
            grid=(MAX_ITEMS, 3),
            in_specs=[x_spec(0), x_spec(1), x_spec(2), x_spec(3),
                      pl.BlockSpec((None, None, D_MODEL, D_FF), lambda it, ph, slot, wf, ie, ir, nr: (l, ie[it], 0, jnp.minimum(ph, 1))),
                      pl.BlockSpec((None, None, D_FF, D_MODEL), lambda it, ph, slot, wf, ie, ir, nr: (l, ie[it], 0, 0)),
                      pl.BlockSpec((None, None, 1, 2 * D_FF), lambda it, ph, slot, wf, ie, ir, nr: (l, ie[it], 0, 0)),
                      pl.BlockSpec((None, None, 1, D_MODEL), lambda it, ph, slot, wf, ie, ir, nr: (l, ie[it], 0, 0))],
            out_specs=pl.BlockSpec(memory_space=pl.ANY),
            scratch_shapes=[pltpu.VMEM(((N_TOK + 1) * OUT_ROWS, LANES), F32),
                            pltpu.VMEM((ITEM_ROWS, D_FF), F32),
                            pltpu.VMEM((ITEM_ROWS, D_FF), BF16),
                            pltpu.VMEM((ROW_TILE * OUT_ROWS, LANES), F32),
                            pltpu.SemaphoreType.DMA(())],
        ),
        out_shape=jax.ShapeDtypeStruct((N_TOK * OUT_ROWS, LANES), F32),
        compiler_params=_cparams(("arbitrary", "arbitrary")),
        name="moe_experts",
    )(slot, wflat, iexp, irow, inr, xs, xs, xs, xs, w1, w2,
      b1.reshape(DEPTH, N_EXPERTS, 1, 2 * D_FF), b2.reshape(DEPTH, N_EXPERTS, 1, D_MODEL))


LN2_TM = 256


def _ln2_kernel(x_ref, f_ref, mod_ref, g_ref, b_ref, o_ref):
    f = jnp.concatenate([f_ref[pl.ds(c, LN2_TM, stride=OUT_ROWS), :] for c in range(OUT_ROWS)], axis=1)
    o_ref[...] = _layer_norm(DEEPNORM_ALPHA * x_ref[...] + mod_ref[5:6, :] * f, g_ref[...], b_ref[...])


def _ln2(x1, f_flat, mod, l, g, b):
    row = pl.BlockSpec((LN2_TM, D_MODEL), lambda i: (i, 0))
    return pl.pallas_call(
        _ln2_kernel,
        grid=(N_TOK // LN2_TM,),
        in_specs=[row, pl.BlockSpec((LN2_TM * OUT_ROWS, LANES), lambda i: (i, 0)), _mod_spec(l, LN2_TM),
                  _full((1, D_MODEL)), _full((1, D_MODEL))],
        out_specs=row,
        out_shape=jax.ShapeDtypeStruct((N_TOK, D_MODEL), F32),
        compiler_params=_cparams(("arbitrary",)),
        name="ln2",
    )(x1, f_flat, mod, g.reshape(1, D_MODEL), b.reshape(1, D_MODEL))


def _moe_layer(x1, mod, l, rw, rb, w1, b1, w2, b2, ln_g, ln_b):
    hflat, code, wts, counts = _router(x1, mod, l, rw, rb)
    code_flat = code[:, :TOP_K].reshape(N_ASSIGN)
    wflat = jnp.concatenate([wts[:, :TOP_K].reshape(N_ASSIGN), jnp.zeros((8,), F32)])
    slot, iexp, irow, inr, meta = _perm(code_flat, counts.astype(I32).reshape(N_EXPERTS))
    xs = _gather(slot, meta, hflat)
    f_flat = _experts(slot, wflat, iexp, irow, inr, xs, l, w1, b1, w2, b2)
    return _ln2(x1, f_flat, mod, l, ln_g, ln_b)


def kernel(x_prompt, x_sample, cache_k_attn, cache_v_attn, cache_k_na, cache_v_na, c, c_ctx, w_mod, b_mod, ln1_g, ln1_b, ln2_g, ln2_b, conv_w_in, conv_w, conv_b, conv_w_out, attn_w_qkv, attn_q_norm, attn_k_norm, attn_w_o, na_w_qkv, na_rpb, na_w_o, router_w, router_b, moe_w1, moe_b1, moe_w2, moe_b2):
    x = jnp.concatenate([x_prompt.reshape(N_PROMPT, D_MODEL), x_sample.reshape(N_SAMPLE, D_MODEL)], axis=0)
    cond8 = jnp.concatenate([c_ctx[None, :], c, jnp.zeros((8 - 1 - DEC_BATCH, D_MODEL), F32)], axis=0)
    mod = _adaln_all(cond8, w_mod, b_mod)
    new_kv = {}
    for l in range(DEPTH):
        kind, j = l % 3, l // 3
        if kind == 0:
            x1 = _conv_layer(x, mod, l, conv_w_in[j], conv_w[j], conv_b[j], conv_w_out[j], ln1_g[l], ln1_b[l])
        elif kind == 1:
            gq = jnp.tile(attn_q_norm[j], N_HEADS).reshape(1, N_HEADS * HEAD_DIM)
            gk = jnp.tile(attn_k_norm[j], N_KV_HEADS).reshape(1, N_KV_HEADS * HEAD_DIM)
            norm_args = (gq, gk, _block_diag_mean())
            wk = N_KV_HEADS * HEAD_DIM
            xp1, nk, nv = _attn_call(x, mod, l, False, attn_w_qkv[j], attn_w_o[j], ln1_g[l], ln1_b[l],
                                     n_seq=BATCH, seq_len=SEQ, tq=SEQ, n_kv=N_KV_HEADS, norm_args=norm_args,
                                     emit_kv=True, name="gqa_prompt")
            new_kv["k_attn"] = nk.reshape(BATCH, 1, SEQ, N_KV_HEADS, HEAD_DIM)
            new_kv["v_attn"] = nv.reshape(BATCH, 1, SEQ, N_KV_HEADS, HEAD_DIM)
            (xs1,) = _attn_call(x, mod, l, True, attn_w_qkv[j], attn_w_o[j], ln1_g[l], ln1_b[l],
                                n_seq=DEC_BATCH, seq_len=DEC_SEQ, tq=256, n_kv=N_KV_HEADS, norm_args=norm_args,
                                rope_args=_rope_tables(),
                                ctx_args=(cache_k_attn[:, j].reshape(DEC_BATCH, PAST_LEN, wk),
                                          cache_v_attn[:, j].reshape(DEC_BATCH, PAST_LEN, wk)),
                                name="gqa_sample")
            x1 = jnp.concatenate([xp1, xs1], axis=0)
        else:
            wk = N_HEADS * HEAD_DIM
            xp1, nk, nv = _attn_call(x, mod, l, False, na_w_qkv[j], na_w_o[j], ln1_g[l], ln1_b[l],
                                     n_seq=BATCH, seq_len=SEQ, tq=SEQ, n_kv=N_HEADS, emit_kv=True,
                                     name="mha_prompt")
            new_kv["k_na"] = nk.reshape(BATCH, 1, SEQ, N_HEADS, HEAD_DIM)
            new_kv["v_na"] = nv.reshape(BATCH, 1, SEQ, N_HEADS, HEAD_DIM)
            (xs1,) = _attn_call(x, mod, l, True, na_w_qkv[j], na_w_o[j], ln1_g[l], ln1_b[l],
                                n_seq=DEC_BATCH, seq_len=DEC_SEQ, tq=GRID_W, n_kv=N_HEADS,
                                ctx_args=(cache_k_na[:, j].reshape(DEC_BATCH, PAST_LEN, wk),
                                          cache_v_na[:, j].reshape(DEC_BATCH, PAST_LEN, wk)),
                                bias=_na_bias_table(na_rpb[j]), name="na_sample")
            x1 = jnp.concatenate([xp1, xs1], axis=0)
        x = _moe_layer(x1, mod, l, router_w[l], router_b[l], moe_w1, moe_b1, moe_w2, moe_b2,
                       ln2_g[l], ln2_b[l])
    y_prompt = x[:N_PROMPT].reshape(BATCH, SEQ, D_MODEL)
    y_sample = x[N_PROMPT:].reshape(DEC_BATCH, DEC_SEQ, D_MODEL)
    return (y_prompt, y_sample, new_kv["k_attn"], new_kv["v_attn"], new_kv["k_na"], new_kv["v_na"])
```

```python
import functools

import numpy as np
import jax
import jax.numpy as jnp
from jax import lax
from jax.experimental import pallas as pl
from jax.experimental.pallas import tpu as pltpu

D_MODEL = 1024
BATCH = 16
SEQ = 256
DEPTH = 4
DEC_BATCH = 2
DEC_SEQ = 1024
PAST_LEN = 256
GRID_W = 64
HEAD_DIM = 64
N_HEADS = 16
N_KV_HEADS = 4
ROPE_THETA = 10000.0
ROPE_PAIRS = HEAD_DIM // 4
WIN_R = 8
WIN_C = 16
N_EXPERTS = 32
TOP_K = 4
D_FF = D_MODEL
SWIGLU_LIMIT = 7.0
SWIGLU_ALPHA = 1.702
DEEPNORM_ALPHA = (2 * DEPTH) ** 0.25
LN_EPS = 1e-5
RMS_EPS = 1e-6

F32 = jnp.float32
BF16 = jnp.bfloat16
I32 = jnp.int32
HIGHEST = lax.Precision.HIGHEST

N_PROMPT = BATCH * SEQ
N_SAMPLE = DEC_BATCH * DEC_SEQ
N_TOK = N_PROMPT + N_SAMPLE
N_ASSIGN = N_TOK * TOP_K

LANES = 128
SUBLANES = 8
PACKED_W = D_MODEL // 2

VMEM_LIMIT = 58 * 1024 * 1024


def _cparams(sem):
    return pltpu.CompilerParams(dimension_semantics=sem, vmem_limit_bytes=VMEM_LIMIT)


def _layer_norm(x, g, b):
    mu = jnp.mean(x, -1, keepdims=True)
    xc = x - mu
    var = jnp.mean(xc * xc, -1, keepdims=True)
    return xc * lax.rsqrt(var + LN_EPS) * g + b


def _mod_row(i, tile):
    n_prompt_tiles = N_PROMPT // tile
    return jnp.where(i < n_prompt_tiles, 0, 1 + (i - n_prompt_tiles) // (DEC_SEQ // tile))


def _mod_spec(l, tile):
    return pl.BlockSpec((None, None, 6, D_MODEL), lambda i, *_: (l, _mod_row(i, tile), 0, 0))


def _full(shape):
    nd = len(shape)
    return pl.BlockSpec(shape, lambda *_: (0,) * nd)


ADALN_TN = 1536


def _adaln_kernel(cond_ref, w_ref, b_ref, o_ref):
    c = cond_ref[...]
    s = c * jax.nn.sigmoid(c)
    o_ref[...] = jnp.dot(s, w_ref[...], precision=HIGHEST, preferred_element_type=F32) + b_ref[...]


def _adaln_all(cond8, w_mod, b_mod):
    n = 6 * D_MODEL
    out = pl.pallas_call(
        _adaln_kernel,
        grid=(DEPTH, n // ADALN_TN),
        in_specs=[
            pl.BlockSpec((8, D_MODEL), lambda l, j: (0, 0)),
            pl.BlockSpec((None, D_MODEL, ADALN_TN), lambda l, j: (l, 0, j)),
            pl.BlockSpec((None, 1, ADALN_TN), lambda l, j: (l, 0, j)),
        ],
        out_specs=pl.BlockSpec((None, 8, ADALN_TN), lambda l, j: (l, 0, j)),
        out_shape=jax.ShapeDtypeStruct((DEPTH, 8, n), F32),
        compiler_params=_cparams(("arbitrary", "arbitrary")),
        name="adaln",
    )(cond8, w_mod, b_mod.reshape(DEPTH, 1, n))
    return out.reshape(DEPTH, 8, 6, D_MODEL)


CONV_TM = 1024


def _conv_kernel(x_ref, mod_ref, win_ref, cw_ref, cb_ref, wout_ref, g_ref, b_ref, o_ref):
    i = pl.program_id(0)
    x = x_ref[...]
    h = (x * (1.0 + mod_ref[1:2, :]) + mod_ref[0:1, :]).astype(BF16)
    gc = jnp.dot(h, win_ref[:, D_MODEL:2 * D_MODEL], preferred_element_type=F32)
    xv = jnp.dot(h, win_ref[:, 2 * D_MODEL:], preferred_element_type=F32)
    u = gc * xv
    seq_len = jnp.where(i < N_PROMPT // CONV_TM, SEQ, DEC_SEQ)
    t = lax.broadcasted_iota(I32, (CONV_TM, 1), 0) & (seq_len - 1)
    u_prev = jnp.where(t == 0, 0.0, pltpu.roll(u, 1, axis=0))
    u_next = jnp.where(t == seq_len - 1, 0.0, pltpu.roll(u, CONV_TM - 1, axis=0))
    y = u_prev * cw_ref[0:1, :] + u * cw_ref[1:2, :] + u_next * cw_ref[2:3, :] + cb_ref[...]
    gb = jnp.dot(h, win_ref[:, :D_MODEL], preferred_element_type=F32)
    v = (gb * y).astype(BF16)
    o = jnp.dot(v, wout_ref[...], preferred_element_type=F32)
    o_ref[...] = _layer_norm(DEEPNORM_ALPHA * x + mod_ref[2:3, :] * o, g_ref[...], b_ref[...])


def _conv_layer(x, mod, l, w_in, cw, cb, w_out, ln_g, ln_b):
    row = pl.BlockSpec((CONV_TM, D_MODEL), lambda i: (i, 0))
    return pl.pallas_call(
        _conv_kernel,
        grid=(N_TOK // CONV_TM,),
        in_specs=[row, _mod_spec(l, CONV_TM), _full((D_MODEL, 3 * D_MODEL)), _full((3, D_MODEL)),
                  _full((1, D_MODEL)), _full((D_MODEL, D_MODEL)), _full((1, D_MODEL)), _full((1, D_MODEL))],
        out_specs=row,
        out_shape=jax.ShapeDtypeStruct((N_TOK, D_MODEL), F32),
        compiler_params=_cparams(("arbitrary",)),
        name="conv_mixer",
    )(x, mod, w_in.astype(BF16), cw, cb.reshape(1, D_MODEL), w_out.astype(BF16),
      ln_g.reshape(1, D_MODEL), ln_b.reshape(1, D_MODEL))


ATTN_CHUNK = 256


def _attn_kernel(*refs, seq_len, tq, n_kv, norm, rope, n_ctx, na, emit_kv):
    refs = list(refs)
    x_ref, mod_ref, wqkv_ref = refs[:3]
    pos = 3
    if norm:
        gq_ref, gk_ref, bd_ref = refs[pos:pos + 3]
        pos += 3
    if rope:
        cos_ref, s1_ref, s2_ref = refs[pos:pos + 3]
        pos += 3
    if n_ctx:
        ck_ref, cv_ref = refs[pos:pos + 2]
        pos += 2
    if na:
        bias_ref = refs[pos]
        pos += 1
    wo_ref, lng_ref, lnb_ref = refs[pos:pos + 3]
    pos += 3
    o_ref = refs[pos]
    pos += 1
    if emit_kv:
        nk_ref, nv_ref = refs[pos:pos + 2]
        pos += 2
    q_scr, k_scr, v_scr, o_scr = refs[pos:pos + 4]

    qt = pl.program_id(1)
    n_qt = seq_len // tq
    wq = N_HEADS * HEAD_DIM
    wk = n_kv * HEAD_DIM
    rep = N_HEADS // n_kv

    def rms(v, g_ref, width):
        sq = v * v
        hi = sq.astype(BF16)
        lo = (sq - hi.astype(F32)).astype(BF16)
        bd = bd_ref[:width, :width]
        ms = jnp.dot(hi, bd, preferred_element_type=F32) + jnp.dot(lo, bd, preferred_element_type=F32)
        return v * lax.rsqrt(ms + RMS_EPS) * g_ref[...]

    def rot(v, rows, width):
        def tab(ref):
            t = ref[rows, :]
            return jnp.concatenate([t] * (width // LANES), axis=1)

        return (v * tab(cos_ref) + pltpu.roll(v, width - ROPE_PAIRS, axis=1) * tab(s1_ref)
                + pltpu.roll(v, ROPE_PAIRS, axis=1) * tab(s2_ref))

    chunk = min(seq_len, ATTN_CHUNK)

    @pl.when(qt == 0)
    def _project():
        def body(ci, carry):
            r0 = pl.multiple_of(ci * chunk, chunk)
            rows = pl.ds(r0, chunk)
            h = (x_ref[rows, :] * (1.0 + mod_ref[1:2, :]) + mod_ref[0:1, :]).astype(BF16)
            q = jnp.dot(h, wqkv_ref[:, :wq], preferred_element_type=F32)
            k = jnp.dot(h, wqkv_ref[:, wq:wq + wk], preferred_element_type=F32)
            v = jnp.dot(h, wqkv_ref[:, wq + wk:], preferred_element_type=F32)
            if norm:
                q = rms(q, gq_ref, wq)
                k = rms(k, gk_ref, wk)
            if emit_kv:
                nk_ref[rows, :] = k
                nv_ref[rows, :] = v
            if rope:
                q = rot(q, rows, wq)
                k = rot(k, rows, wk)
            q_scr[rows, :] = (q * (HEAD_DIM ** -0.5)).astype(BF16)
            k_scr[pl.ds(n_ctx + r0, chunk), :] = k.astype(BF16)
            v_scr[pl.ds(n_ctx + r0, chunk), :] = v.astype(BF16)
            return carry

        lax.fori_loop(0, seq_len // chunk, body, 0)
        if n_ctx:
            k_scr[:n_ctx, :] = ck_ref[...].astype(BF16)
            v_scr[:n_ctx, :] = cv_ref[...].astype(BF16)

    q0 = pl.multiple_of(qt * tq, tq)
    if na:
        row_start = jnp.clip(qt - WIN_R // 2, 0, DEC_SEQ // GRID_W - WIN_R)
        d0 = row_start - qt + (WIN_R - 1)
        k0 = pl.multiple_of(n_ctx + row_start * GRID_W, GRID_W)
    for hd in range(N_HEADS):
        g = hd // rep
        hs = slice(hd * HEAD_DIM, (hd + 1) * HEAD_DIM)
        gs = slice(g * HEAD_DIM, (g + 1) * HEAD_DIM)
        qh = q_scr[pl.ds(q0, tq), hs]
        dn = (((1,), (1,)), ((), ()))
        if na:
            kc, vc = k_scr[:n_ctx, gs], v_scr[:n_ctx, gs]
            kl, vl = k_scr[pl.ds(k0, WIN_R * GRID_W), gs], v_scr[pl.ds(k0, WIN_R * GRID_W), gs]
            bias = jnp.concatenate([bias_ref[hd, pl.ds(d0 + 2 * j, 1)][0] for j in range(WIN_R // 2)], axis=1)
            s = jnp.concatenate([lax.dot_general(qh, kc, dn, preferred_element_type=F32),
                                 lax.dot_general(qh, kl, dn, preferred_element_type=F32) + bias], axis=1)
        else:
            s = lax.dot_general(qh, k_scr[:, gs], dn, preferred_element_type=F32)
        e = jnp.exp(s - jnp.max(s, axis=1, keepdims=True))
        den = jnp.sum(e, axis=1, keepdims=True)
        eb = e.astype(BF16)
        if na:
            oh = (jnp.dot(eb[:, :n_ctx], vc, preferred_element_type=F32)
                  + jnp.dot(eb[:, n_ctx:], vl, preferred_element_type=F32))
        else:
            oh = jnp.dot(eb, v_scr[:, gs], preferred_element_type=F32)
        o_scr[pl.ds(q0, tq), hs] = (oh / den).astype(BF16)

    @pl.when(qt == n_qt - 1)
    def _finish():
        def body(ci, carry):
            rows = pl.ds(pl.multiple_of(ci * chunk, chunk), chunk)
            o = jnp.dot(o_scr[rows, :], wo_ref[...], preferred_element_type=F32)
            o_ref[rows, :] = _layer_norm(DEEPNORM_ALPHA * x_ref[rows, :] + mod_ref[2:3, :] * o,
                                         lng_ref[...], lnb_ref[...])
            return carry

        lax.fori_loop(0, seq_len // chunk, body, 0)


def _attn_call(x, mod, l, latent, w_qkv, w_o, ln_g, ln_b, *, n_seq, seq_len, tq, n_kv, norm_args=None,
               rope_args=None, ctx_args=None, bias=None, emit_kv=False, name="attn"):
    wq = N_HEADS * HEAD_DIM
    wk = n_kv * HEAD_DIM
    n_ctx = PAST_LEN if ctx_args is not None else 0
    n_qt = seq_len // tq
    seq_off = N_PROMPT // seq_len if latent else 0
    seq_spec = pl.BlockSpec((seq_len, D_MODEL), lambda s, t: (s, 0))
    mod_spec = pl.BlockSpec((None, None, 6, D_MODEL), lambda s, t: (l, (1 + s) if latent else 0, 0, 0))
    in_specs = [pl.BlockSpec((seq_len, D_MODEL), lambda s, t: (s + seq_off, 0)), mod_spec,
                pl.BlockSpec((D_MODEL, wq + 2 * wk), lambda s, t: (0, 0))]
    args = [x, mod, w_qkv.astype(BF16)]
    if norm_args is not None:
        gq, gk, bd = norm_args
        in_specs += [pl.BlockSpec((1, wq), lambda s, t: (0, 0)), pl.BlockSpec((1, wk), lambda s, t: (0, 0)),
                     pl.BlockSpec((wq, wq), lambda s, t: (0, 0))]
        args += [gq, gk, bd]
    if rope_args is not None:
        in_specs += [pl.BlockSpec((seq_len, LANES), lambda s, t: (0, 0))] * 3
        args += list(rope_args)
    if ctx_args is not None:
        in_specs += [pl.BlockSpec((None, n_ctx, wk), lambda s, t: (s, 0, 0))] * 2
        args += list(ctx_args)
    if bias is not None:
        in_specs += [pl.BlockSpec(bias.shape, lambda s, t: (0, 0, 0, 0))]
        args += [bias]
    in_specs += [pl.BlockSpec((D_MODEL, D_MODEL), lambda s, t: (0, 0)),
                 pl.BlockSpec((1, D_MODEL), lambda s, t: (0, 0)), pl.BlockSpec((1, D_MODEL), lambda s, t: (0, 0))]
    args += [w_o.astype(BF16), ln_g.reshape(1, D_MODEL), ln_b.reshape(1, D_MODEL)]
    out_specs = [seq_spec]
    out_shape = [jax.ShapeDtypeStruct((n_seq * seq_len, D_MODEL), F32)]
    if emit_kv:
        out_specs += [pl.BlockSpec((seq_len, wk), lambda s, t: (s, 0))] * 2
        out_shape += [jax.ShapeDtypeStruct((n_seq * seq_len, wk), F32)] * 2
    kern = functools.partial(_attn_kernel, seq_len=seq_len, tq=tq, n_kv=n_kv, norm=norm_args is not None,
                             rope=rope_args is not None, n_ctx=n_ctx, na=bias is not None, emit_kv=emit_kv)
    return pl.pallas_call(
        kern,
        grid=(n_seq, n_qt),
        in_specs=in_specs,
        out_specs=out_specs,
        out_shape=out_shape,
        scratch_shapes=[pltpu.VMEM((seq_len, wq), BF16), pltpu.VMEM((n_ctx + seq_len, wk), BF16),
                        pltpu.VMEM((n_ctx + seq_len, wk), BF16), pltpu.VMEM((seq_len, wq), BF16)],
        compiler_params=_cparams(("arbitrary", "arbitrary")),
        name=name,
    )(*args)


def _rope_tables():
    t = np.arange(DEC_SEQ)
    pos = np.stack([t // GRID_W, t % GRID_W], axis=1).astype(np.float64)
    inv = (ROPE_THETA ** (-np.arange(ROPE_PAIRS, dtype=np.float32) / ROPE_PAIRS)).astype(np.float64)
    ang = pos[:, :, None] * inv[None, None, :]
    cos, sin = np.cos(ang), np.sin(ang)
    zero = np.zeros_like(sin)
    cos_t = np.concatenate([cos, cos], axis=2).reshape(DEC_SEQ, HEAD_DIM)
    s1_t = np.concatenate([-sin, zero], axis=2).reshape(DEC_SEQ, HEAD_DIM)
    s2_t = np.concatenate([zero, sin], axis=2).reshape(DEC_SEQ, HEAD_DIM)
    return tuple(jnp.asarray(np.tile(a, (1, LANES // HEAD_DIM)), F32) for a in (cos_t, s1_t, s2_t))


def _block_diag_mean():
    a = np.kron(np.eye(N_HEADS), np.full((HEAD_DIM, HEAD_DIM), 1.0 / HEAD_DIM))
    return jnp.asarray(a, BF16)


def _na_bias_table(rpb):
    cols = np.arange(GRID_W)
    col_start = np.clip(cols - WIN_C // 2, 0, GRID_W - WIN_C)
    kc = np.arange(GRID_W)
    inside = (kc[None, :] >= col_start[:, None]) & (kc[None, :] < col_start[:, None] + WIN_C)
    off = np.clip(kc[None, :] - cols[:, None] + (WIN_C - 1), 0, 2 * WIN_C - 2)
    blocks = jnp.where(jnp.asarray(inside)[None, None], rpb[:, :, off], -1e30)
    return jnp.concatenate([blocks[:, :-1], blocks[:, 1:]], axis=-1)


MOE_TM = 256
N_TILES = N_TOK // MOE_TM
CHUNK_SHIFT, BIG_SHIFT, SUB_SHIFT = 3, 5, 8
CHUNK = 1 << CHUNK_SHIFT
BIG = 1 << BIG_SHIFT
TILE_ROWS = 1280
SUB = 1 << SUB_SHIFT
ITEM_ROWS = 2048
MAX_ITEMS = 48
ITEM_TABLE = 64
N_SEG = N_TILES * N_EXPERTS

assert TILE_ROWS >= MOE_TM * TOP_K + N_EXPERTS * (CHUNK - 1) and TILE_ROWS % SUB == 0
assert MAX_ITEMS >= N_EXPERTS + (N_ASSIGN + N_SEG * (CHUNK - 1) - 1) // (ITEM_ROWS - MOE_TM)


def _dispatch_kernel(x_ref, mod_ref, rwt_ref, rbc_ref, upper_ref, lower_ref, xsb_ref, route_ref, tab_ref):
    h2 = x_ref[...] * (1.0 + mod_ref[4:5, :]) + mod_ref[3:4, :]
    logits = lax.dot_general(rwt_ref[...], h2, (((1,), (1,)), ((), ())), precision=HIGHEST,
                             preferred_element_type=F32) + rbc_ref[...]
    sub = lax.broadcasted_iota(I32, (N_EXPERTS, MOE_TM), 0)
    vals, hots = [], []
    cur = logits
    for _ in range(TOP_K):
        m = jnp.max(cur, axis=0, keepdims=True)
        idx = jnp.min(jnp.where(cur == m, sub, N_EXPERTS), axis=0, keepdims=True)
        hot = sub == idx
        vals.append(m)
        hots.append(hot)
        cur = jnp.where(hot, -jnp.inf, cur)
    exps = [jnp.exp(v - vals[0]) for v in vals]
    den = (exps[0] + exps[1]) + (exps[2] + exps[3])
    mask = jnp.zeros((N_EXPERTS, MOE_TM), F32)
    for hot in hots:
        mask = mask + jnp.where(hot, 1.0, 0.0)
    before = jnp.dot(mask.astype(BF16), upper_ref[...], preferred_element_type=F32)
    n = jnp.sum(mask, axis=1, keepdims=True)
    p = (((n.astype(I32) + (CHUNK - 1)) >> CHUNK_SHIFT) << CHUNK_SHIFT).astype(F32)
    start = jnp.dot(lower_ref[...], jnp.broadcast_to(p, (N_EXPERTS, LANES)), precision=HIGHEST,
                    preferred_element_type=F32)[:, :1]
    base = start + before
    lps = [jnp.sum(jnp.where(hot, base, 0.0), axis=0, keepdims=True) for hot in hots]
    route_ref[...] = jnp.concatenate(lps + [e / den for e in exps], axis=0)
    lane = lax.broadcasted_iota(I32, (N_EXPERTS, LANES), 1)
    tab_ref[...] = jnp.where(lane == 0, n, jnp.where(lane == 1, p, jnp.where(lane == 2, start, 0.0)))
    jrow = lax.broadcasted_iota(I32, (TILE_ROWS, MOE_TM), 0)
    onehot = jnp.zeros((TILE_ROWS, MOE_TM), F32)
    for lp in lps:
        onehot = jnp.where(jrow == lp.astype(I32), 1.0, onehot)
    xs = jnp.dot(onehot.astype(BF16), h2.astype(BF16), preferred_element_type=F32)
    xsb_ref[...] = pltpu.pack_elementwise([xs[:, :PACKED_W], xs[:, PACKED_W:]], packed_dtype=BF16)


def _dispatch(x1, mod, l, rw, rb):
    upper = jnp.asarray(np.triu(np.ones((MOE_TM, MOE_TM)), 1), BF16)
    lower = jnp.asarray(np.tril(np.ones((N_EXPERTS, N_EXPERTS)), -1), F32)
    return pl.pallas_call(
        _dispatch_kernel,
        grid=(N_TILES,),
        in_specs=[pl.BlockSpec((MOE_TM, D_MODEL), lambda i: (i, 0)), _mod_spec(l, MOE_TM),
                  _full((N_EXPERTS, D_MODEL)), _full((N_EXPERTS, 1)), _full((MOE_TM, MOE_TM)),
                  _full((N_EXPERTS, N_EXPERTS))],
        out_specs=[pl.BlockSpec((None, TILE_ROWS, PACKED_W), lambda i: (i, 0, 0)),
                   pl.BlockSpec((None, 2 * TOP_K, MOE_TM), lambda i: (i, 0, 0)),
                   pl.BlockSpec((None, N_EXPERTS, LANES), lambda i: (i, 0, 0))],
        out_shape=[jax.ShapeDtypeStruct((N_TILES, TILE_ROWS, PACKED_W), I32),
                   jax.ShapeDtypeStruct((N_TILES, 2 * TOP_K, MOE_TM), F32),
                   jax.ShapeDtypeStruct((N_TILES, N_EXPERTS, LANES), F32)],
        compiler_params=_cparams(("arbitrary",)),
        name="moe_dispatch",
    )(x1, mod, rw.T, rb.reshape(N_EXPERTS, 1), upper, lower)


def _plan_kernel(p_ref, iexp_ref, ib0_ref, ib1_ref, irows_ref, cnt_ref):
    cnt_ref[0] = 0

    def emit(e, b0, b1, rows):
        it = cnt_ref[0]
        iexp_ref[it] = e
        ib0_ref[it] = b0
        ib1_ref[it] = b1
        irows_ref[it] = rows
        cnt_ref[0] = it + 1

    def per_expert(e, last_e):
        def per_tile(b, carry):
            b0, rows = carry
            pb = p_ref[b * N_EXPERTS + e]
            full = rows + pb > ITEM_ROWS

            @pl.when(full)
            def _():
                emit(e, b0, b, rows)

            return jnp.where(full, b, b0), jnp.where(full, pb, rows + pb)

        b0, rows = lax.fori_loop(0, N_TILES, per_tile, (jnp.int32(0), jnp.int32(0)))

        @pl.when(rows > 0)
        def _():
            emit(e, b0, jnp.int32(N_TILES), rows)

        return jnp.where(rows > 0, e, last_e)

    last_e = lax.fori_loop(0, N_EXPERTS, per_expert, jnp.int32(0))

    def idle(it, c):
        iexp_ref[it] = last_e
        ib0_ref[it] = 0
        ib1_ref[it] = 0
        irows_ref[it] = 0
        return c

    lax.fori_loop(cnt_ref[0], ITEM_TABLE, idle, 0)


def _plan(ptab):
    smem = pl.BlockSpec(memory_space=pltpu.SMEM)
    return pl.pallas_call(
        _plan_kernel,
        in_specs=[smem],
        out_specs=[smem] * 4,
        out_shape=[jax.ShapeDtypeStruct((ITEM_TABLE,), I32)] * 4,
        scratch_shapes=[pltpu.SMEM((1,), I32)],
        name="moe_plan",
    )(ptab)


def _unpack_rows(p):
    lo = pltpu.unpack_elementwise(p, index=0, packed_dtype=BF16, unpacked_dtype=F32)
    hi = pltpu.unpack_elementwise(p, index=1, packed_dtype=BF16, unpacked_dtype=F32)
    return jnp.concatenate([lo.astype(BF16), hi.astype(BF16)], axis=1)


def _pack_rows(y):
    return pltpu.pack_elementwise([y[:, :PACKED_W], y[:, PACKED_W:]], packed_dtype=BF16)


def _expert_kernel(iexp_ref, ib0_ref, ib1_ref, irows_ref, ptab_ref, stab_ref, xsb_hbm, w1_ref, w2_ref,
                   b1_ref, b2_ref, ysb_hbm, xg_ref, g_ref, a_ref, yb_ref, sem_big, sem_small):
    it = pl.program_id(0)
    ph = pl.program_id(1)
    e = iexp_ref[it]
    b0 = ib0_ref[it]
    b1 = ib1_ref[it]
    rows = irows_ref[it]
    n_sub = (rows + (SUB - 1)) >> SUB_SHIFT

    def aligned(v):
        return v if isinstance(v, int) else pl.multiple_of(v, CHUNK)

    def copy(to_vmem, b, src, dst, size, sem):
        hbm = (xsb_hbm if to_vmem else ysb_hbm).at[b, pl.ds(aligned(src), size), :]
        if to_vmem:
            return pltpu.make_async_copy(hbm, xg_ref.at[pl.ds(aligned(dst), size), :], sem)
        return pltpu.make_async_copy(yb_ref.at[pl.ds(aligned(dst), size), :], hbm, sem)

    def move_segments(to_vmem):
        def per_tile(b, carry):
            dst, n_big, n_small = carry
            pb = ptab_ref[b * N_EXPERTS + e]
            src = stab_ref[b * N_EXPERTS + e]
            nb = pb >> BIG_SHIFT
            ns = (pb - nb * BIG) >> CHUNK_SHIFT

            def big(c, z):
                copy(to_vmem, b, src + c * BIG, dst + c * BIG, BIG, sem_big).start()
                return z

            def small(c, z):
                off = nb * BIG + c * CHUNK
                copy(to_vmem, b, src + off, dst + off, CHUNK, sem_small).start()
                return z

            lax.fori_loop(0, nb, big, 0)
            lax.fori_loop(0, ns, small, 0)
            return dst + pb, n_big + nb, n_small + ns

        zero = jnp.int32(0)
        _, n_big, n_small = lax.fori_loop(b0, b1, per_tile, (zero, zero, zero))

        def wait_big(c, z):
            copy(to_vmem, 0, 0, 0, BIG, sem_big).wait()
            return z

        def wait_small(c, z):
            copy(to_vmem, 0, 0, 0, CHUNK, sem_small).wait()
            return z

        lax.fori_loop(0, n_big, wait_big, 0)
        lax.fori_loop(0, n_small, wait_small, 0)

    @pl.when((it == 0) & (ph == 0))
    def _init():
        xg_ref[...] = jnp.zeros_like(xg_ref)

    @pl.when((ph == 0) & (rows > 0))
    def _gate():
        move_segments(True)

        def body(i, c):
            r = pl.ds(pl.multiple_of(i * SUB, SUB), SUB)
            x = _unpack_rows(xg_ref[r, :])
            g_ref[r, :] = jnp.dot(x, w1_ref[...].astype(BF16), preferred_element_type=F32) + b1_ref[:, :D_FF]
            return c

        lax.fori_loop(0, n_sub, body, 0)

    @pl.when((ph == 1) & (rows > 0))
    def _up():
        def body(i, c):
            r = pl.ds(pl.multiple_of(i * SUB, SUB), SUB)
            x = _unpack_rows(xg_ref[r, :])
            up = jnp.dot(x, w1_ref[...].astype(BF16), preferred_element_type=F32) + b1_ref[:, D_FF:]
            gate = jnp.minimum(g_ref[r, :], SWIGLU_LIMIT)
            up = jnp.clip(up, -SWIGLU_LIMIT, SWIGLU_LIMIT)
            a_ref[r, :] = (gate * jax.nn.sigmoid(SWIGLU_ALPHA * gate) * (up + 1.0)).astype(BF16)
            return c

        lax.fori_loop(0, n_sub, body, 0)

    @pl.when((ph == 2) & (rows > 0))
    def _down():
        def body(i, c):
            r = pl.ds(pl.multiple_of(i * SUB, SUB), SUB)
            y = jnp.dot(a_ref[r, :], w2_ref[...].astype(BF16), preferred_element_type=F32) + b2_ref[...]
            yb_ref[r, :] = _pack_rows(y)
            return c

        lax.fori_loop(0, n_sub, body, 0)
        move_segments(False)


def _experts(iexp, ib0, ib1, irows, ptab, stab, xsb, l, w1, b1, w2, b2):
    n_prefetch = 6
    return pl.pallas_call(
        _expert_kernel,
        grid_spec=pltpu.PrefetchScalarGridSpec(
            num_scalar_prefetch=n_prefetch,
            grid=(MAX_ITEMS, 3),
            in_specs=[pl.BlockSpec(memory_space=pl.ANY),
                      pl.BlockSpec((None, None, D_MODEL, D_FF), lambda it, ph, ie, *_: (l, ie[it], 0, jnp.minimum(ph, 1))),
                      pl.BlockSpec((None, None, D_FF, D_MODEL), lambda it, ph, ie, *_: (l, ie[it], 0, 0)),
                      pl.BlockSpec((None, None, 1, 2 * D_FF), lambda it, ph, ie, *_: (l, ie[it], 0, 0)),
                      pl.BlockSpec((None, None, 1, D_MODEL), lambda it, ph, ie, *_: (l, ie[it], 0, 0))],
            out_specs=pl.BlockSpec(memory_space=pl.ANY),
            scratch_shapes=[pltpu.VMEM((ITEM_ROWS, PACKED_W), I32),
                            pltpu.VMEM((ITEM_ROWS, D_FF), F32),
                            pltpu.VMEM((ITEM_ROWS, D_FF), BF16),
                            pltpu.VMEM((ITEM_ROWS, PACKED_W), I32),
                            pltpu.SemaphoreType.DMA(()),
                            pltpu.SemaphoreType.DMA(())],
        ),
        out_shape=jax.ShapeDtypeStruct((N_TILES, TILE_ROWS, PACKED_W), I32),
        input_output_aliases={n_prefetch: 0},
        compiler_params=_cparams(("arbitrary", "arbitrary")),
        name="moe_experts",
    )(iexp, ib0, ib1, irows, ptab, stab, xsb, w1, w2,
      b1.reshape(DEPTH, N_EXPERTS, 1, 2 * D_FF), b2.reshape(DEPTH, N_EXPERTS, 1, D_MODEL))


def _combine_kernel(x_ref, ysb_ref, route_ref, eye_ref, mod_ref, g_ref, b_ref, o_ref):
    rt = lax.dot_general(eye_ref[...], route_ref[...], (((1,), (1,)), ((), ())), precision=HIGHEST,
                         preferred_element_type=F32)
    lane = lax.broadcasted_iota(I32, (MOE_TM, TILE_ROWS), 1)
    c_hi = jnp.zeros((MOE_TM, TILE_ROWS), F32)
    c_lo = jnp.zeros((MOE_TM, TILE_ROWS), F32)
    for k in range(TOP_K):
        hit = lane == rt[:, k:k + 1].astype(I32)
        w = rt[:, TOP_K + k:TOP_K + k + 1]
        w_hi = w.astype(BF16).astype(F32)
        c_hi = jnp.where(hit, w_hi, c_hi)
        c_lo = jnp.where(hit, w - w_hi, c_lo)
    y = _unpack_rows(ysb_ref[...])
    f = (jnp.dot(c_hi.astype(BF16), y, preferred_element_type=F32)
         + jnp.dot(c_lo.astype(BF16), y, preferred_element_type=F32))
    o_ref[...] = _layer_norm(DEEPNORM_ALPHA * x_ref[...] + mod_ref[5:6, :] * f, g_ref[...], b_ref[...])


def _combine(x1, ysb, route, mod, l, g, b):
    row = pl.BlockSpec((MOE_TM, D_MODEL), lambda i: (i, 0))
    return pl.pallas_call(
        _combine_kernel,
        grid=(N_TILES,),
        in_specs=[row, pl.BlockSpec((None, TILE_ROWS, PACKED_W), lambda i: (i, 0, 0)),
                  pl.BlockSpec((None, 2 * TOP_K, MOE_TM), lambda i: (i, 0, 0)),
                  _full((MOE_TM, MOE_TM)), _mod_spec(l, MOE_TM), _full((1, D_MODEL)), _full((1, D_MODEL))],
        out_specs=row,
        out_shape=jax.ShapeDtypeStruct((N_TOK, D_MODEL), F32),
        compiler_params=_cparams(("arbitrary",)),
        name="moe_combine",
    )(x1, ysb, route, jnp.eye(MOE_TM, dtype=F32), mod, g.reshape(1, D_MODEL), b.reshape(1, D_MODEL))


def _moe_layer(x1, mod, l, rw, rb, w1, b1, w2, b2, ln_g, ln_b):
    xsb, route, tab = _dispatch(x1, mod, l, rw, rb)
    ptab = tab[:, :, 1].astype(I32).reshape(N_SEG)
    stab = tab[:, :, 2].astype(I32).reshape(N_SEG)
    iexp, ib0, ib1, irows = _plan(ptab)
    ysb = _experts(iexp, ib0, ib1, irows, ptab, stab, xsb, l, w1, b1, w2, b2)
    return _combine(x1, ysb, route, mod, l, ln_g, ln_b)


def kernel(x_prompt, x_sample, cache_k_attn, cache_v_attn, cache_k_na, cache_v_na, c, c_ctx, w_mod, b_mod, ln1_g, ln1_b, ln2_g, ln2_b, conv_w_in, conv_w, conv_b, conv_w_out, attn_w_qkv, attn_q_norm, attn_k_norm, attn_w_o, na_w_qkv, na_rpb, na_w_o, router_w, router_b, moe_w1, moe_b1, moe_w2, moe_b2):
    x = jnp.concatenate([x_prompt.reshape(N_PROMPT, D_MODEL), x_sample.reshape(N_SAMPLE, D_MODEL)], axis=0)
    cond8 = jnp.concatenate([c_ctx[None, :], c, jnp.zeros((8 - 1 - DEC_BATCH, D_MODEL), F32)], axis=0)
    mod = _adaln_all(cond8, w_mod, b_mod)
    new_kv = {}
    for l in range(DEPTH):
        kind, j = l % 3, l // 3
        if kind == 0:
            x1 = _conv_layer(x, mod, l, conv_w_in[j], conv_w[j], conv_b[j], conv_w_out[j], ln1_g[l], ln1_b[l])
        elif kind == 1:
            gq = jnp.tile(attn_q_norm[j], N_HEADS).reshape(1, N_HEADS * HEAD_DIM)
            gk = jnp.tile(attn_k_norm[j], N_KV_HEADS).reshape(1, N_KV_HEADS * HEAD_DIM)
            norm_args = (gq, gk, _block_diag_mean())
            wk = N_KV_HEADS * HEAD_DIM
            xp1, nk, nv = _attn_call(x, mod, l, False, attn_w_qkv[j], attn_w_o[j], ln1_g[l], ln1_b[l],
                                     n_seq=BATCH, seq_len=SEQ, tq=SEQ, n_kv=N_KV_HEADS, norm_args=norm_args,
                                     emit_kv=True, name="gqa_prompt")
            new_kv["k_attn"] = nk.reshape(BATCH, 1, SEQ, N_KV_HEADS, HEAD_DIM)
            new_kv["v_attn"] = nv.reshape(BATCH, 1, SEQ, N_KV_HEADS, HEAD_DIM)
            (xs1,) = _attn_call(x, mod, l, True, attn_w_qkv[j], attn_w_o[j], ln1_g[l], ln1_b[l],
                                n_seq=DEC_BATCH, seq_len=DEC_SEQ, tq=256, n_kv=N_KV_HEADS, norm_args=norm_args,
                                rope_args=_rope_tables(),
                                ctx_args=(cache_k_attn[:, j].reshape(DEC_BATCH, PAST_LEN, wk),
                                          cache_v_attn[:, j].reshape(DEC_BATCH, PAST_LEN, wk)),
                                name="gqa_sample")
            x1 = jnp.concatenate([xp1, xs1], axis=0)
        else:
            wk = N_HEADS * HEAD_DIM
            xp1, nk, nv = _attn_call(x, mod, l, False, na_w_qkv[j], na_w_o[j], ln1_g[l], ln1_b[l],
                                     n_seq=BATCH, seq_len=SEQ, tq=SEQ, n_kv=N_HEADS, emit_kv=True,
                                     name="mha_prompt")
            new_kv["k_na"] = nk.reshape(BATCH, 1, SEQ, N_HEADS, HEAD_DIM)
            new_kv["v_na"] = nv.reshape(BATCH, 1, SEQ, N_HEADS, HEAD_DIM)
            (xs1,) = _attn_call(x, mod, l, True, na_w_qkv[j], na_w_o[j], ln1_g[l], ln1_b[l],
                                n_seq=DEC_BATCH, seq_len=DEC_SEQ, tq=GRID_W, n_kv=N_HEADS,
                                ctx_args=(cache_k_na[:, j].reshape(DEC_BATCH, PAST_LEN, wk),
                                          cache_v_na[:, j].reshape(DEC_BATCH, PAST_LEN, wk)),
                                bias=_na_bias_table(na_rpb[j]), name="na_sample")
            x1 = jnp.concatenate([xp1, xs1], axis=0)
        x = _moe_layer(x1, mod, l, router_w[l], router_b[l], moe_w1, moe_b1, moe_w2, moe_b2,
                       ln2_g[l], ln2_b[l])
    y_prompt = x[:N_PROMPT].reshape(BATCH, SEQ, D_MODEL)
    y_sample = x[N_PROMPT:].reshape(DEC_BATCH, DEC_SEQ, D_MODEL)
    return (y_prompt, y_sample, new_kv["k_attn"], new_kv["v_attn"], new_kv["k_na"], new_kv["v_na"])
```

```python
import functools

import numpy as np
import jax
import jax.numpy as jnp
from jax import lax
from jax.experimental import pallas as pl
from jax.experimental.pallas import tpu as pltpu

D_MODEL = 1024
BATCH = 16
SEQ = 256
DEPTH = 4
DEC_BATCH = 2
DEC_SEQ = 1024
PAST_LEN = 256
GRID_W = 64
HEAD_DIM = 64
N_HEADS = 16
N_KV_HEADS = 4
ROPE_THETA = 10000.0
ROPE_PAIRS = HEAD_DIM // 4
WIN_R = 8
WIN_C = 16
N_EXPERTS = 32
TOP_K = 4
D_FF = D_MODEL
SWIGLU_LIMIT = 7.0
SWIGLU_ALPHA = 1.702
DEEPNORM_ALPHA = (2 * DEPTH) ** 0.25
LN_EPS = 1e-5
RMS_EPS = 1e-6

F32 = jnp.float32
BF16 = jnp.bfloat16
I32 = jnp.int32
HIGHEST = lax.Precision.HIGHEST

N_PROMPT = BATCH * SEQ
N_SAMPLE = DEC_BATCH * DEC_SEQ
N_TOK = N_PROMPT + N_SAMPLE
N_ASSIGN = N_TOK * TOP_K

LANES = 128
SUBLANES = 8
PACKED_W = D_MODEL // 2

VMEM_LIMIT = 58 * 1024 * 1024


def _cparams(sem):
    return pltpu.CompilerParams(dimension_semantics=sem, vmem_limit_bytes=VMEM_LIMIT)


def _layer_norm(x, g, b):
    mu = jnp.mean(x, -1, keepdims=True)
    xc = x - mu
    var = jnp.mean(xc * xc, -1, keepdims=True)
    return xc * lax.rsqrt(var + LN_EPS) * g + b


def _mod_row(i, tile):
    n_prompt_tiles = N_PROMPT // tile
    return jnp.where(i < n_prompt_tiles, 0, 1 + (i - n_prompt_tiles) // (DEC_SEQ // tile))


def _mod_spec(l, tile):
    return pl.BlockSpec((None, None, 6, D_MODEL), lambda i, *_: (l, _mod_row(i, tile), 0, 0))


def _full(shape):
    nd = len(shape)
    return pl.BlockSpec(shape, lambda *_: (0,) * nd)


ADALN_TN = 1536


def _adaln_kernel(cond_ref, w_ref, b_ref, o_ref):
    c = cond_ref[...]
    s = c * jax.nn.sigmoid(c)
    o_ref[...] = jnp.dot(s, w_ref[...], precision=HIGHEST, preferred_element_type=F32) + b_ref[...]


def _adaln_all(cond8, w_mod, b_mod):
    n = 6 * D_MODEL
    out = pl.pallas_call(
        _adaln_kernel,
        grid=(DEPTH, n // ADALN_TN),
        in_specs=[
            pl.BlockSpec((8, D_MODEL), lambda l, j: (0, 0)),
            pl.BlockSpec((None, D_MODEL, ADALN_TN), lambda l, j: (l, 0, j)),
            pl.BlockSpec((None, 1, ADALN_TN), lambda l, j: (l, 0, j)),
        ],
        out_specs=pl.BlockSpec((None, 8, ADALN_TN), lambda l, j: (l, 0, j)),
        out_shape=jax.ShapeDtypeStruct((DEPTH, 8, n), F32),
        compiler_params=_cparams(("arbitrary", "arbitrary")),
        name="adaln",
    )(cond8, w_mod, b_mod.reshape(DEPTH, 1, n))
    return out.reshape(DEPTH, 8, 6, D_MODEL)


CONV_TM = 1024


def _conv_kernel(x_ref, mod_ref, win_ref, cw_ref, cb_ref, wout_ref, g_ref, b_ref, o_ref):
    i = pl.program_id(0)
    x = x_ref[...]
    h = (x * (1.0 + mod_ref[1:2, :]) + mod_ref[0:1, :]).astype(BF16)
    gc = jnp.dot(h, win_ref[:, D_MODEL:2 * D_MODEL], preferred_element_type=F32)
    xv = jnp.dot(h, win_ref[:, 2 * D_MODEL:], preferred_element_type=F32)
    u = gc * xv
    seq_len = jnp.where(i < N_PROMPT // CONV_TM, SEQ, DEC_SEQ)
    t = lax.broadcasted_iota(I32, (CONV_TM, 1), 0) & (seq_len - 1)
    u_prev = jnp.where(t == 0, 0.0, pltpu.roll(u, 1, axis=0))
    u_next = jnp.where(t == seq_len - 1, 0.0, pltpu.roll(u, CONV_TM - 1, axis=0))
    y = u_prev * cw_ref[0:1, :] + u * cw_ref[1:2, :] + u_next * cw_ref[2:3, :] + cb_ref[...]
    gb = jnp.dot(h, win_ref[:, :D_MODEL], preferred_element_type=F32)
    v = (gb * y).astype(BF16)
    o = jnp.dot(v, wout_ref[...], preferred_element_type=F32)
    o_ref[...] = _layer_norm(DEEPNORM_ALPHA * x + mod_ref[2:3, :] * o, g_ref[...], b_ref[...])


def _conv_layer(x, mod, l, w_in, cw, cb, w_out, ln_g, ln_b):
    row = pl.BlockSpec((CONV_TM, D_MODEL), lambda i: (i, 0))
    return pl.pallas_call(
        _conv_kernel,
        grid=(N_TOK // CONV_TM,),
        in_specs=[row, _mod_spec(l, CONV_TM), _full((D_MODEL, 3 * D_MODEL)), _full((3, D_MODEL)),
                  _full((1, D_MODEL)), _full((D_MODEL, D_MODEL)), _full((1, D_MODEL)), _full((1, D_MODEL))],
        out_specs=row,
        out_shape=jax.ShapeDtypeStruct((N_TOK, D_MODEL), F32),
        compiler_params=_cparams(("arbitrary",)),
        name="conv_mixer",
    )(x, mod, w_in.astype(BF16), cw, cb.reshape(1, D_MODEL), w_out.astype(BF16),
      ln_g.reshape(1, D_MODEL), ln_b.reshape(1, D_MODEL))


ATTN_CHUNK = 256


def _attn_kernel(*refs, seq_len, tq, n_kv, norm, rope, n_ctx, na, emit_kv):
    refs = list(refs)
    x_ref, mod_ref, wqkv_ref = refs[:3]
    pos = 3
    if norm:
        gq_ref, gk_ref, bd_ref = refs[pos:pos + 3]
        pos += 3
    if rope:
        cos_ref, s1_ref, s2_ref = refs[pos:pos + 3]
        pos += 3
    if n_ctx:
        ck_ref, cv_ref = refs[pos:pos + 2]
        pos += 2
    if na:
        bias_ref = refs[pos]
        pos += 1
    wo_ref, lng_ref, lnb_ref = refs[pos:pos + 3]
    pos += 3
    o_ref = refs[pos]
    pos += 1
    if emit_kv:
        nk_ref, nv_ref = refs[pos:pos + 2]
        pos += 2
    q_scr, k_scr, v_scr, o_scr = refs[pos:pos + 4]

    qt = pl.program_id(1)
    n_qt = seq_len // tq
    wq = N_HEADS * HEAD_DIM
    wk = n_kv * HEAD_DIM
    rep = N_HEADS // n_kv

    def rms(v, g_ref, width):
        sq = v * v
        hi = sq.astype(BF16)
        lo = (sq - hi.astype(F32)).astype(BF16)
        bd = bd_ref[:width, :width]
        ms = jnp.dot(hi, bd, preferred_element_type=F32) + jnp.dot(lo, bd, preferred_element_type=F32)
        return v * lax.rsqrt(ms + RMS_EPS) * g_ref[...]

    def rot(v, rows, width):
        def tab(ref):
            t = ref[rows, :]
            return jnp.concatenate([t] * (width // LANES), axis=1)

        return (v * tab(cos_ref) + pltpu.roll(v, width - ROPE_PAIRS, axis=1) * tab(s1_ref)
                + pltpu.roll(v, ROPE_PAIRS, axis=1) * tab(s2_ref))

    chunk = min(seq_len, ATTN_CHUNK)

    @pl.when(qt == 0)
    def _project():
        def body(ci, carry):
            r0 = pl.multiple_of(ci * chunk, chunk)
            rows = pl.ds(r0, chunk)
            h = (x_ref[rows, :] * (1.0 + mod_ref[1:2, :]) + mod_ref[0:1, :]).astype(BF16)
            q = jnp.dot(h, wqkv_ref[:, :wq], preferred_element_type=F32)
            k = jnp.dot(h, wqkv_ref[:, wq:wq + wk], preferred_element_type=F32)
            v = jnp.dot(h, wqkv_ref[:, wq + wk:], preferred_element_type=F32)
            if norm:
                q = rms(q, gq_ref, wq)
                k = rms(k, gk_ref, wk)
            if emit_kv:
                nk_ref[rows, :] = k
                nv_ref[rows, :] = v
            if rope:
                q = rot(q, rows, wq)
                k = rot(k, rows, wk)
            q_scr[rows, :] = (q * (HEAD_DIM ** -0.5)).astype(BF16)
            k_scr[pl.ds(n_ctx + r0, chunk), :] = k.astype(BF16)
            v_scr[pl.ds(n_ctx + r0, chunk), :] = v.astype(BF16)
            return carry

        lax.fori_loop(0, seq_len // chunk, body, 0)
        if n_ctx:
            k_scr[:n_ctx, :] = ck_ref[...].astype(BF16)
            v_scr[:n_ctx, :] = cv_ref[...].astype(BF16)

    q0 = pl.multiple_of(qt * tq, tq)
    if na:
        row_start = jnp.clip(qt - WIN_R // 2, 0, DEC_SEQ // GRID_W - WIN_R)
        d0 = row_start - qt + (WIN_R - 1)
        k0 = pl.multiple_of(n_ctx + row_start * GRID_W, GRID_W)
    for hd in range(N_HEADS):
        g = hd // rep
        hs = slice(hd * HEAD_DIM, (hd + 1) * HEAD_DIM)
        gs = slice(g * HEAD_DIM, (g + 1) * HEAD_DIM)
        qh = q_scr[pl.ds(q0, tq), hs]
        dn = (((1,), (1,)), ((), ()))
        if na:
            kc, vc = k_scr[:n_ctx, gs], v_scr[:n_ctx, gs]
            kl, vl = k_scr[pl.ds(k0, WIN_R * GRID_W), gs], v_scr[pl.ds(k0, WIN_R * GRID_W), gs]
            bias = jnp.concatenate([bias_ref[hd, pl.ds(d0 + 2 * j, 1)][0] for j in range(WIN_R // 2)], axis=1)
            s = jnp.concatenate([lax.dot_general(qh, kc, dn, preferred_element_type=F32),
                                 lax.dot_general(qh, kl, dn, preferred_element_type=F32) + bias], axis=1)
        else:
            s = lax.dot_general(qh, k_scr[:, gs], dn, preferred_element_type=F32)
        e = jnp.exp(s - jnp.max(s, axis=1, keepdims=True))
        den = jnp.sum(e, axis=1, keepdims=True)
        eb = e.astype(BF16)
        if na:
            oh = (jnp.dot(eb[:, :n_ctx], vc, preferred_element_type=F32)
                  + jnp.dot(eb[:, n_ctx:], vl, preferred_element_type=F32))
        else:
            oh = jnp.dot(eb, v_scr[:, gs], preferred_element_type=F32)
        o_scr[pl.ds(q0, tq), hs] = (oh / den).astype(BF16)

    @pl.when(qt == n_qt - 1)
    def _finish():
        def body(ci, carry):
            rows = pl.ds(pl.multiple_of(ci * chunk, chunk), chunk)
            o = jnp.dot(o_scr[rows, :], wo_ref[...], preferred_element_type=F32)
            o_ref[rows, :] = _layer_norm(DEEPNORM_ALPHA * x_ref[rows, :] + mod_ref[2:3, :] * o,
                                         lng_ref[...], lnb_ref[...])
            return carry

        lax.fori_loop(0, seq_len // chunk, body, 0)


def _attn_call(x, mod, l, latent, w_qkv, w_o, ln_g, ln_b, *, n_seq, seq_len, tq, n_kv, norm_args=None,
               rope_args=None, ctx_args=None, bias=None, emit_kv=False, name="attn"):
    wq = N_HEADS * HEAD_DIM
    wk = n_kv * HEAD_DIM
    n_ctx = PAST_LEN if ctx_args is not None else 0
    n_qt = seq_len // tq
    seq_off = N_PROMPT // seq_len if latent else 0
    seq_spec = pl.BlockSpec((seq_len, D_MODEL), lambda s, t: (s, 0))
    mod_spec = pl.BlockSpec((None, None, 6, D_MODEL), lambda s, t: (l, (1 + s) if latent else 0, 0, 0))
    in_specs = [pl.BlockSpec((seq_len, D_MODEL), lambda s, t: (s + seq_off, 0)), mod_spec,
                pl.BlockSpec((D_MODEL, wq + 2 * wk), lambda s, t: (0, 0))]
    args = [x, mod, w_qkv.astype(BF16)]
    if norm_args is not None:
        gq, gk, bd = norm_args
        in_specs += [pl.BlockSpec((1, wq), lambda s, t: (0, 0)), pl.BlockSpec((1, wk), lambda s, t: (0, 0)),
                     pl.BlockSpec((wq, wq), lambda s, t: (0, 0))]
        args += [gq, gk, bd]
    if rope_args is not None:
        in_specs += [pl.BlockSpec((seq_len, LANES), lambda s, t: (0, 0))] * 3
        args += list(rope_args)
    if ctx_args is not None:
        in_specs += [pl.BlockSpec((None, n_ctx, wk), lambda s, t: (s, 0, 0))] * 2
        args += list(ctx_args)
    if bias is not None:
        in_specs += [pl.BlockSpec(bias.shape, lambda s, t: (0, 0, 0, 0))]
        args += [bias]
    in_specs += [pl.BlockSpec((D_MODEL, D_MODEL), lambda s, t: (0, 0)),
                 pl.BlockSpec((1, D_MODEL), lambda s, t: (0, 0)), pl.BlockSpec((1, D_MODEL), lambda s, t: (0, 0))]
    args += [w_o.astype(BF16), ln_g.reshape(1, D_MODEL), ln_b.reshape(1, D_MODEL)]
    out_specs = [seq_spec]
    out_shape = [jax.ShapeDtypeStruct((n_seq * seq_len, D_MODEL), F32)]
    if emit_kv:
        out_specs += [pl.BlockSpec((seq_len, wk), lambda s, t: (s, 0))] * 2
        out_shape += [jax.ShapeDtypeStruct((n_seq * seq_len, wk), F32)] * 2
    kern = functools.partial(_attn_kernel, seq_len=seq_len, tq=tq, n_kv=n_kv, norm=norm_args is not None,
                             rope=rope_args is not None, n_ctx=n_ctx, na=bias is not None, emit_kv=emit_kv)
    return pl.pallas_call(
        kern,
        grid=(n_seq, n_qt),
        in_specs=in_specs,
        out_specs=out_specs,
        out_shape=out_shape,
        scratch_shapes=[pltpu.VMEM((seq_len, wq), BF16), pltpu.VMEM((n_ctx + seq_len, wk), BF16),
                        pltpu.VMEM((n_ctx + seq_len, wk), BF16), pltpu.VMEM((seq_len, wq), BF16)],
        compiler_params=_cparams(("arbitrary", "arbitrary")),
        name=name,
    )(*args)


def _rope_tables():
    t = np.arange(DEC_SEQ)
    pos = np.stack([t // GRID_W, t % GRID_W], axis=1).astype(np.float64)
    inv = (ROPE_THETA ** (-np.arange(ROPE_PAIRS, dtype=np.float32) / ROPE_PAIRS)).astype(np.float64)
    ang = pos[:, :, None] * inv[None, None, :]
    cos, sin = np.cos(ang), np.sin(ang)
    zero = np.zeros_like(sin)
    cos_t = np.concatenate([cos, cos], axis=2).reshape(DEC_SEQ, HEAD_DIM)
    s1_t = np.concatenate([-sin, zero], axis=2).reshape(DEC_SEQ, HEAD_DIM)
    s2_t = np.concatenate([zero, sin], axis=2).reshape(DEC_SEQ, HEAD_DIM)
    return tuple(jnp.asarray(np.tile(a, (1, LANES // HEAD_DIM)), F32) for a in (cos_t, s1_t, s2_t))


def _block_diag_mean():
    a = np.kron(np.eye(N_HEADS), np.full((HEAD_DIM, HEAD_DIM), 1.0 / HEAD_DIM))
    return jnp.asarray(a, BF16)


def _na_bias_table(rpb):
    cols = np.arange(GRID_W)
    col_start = np.clip(cols - WIN_C // 2, 0, GRID_W - WIN_C)
    kc = np.arange(GRID_W)
    inside = (kc[None, :] >= col_start[:, None]) & (kc[None, :] < col_start[:, None] + WIN_C)
    off = np.clip(kc[None, :] - cols[:, None] + (WIN_C - 1), 0, 2 * WIN_C - 2)
    blocks = jnp.where(jnp.asarray(inside)[None, None], rpb[:, :, off], -1e30)
    return jnp.concatenate([blocks[:, :-1], blocks[:, 1:]], axis=-1)


MOE_TM = 256
N_TILES = N_TOK // MOE_TM
CHUNK_SHIFT, BIG_SHIFT, SUB_SHIFT = 3, 5, 8
CHUNK = 1 << CHUNK_SHIFT
BIG = 1 << BIG_SHIFT
TILE_ROWS = 1280
SUB = 1 << SUB_SHIFT
ITEM_ROWS = 2048
MAX_ITEMS = 48
ITEM_TABLE = 64
N_SEG =N_TILES * N_EXPERTS

assert TILE_ROWS >= MOE_TM * TOP_K + N_EXPERTS * (CHUNK - 1) and TILE_ROWS % SUB == 0
assert MAX_ITEMS >= N_EXPERTS + (N_ASSIGN + N_SEG * (CHUNK - 1) - 1) // (ITEM_ROWS - MOE_TM)


def _dispatch_kernel(x_ref, mod_ref, rwt_ref, rbc_ref, upper_ref, lower_ref, xsb_ref, route_ref, tab_ref):
    h2 = x_ref[...] * (1.0 + mod_ref[4:5, :]) + mod_ref[3:4, :]
    logits = lax.dot_general(rwt_ref[...], h2, (((1,), (1,)), ((), ())), precision=HIGHEST,
                             preferred_element_type=F32) + rbc_ref[...]
    sub = lax.broadcasted_iota(I32, (N_EXPERTS, MOE_TM), 0)
    vals, hots = [], []
    cur = logits
    for _ in range(TOP_K):
        m = jnp.max(cur, axis=0, keepdims=True)
        idx = jnp.min(jnp.where(cur == m, sub, N_EXPERTS), axis=0, keepdims=True)
        hot = sub == idx
        vals.append(m)
        hots.append(hot)
        cur = jnp.where(hot, -jnp.inf, cur)
    exps = [jnp.exp(v - vals[0]) for v in vals]
    den = (exps[0] + exps[1]) + (exps[2] + exps[3])
    mask = jnp.zeros((N_EXPERTS, MOE_TM), F32)
    for hot in hots:
        mask = mask + jnp.where(hot, 1.0, 0.0)
    before = jnp.dot(mask.astype(BF16), upper_ref[...], preferred_element_type=F32)
    n = jnp.sum(mask, axis=1, keepdims=True)
    p = (((n.astype(I32) + (CHUNK - 1)) >> CHUNK_SHIFT) << CHUNK_SHIFT).astype(F32)
    start = jnp.dot(lower_ref[...], jnp.broadcast_to(p, (N_EXPERTS, LANES)), precision=HIGHEST,
                    preferred_element_type=F32)[:, :1]
    base = start + before
    lps = [jnp.sum(jnp.where(hot, base, 0.0), axis=0, keepdims=True) for hot in hots]
    route_ref[...] = jnp.concatenate(lps + [e / den for e in exps], axis=0)
    lane = lax.broadcasted_iota(I32, (N_EXPERTS, LANES), 1)
    tab_ref[...] = jnp.where(lane == 0, n, jnp.where(lane == 1, p, jnp.where(lane == 2, start, 0.0)))
    jrow = lax.broadcasted_iota(I32, (TILE_ROWS, MOE_TM), 0)
    onehot = jnp.zeros((TILE_ROWS, MOE_TM), F32)
    for lp in lps:
        onehot = jnp.where(jrow == lp.astype(I32), 1.0, onehot)
    xs = jnp.dot(onehot.astype(BF16), h2.astype(BF16), preferred_element_type=F32)
    xsb_ref[...] = pltpu.pack_elementwise([xs[:, :PACKED_W], xs[:, PACKED_W:]], packed_dtype=BF16)


def _dispatch(x1, mod, l, rw, rb):
    upper = jnp.asarray(np.triu(np.ones((MOE_TM, MOE_TM)), 1), BF16)
    lower = jnp.asarray(np.tril(np.ones((N_EXPERTS, N_EXPERTS)), -1), F32)
    return pl.pallas_call(
        _dispatch_kernel,
        grid=(N_TILES,),
        in_specs=[pl.BlockSpec((MOE_TM, D_MODEL), lambda i: (i, 0)), _mod_spec(l, MOE_TM),
                  _full((N_EXPERTS, D_MODEL)), _full((N_EXPERTS, 1)), _full((MOE_TM, MOE_TM)),
                  _full((N_EXPERTS, N_EXPERTS))],
        out_specs=[pl.BlockSpec((None, TILE_ROWS, PACKED_W), lambda i: (i, 0, 0)),
                   pl.BlockSpec((None, 2 * TOP_K, MOE_TM), lambda i: (i, 0, 0)),
                   pl.BlockSpec((None, N_EXPERTS, LANES), lambda i: (i, 0, 0))],
        out_shape=[jax.ShapeDtypeStruct((N_TILES, TILE_ROWS, PACKED_W), I32),
                   jax.ShapeDtypeStruct((N_TILES, 2 * TOP_K, MOE_TM), F32),
                   jax.ShapeDtypeStruct((N_TILES, N_EXPERTS, LANES), F32)],
        compiler_params=_cparams(("arbitrary",)),
        name="moe_dispatch",
    )(x1, mod, rw.T, rb.reshape(N_EXPERTS, 1), upper, lower)


def _plan_kernel(p_ref, iexp_ref, ib0_ref, ib1_ref, irows_ref, cnt_ref):
    cnt_ref[0] = 0

    def emit(e, b0, b1, rows):
        it = cnt_ref[0]
        iexp_ref[it] = e
        ib0_ref[it] = b0
        ib1_ref[it] = b1
        irows_ref[it] = rows
        cnt_ref[0] = it + 1

    def per_expert(e, last_e):
        def per_tile(b, carry):
            b0, rows = carry
            pb = p_ref[b * N_EXPERTS + e]
            full = rows + pb > ITEM_ROWS

            @pl.when(full)
            def _():
                emit(e, b0, b, rows)

            return jnp.where(full, b, b0), jnp.where(full, pb, rows + pb)

        b0, rows = lax.fori_loop(0, N_TILES, per_tile, (jnp.int32(0), jnp.int32(0)))

        @pl.when(rows > 0)
        def _():
            emit(e, b0, jnp.int32(N_TILES), rows)

        return jnp.where(rows > 0, e, last_e)

    last_e = lax.fori_loop(0, N_EXPERTS, per_expert, jnp.int32(0))

    def idle(it, c):
        iexp_ref[it] = last_e
        ib0_ref[it] = 0
        ib1_ref[it] = 0
        irows_ref[it] = 0
        return c

    lax.fori_loop(cnt_ref[0], ITEM_TABLE, idle, 0)


def _plan(ptab):
    smem = pl.BlockSpec(memory_space=pltpu.SMEM)
    return pl.pallas_call(
        _plan_kernel,
        in_specs=[smem],
        out_specs=[smem] * 4,
        out_shape=[jax.ShapeDtypeStruct((ITEM_TABLE,), I32)] * 4,
        scratch_shapes=[pltpu.SMEM((1,), I32)],
        name="moe_plan",
    )(ptab)


def _unpack_halves(p):
    lo = pltpu.unpack_elementwise(p, index=0, packed_dtype=BF16, unpacked_dtype=F32)
    hi = pltpu.unpack_elementwise(p, index=1, packed_dtype=BF16, unpacked_dtype=F32)
    return lo.astype(BF16), hi.astype(BF16)


def _unpack_rows(p):
    return jnp.concatenate(_unpack_halves(p), axis=1)


def _pack_rows(y):
    return pltpu.pack_elementwise([y[:, :PACKED_W], y[:, PACKED_W:]], packed_dtype=BF16)


def _expert_kernel(iexp_ref, ib0_ref, ib1_ref, irows_ref, ptab_ref, stab_ref, xsb_hbm, w1_hbm, w2_hbm,
                   b1_ref, b2_ref, ysb_hbm, xg_ref, g_ref, a_ref, yb_ref, w_ref, sem_big, sem_small, w_sem,
                   *, layer):
    it = pl.program_id(0)
    ph = pl.program_id(1)
    e = iexp_ref[it]
    b0 = ib0_ref[it]
    b1 = ib1_ref[it]
    rows = irows_ref[it]

    def weight_copies(item, phase):
        ee = iexp_ref[item]
        copies = []
        for k in range(2):
            half = pl.ds(k * (D_MODEL // 2), D_MODEL // 2)
            if phase < 2:
                src = w1_hbm.at[layer, ee, half, pl.ds(phase * D_FF, D_FF)]
            else:
                src = w2_hbm.at[layer, ee, half, :]
            copies.append(pltpu.make_async_copy(src, w_ref.at[phase, half, :], w_sem.at[phase]))
        return copies

    def start_weights(item, phase):
        @pl.when(irows_ref[item] > 0)
        def _():
            for cp in weight_copies(item, phase):
                cp.start()

    def wait_weights(phase):
        for cp in weight_copies(it, phase):
            cp.wait()

    def aligned(v):
        return v if isinstance(v, int) else pl.multiple_of(v, CHUNK)

    def copy(to_vmem, b, src, dst, size, sem):
        hbm = (xsb_hbm if to_vmem else ysb_hbm).at[b, pl.ds(aligned(src), size), :]
        if to_vmem:
            return pltpu.make_async_copy(hbm, xg_ref.at[pl.ds(aligned(dst), size), :], sem)
        return pltpu.make_async_copy(yb_ref.at[pl.ds(aligned(dst), size), :], hbm, sem)

    def move_segments(to_vmem):
        def per_tile(b, carry):
            dst, n_big, n_small = carry
            pb = ptab_ref[b * N_EXPERTS + e]
            src = stab_ref[b * N_EXPERTS + e]
            nb = pb >> BIG_SHIFT
            ns = (pb - nb * BIG) >> CHUNK_SHIFT

            def big(c, z):
                copy(to_vmem, b, src + c * BIG, dst + c * BIG, BIG, sem_big).start()
                return z

            def small(c, z):
                off = nb * BIG + c * CHUNK
                copy(to_vmem, b, src + off, dst + off, CHUNK, sem_small).start()
                return z

            lax.fori_loop(0, nb, big, 0)
            lax.fori_loop(0, ns, small, 0)
            return dst + pb, n_big + nb, n_small + ns

        zero = jnp.int32(0)
        _, n_big, n_small = lax.fori_loop(b0, b1, per_tile, (zero, zero, zero))

        def wait_big(c, z):
            copy(to_vmem, 0, 0, 0, BIG, sem_big).wait()
            return z

        def wait_small(c, z):
            copy(to_vmem, 0, 0, 0, CHUNK, sem_small).wait()
            return z

        lax.fori_loop(0, n_big, wait_big, 0)
        lax.fori_loop(0, n_small, wait_small, 0)

    @pl.when((it == 0) & (ph == 0))
    def _init():
        xg_ref[...] = jnp.zeros_like(xg_ref)

    def for_sub_tiles(fn):
        n_full = rows >> SUB_SHIFT
        rem = rows - (n_full << SUB_SHIFT)

        def body(i, c):
            fn(pl.ds(pl.multiple_of(i * SUB, SUB), SUB))
            return c

        lax.fori_loop(0, n_full, body, 0)
        r0 = pl.multiple_of(n_full * SUB, SUB)

        @pl.when((rem > 0) & (rem <= SUB // 2))
        def _():
            fn(pl.ds(r0, SUB // 2))

        @pl.when(rem > SUB // 2)
        def _():
            fn(pl.ds(r0, SUB))

    def matmul(lo, hi, phase):
        half = D_MODEL // 2
        return (jnp.dot(lo, w_ref[phase, :half, :].astype(BF16), preferred_element_type=F32)
                + jnp.dot(hi, w_ref[phase, half:, :].astype(BF16), preferred_element_type=F32))

    @pl.when((it == 0) & (ph == 0))
    def _first_weights():
        start_weights(0, 0)
        start_weights(0, 1)

    @pl.when((ph == 0) & (rows > 0))
    def _gate():
        start_weights(it, 2)
        move_segments(True)
        wait_weights(0)

        def gate(r):
            lo, hi = _unpack_halves(xg_ref[r, :])
            g_ref[r, :] = matmul(lo, hi, 0) + b1_ref[:, :D_FF]

        for_sub_tiles(gate)

    @pl.when((ph == 1) & (rows > 0))
    def _up():
        start_weights(it + 1, 0)
        wait_weights(1)

        def up_act(r):
            lo, hi = _unpack_halves(xg_ref[r, :])
            up = matmul(lo, hi, 1) + b1_ref[:, D_FF:]
            gate = jnp.minimum(g_ref[r, :], SWIGLU_LIMIT)
            up = jnp.clip(up, -SWIGLU_LIMIT, SWIGLU_LIMIT)
            a_ref[r, :] = (gate * jax.nn.sigmoid(SWIGLU_ALPHA * gate) * (up + 1.0)).astype(BF16)

        for_sub_tiles(up_act)

    @pl.when((ph == 2) & (rows > 0))
    def _down():
        start_weights(it + 1, 1)
        wait_weights(2)

        def down(r):
            y = matmul(a_ref[r, :D_FF // 2], a_ref[r, D_FF // 2:], 2) + b2_ref[...]
            yb_ref[r, :] = _pack_rows(y)

        for_sub_tiles(down)
        move_segments(False)


def _experts(iexp, ib0, ib1, irows, ptab, stab, xsb, l, w1, b1, w2, b2):
    n_prefetch = 6
    any_spec = pl.BlockSpec(memory_space=pl.ANY)
    return pl.pallas_call(
        functools.partial(_expert_kernel, layer=l),
        grid_spec=pltpu.PrefetchScalarGridSpec(
            num_scalar_prefetch=n_prefetch,
            grid=(MAX_ITEMS, 3),
            in_specs=[any_spec, any_spec, any_spec,
                      pl.BlockSpec((None, None, 1, 2 * D_FF), lambda it, ph, ie, *_: (l, ie[it], 0, 0)),
                      pl.BlockSpec((None, None, 1, D_MODEL), lambda it, ph, ie, *_: (l, ie[it], 0, 0))],
            out_specs=pl.BlockSpec(memory_space=pl.ANY),
            scratch_shapes=[pltpu.VMEM((ITEM_ROWS, PACKED_W), I32),
                            pltpu.VMEM((ITEM_ROWS, D_FF), F32),
                            pltpu.VMEM((ITEM_ROWS, D_FF), BF16),
                            pltpu.VMEM((ITEM_ROWS, PACKED_W), I32),
                            pltpu.VMEM((3, D_MODEL, D_FF), F32),
                            pltpu.SemaphoreType.DMA(()),
                            pltpu.SemaphoreType.DMA(()),
                            pltpu.SemaphoreType.DMA((3,))],
        ),
        out_shape=jax.ShapeDtypeStruct((N_TILES, TILE_ROWS, PACKED_W), I32),
        input_output_aliases={n_prefetch: 0},
        compiler_params=_cparams(("arbitrary", "arbitrary")),
        name="moe_experts",
    )(iexp, ib0, ib1, irows, ptab, stab, xsb, w1, w2,
      b1.reshape(DEPTH, N_EXPERTS, 1, 2 * D_FF), b2.reshape(DEPTH, N_EXPERTS, 1, D_MODEL))


def _combine_kernel(x_ref, ysb_ref, route_ref, eye_ref, mod_ref, g_ref, b_ref, o_ref):
    rt = lax.dot_general(eye_ref[...], route_ref[...], (((1,), (1,)), ((), ())), precision=HIGHEST,
                         preferred_element_type=F32)
    lane = lax.broadcasted_iota(I32, (MOE_TM, TILE_ROWS), 1)
    c_hi = jnp.zeros((MOE_TM, TILE_ROWS), F32)
    c_lo = jnp.zeros((MOE_TM, TILE_ROWS), F32)
    for k in range(TOP_K):
        hit = lane == rt[:, k:k + 1].astype(I32)
        w = rt[:, TOP_K + k:TOP_K + k + 1]
        w_hi = w.astype(BF16).astype(F32)
        c_hi = jnp.where(hit, w_hi, c_hi)
        c_lo = jnp.where(hit, w - w_hi, c_lo)
    y = _unpack_rows(ysb_ref[...])
    f = (jnp.dot(c_hi.astype(BF16), y, preferred_element_type=F32)
         + jnp.dot(c_lo.astype(BF16), y, preferred_element_type=F32))
    o_ref[...] = _layer_norm(DEEPNORM_ALPHA * x_ref[...] + mod_ref[5:6, :] * f, g_ref[...], b_ref[...])


def _combine(x1, ysb, route, mod, l, g, b):
    row = pl.BlockSpec((MOE_TM, D_MODEL), lambda i: (i, 0))
    return pl.pallas_call(
        _combine_kernel,
        grid=(N_TILES,),
        in_specs=[row, pl.BlockSpec((None, TILE_ROWS, PACKED_W), lambda i: (i, 0, 0)),
                  pl.BlockSpec((None, 2 * TOP_K, MOE_TM), lambda i: (i, 0, 0)),
                  _full((MOE_TM, MOE_TM)), _mod_spec(l, MOE_TM), _full((1, D_MODEL)), _full((1, D_MODEL))],
        out_specs=row,
        out_shape=jax.ShapeDtypeStruct((N_TOK, D_MODEL), F32),
        compiler_params=_cparams(("arbitrary",)),
        name="moe_combine",
    )(x1, ysb, route, jnp.eye(MOE_TM, dtype=F32), mod, g.reshape(1, D_MODEL), b.reshape(1, D_MODEL))


def _moe_layer(x1, mod, l, rw, rb, w1, b1, w2, b2, ln_g, ln_b):
    xsb, route, tab = _dispatch(x1, mod, l, rw, rb)
    ptab = tab[:, :, 1].astype(I32).reshape(N_SEG)
    stab = tab[:, :, 2].astype(I32).reshape(N_SEG)
    iexp, ib0, ib1, irows = _plan(ptab)
    ysb = _experts(iexp, ib0, ib1, irows, ptab, stab, xsb, l, w1, b1, w2, b2)
    return _combine(x1, ysb, route, mod, l, ln_g, ln_b)


def kernel(x_prompt, x_sample, cache_k_attn, cache_v_attn, cache_k_na, cache_v_na, c, c_ctx, w_mod, b_mod, ln1_g, ln1_b, ln2_g, ln2_b, conv_w_in, conv_w, conv_b, conv_w_out, attn_w_qkv, attn_q_norm, attn_k_norm, attn_w_o, na_w_qkv, na_rpb, na_w_o, router_w, router_b, moe_w1, moe_b1, moe_w2, moe_b2):
    x = jnp.concatenate([x_prompt.reshape(N_PROMPT, D_MODEL), x_sample.reshape(N_SAMPLE, D_MODEL)], axis=0)
    cond8 = jnp.concatenate([c_ctx[None, :], c, jnp.zeros((8 - 1 - DEC_BATCH, D_MODEL), F32)], axis=0)
    mod = _adaln_all(cond8, w_mod, b_mod)
    new_kv = {}
    for l in range(DEPTH):
        kind, j = l % 3, l // 3
        if kind == 0:
            x1 = _conv_layer(x, mod, l, conv_w_in[j], conv_w[j], conv_b[j], conv_w_out[j], ln1_g[l], ln1_b[l])
        elif kind == 1:
            gq = jnp.tile(attn_q_norm[j], N_HEADS).reshape(1, N_HEADS * HEAD_DIM)
            gk = jnp.tile(attn_k_norm[j], N_KV_HEADS).reshape(1, N_KV_HEADS * HEAD_DIM)
            norm_args = (gq, gk, _block_diag_mean())
            wk = N_KV_HEADS * HEAD_DIM
            xp1, nk, nv = _attn_call(x, mod, l, False, attn_w_qkv[j], attn_w_o[j], ln1_g[l], ln1_b[l],
                                     n_seq=BATCH, seq_len=SEQ, tq=SEQ, n_kv=N_KV_HEADS, norm_args=norm_args,
                                     emit_kv=True, name="gqa_prompt")
            new_kv["k_attn"] = nk.reshape(BATCH, 1, SEQ, N_KV_HEADS, HEAD_DIM)
            new_kv["v_attn"] = nv.reshape(BATCH, 1, SEQ, N_KV_HEADS, HEAD_DIM)
            (xs1,) = _attn_call(x, mod, l, True, attn_w_qkv[j], attn_w_o[j], ln1_g[l], ln1_b[l],
                                n_seq=DEC_BATCH, seq_len=DEC_SEQ, tq=256, n_kv=N_KV_HEADS, norm_args=norm_args,
                                rope_args=_rope_tables(),
                                ctx_args=(cache_k_attn[:, j].reshape(DEC_BATCH, PAST_LEN, wk),
                                          cache_v_attn[:, j].reshape(DEC_BATCH, PAST_LEN, wk)),
                                name="gqa_sample")
            x1 = jnp.concatenate([xp1, xs1], axis=0)
        else:
            wk = N_HEADS * HEAD_DIM
            xp1, nk, nv = _attn_call(x, mod, l, False, na_w_qkv[j], na_w_o[j], ln1_g[l], ln1_b[l],
                                     n_seq=BATCH, seq_len=SEQ, tq=SEQ, n_kv=N_HEADS, emit_kv=True,
                                     name="mha_prompt")
            new_kv["k_na"] = nk.reshape(BATCH, 1, SEQ, N_HEADS, HEAD_DIM)
            new_kv["v_na"] = nv.reshape(BATCH, 1, SEQ, N_HEADS, HEAD_DIM)
            (xs1,) = _attn_call(x, mod, l, True, na_w_qkv[j], na_w_o[j], ln1_g[l], ln1_b[l],
                                n_seq=DEC_BATCH, seq_len=DEC_SEQ, tq=GRID_W, n_kv=N_HEADS,
                                ctx_args=(cache_k_na[:, j].reshape(DEC_BATCH, PAST_LEN, wk),
                                          cache_v_na[:, j].reshape(DEC_BATCH, PAST_LEN, wk)),
                                bias=_na_bias_table(na_rpb[j]), name="na_sample")
            x1 = jnp.concatenate([xp1, xs1], axis=0)
        x = _moe_layer(x1, mod, l, router_w[l], router_b[l], moe_w1, moe_b1, moe_w2, moe_b2,
                       ln2_g[l], ln2_b[l])
    y_prompt = x[:N_PROMPT].reshape(BATCH, SEQ, D_MODEL)
    y_sample = x[N_PROMPT:].reshape(DEC_BATCH, DEC_SEQ, D_MODEL)
    return (y_prompt, y_sample, new_kv["k_attn"], new_kv["v_attn"], new_kv["k_na"], new_kv["v_na"])
```

```python
import functools

import numpy as np
import jax
import jax.numpy as jnp
from jax import lax
from jax.experimental import pallas as pl
from jax.experimental.pallas import tpu as pltpu

D_MODEL = 1024
BATCH = 16
SEQ = 256
DEPTH = 4
DEC_BATCH = 2
DEC_SEQ = 1024
PAST_LEN = 256
GRID_W = 64
HEAD_DIM = 64
N_HEADS = 16
N_KV_HEADS = 4
ROPE_THETA = 10000.0
ROPE_PAIRS = HEAD_DIM // 4
WIN_R = 8
WIN_C = 16
N_EXPERTS = 32
TOP_K = 4
D_FF = D_MODEL
SWIGLU_LIMIT = 7.0
SWIGLU_ALPHA = 1.702
DEEPNORM_ALPHA = (2 * DEPTH) ** 0.25
LN_EPS = 1e-5
RMS_EPS = 1e-6

F32 = jnp.float32
BF16 = jnp.bfloat16
I32 = jnp.int32
HIGHEST = lax.Precision.HIGHEST

N_PROMPT = BATCH * SEQ
N_SAMPLE = DEC_BATCH * DEC_SEQ
N_TOK = N_PROMPT + N_SAMPLE
N_ASSIGN = N_TOK * TOP_K

LANES = 128
SUBLANES = 8
PACKED_W = D_MODEL // 2

VMEM_LIMIT = 58 * 1024 * 1024


def _cparams(sem):
    return pltpu.CompilerParams(dimension_semantics=sem, vmem_limit_bytes=VMEM_LIMIT)


def _layer_norm(x, g, b):
    mu = jnp.mean(x, -1, keepdims=True)
    xc = x - mu
    var = jnp.mean(xc * xc, -1, keepdims=True)
    return xc * lax.rsqrt(var + LN_EPS) * g + b


def _mod_row(i, tile):
    n_prompt_tiles = N_PROMPT // tile
    return jnp.where(i < n_prompt_tiles, 0, 1 + (i - n_prompt_tiles) // (DEC_SEQ // tile))


def _mod_spec(l, tile):
    return pl.BlockSpec((None, None, 6, D_MODEL), lambda i, *_: (l, _mod_row(i, tile), 0, 0))


def _full(shape):
    nd = len(shape)
    return pl.BlockSpec(shape, lambda *_: (0,) * nd)


ADALN_TN = 1536


def _adaln_kernel(cond_ref, w_ref, b_ref, o_ref):
    c = cond_ref[...]
    s = c * jax.nn.sigmoid(c)
    o_ref[...] = jnp.dot(s, w_ref[...], precision=HIGHEST, preferred_element_type=F32) + b_ref[...]


def _adaln_all(cond8, w_mod, b_mod):
    n = 6 * D_MODEL
    out = pl.pallas_call(
        _adaln_kernel,
        grid=(DEPTH, n // ADALN_TN),
        in_specs=[
            pl.BlockSpec((8, D_MODEL), lambda l, j: (0, 0)),
            pl.BlockSpec((None, D_MODEL, ADALN_TN), lambda l, j: (l, 0, j)),
            pl.BlockSpec((None, 1, ADALN_TN), lambda l, j: (l, 0, j)),
        ],
        out_specs=pl.BlockSpec((None, 8, ADALN_TN), lambda l, j: (l, 0, j)),
        out_shape=jax.ShapeDtypeStruct((DEPTH, 8, n), F32),
        compiler_params=_cparams(("arbitrary", "arbitrary")),
        name="adaln",
    )(cond8, w_mod, b_mod.reshape(DEPTH, 1, n))
    return out.reshape(DEPTH, 8, 6, D_MODEL)


CONV_TM = 1024


def _conv_kernel(x_ref, mod_ref, win_ref, cw_ref, cb_ref, wout_ref, g_ref, b_ref, o_ref):
    i = pl.program_id(0)
    x = x_ref[...]
    h = (x * (1.0 + mod_ref[1:2, :]) + mod_ref[0:1, :]).astype(BF16)
    gc = jnp.dot(h, win_ref[:, D_MODEL:2 * D_MODEL], preferred_element_type=F32)
    xv = jnp.dot(h, win_ref[:, 2 * D_MODEL:], preferred_element_type=F32)
    u = gc * xv
    seq_len = jnp.where(i < N_PROMPT // CONV_TM, SEQ, DEC_SEQ)
    t = lax.broadcasted_iota(I32, (CONV_TM, 1), 0) & (seq_len - 1)
    u_prev = jnp.where(t == 0, 0.0, pltpu.roll(u, 1, axis=0))
    u_next = jnp.where(t == seq_len - 1, 0.0, pltpu.roll(u, CONV_TM - 1, axis=0))
    y = u_prev * cw_ref[0:1, :] + u * cw_ref[1:2, :] + u_next * cw_ref[2:3, :] + cb_ref[...]
    gb = jnp.dot(h, win_ref[:, :D_MODEL], preferred_element_type=F32)
    v = (gb * y).astype(BF16)
    o = jnp.dot(v, wout_ref[...], preferred_element_type=F32)
    o_ref[...] = _layer_norm(DEEPNORM_ALPHA * x + mod_ref[2:3, :] * o, g_ref[...], b_ref[...])


def _conv_layer(x, mod, l, w_in, cw, cb, w_out, ln_g, ln_b):
    row = pl.BlockSpec((CONV_TM, D_MODEL), lambda i: (i, 0))
    return pl.pallas_call(
        _conv_kernel,
        grid=(N_TOK // CONV_TM,),
        in_specs=[row, _mod_spec(l, CONV_TM), _full((D_MODEL, 3 * D_MODEL)), _full((3, D_MODEL)),
                  _full((1, D_MODEL)), _full((D_MODEL, D_MODEL)), _full((1, D_MODEL)), _full((1, D_MODEL))],
        out_specs=row,
        out_shape=jax.ShapeDtypeStruct((N_TOK, D_MODEL), F32),
        compiler_params=_cparams(("arbitrary",)),
        name="conv_mixer",
    )(x, mod, w_in.astype(BF16), cw, cb.reshape(1, D_MODEL), w_out.astype(BF16),
      ln_g.reshape(1, D_MODEL), ln_b.reshape(1, D_MODEL))


ATTN_CHUNK = 256


def _attn_kernel(*refs, seq_len, tq, n_kv, norm, rope, n_ctx, na, emit_kv):
    refs = list(refs)
    x_ref, mod_ref, wqkv_ref = refs[:3]
    pos = 3
    if norm:
        gq_ref, gk_ref, bd_ref = refs[pos:pos + 3]
        pos += 3
    if rope:
        cos_ref, s1_ref, s2_ref = refs[pos:pos + 3]
        pos += 3
    if n_ctx:
        ck_ref, cv_ref = refs[pos:pos + 2]
        pos += 2
    if na:
        bias_ref = refs[pos]
        pos += 1
    wo_ref, lng_ref, lnb_ref = refs[pos:pos + 3]
    pos += 3
    o_ref = refs[pos]
    pos += 1
    if emit_kv:
        nk_ref, nv_ref = refs[pos:pos + 2]
        pos += 2
    q_scr, k_scr, v_scr, o_scr = refs[pos:pos + 4]

    qt = pl.program_id(1)
    n_qt = seq_len // tq
    wq = N_HEADS * HEAD_DIM
    wk = n_kv * HEAD_DIM
    rep = N_HEADS // n_kv

    def rms(v, g_ref, width):
        ms = jnp.dot((v * v).astype(BF16), bd_ref[:width, :width], preferred_element_type=F32)
        return v * lax.rsqrt(ms + RMS_EPS) * g_ref[...]

    def rot(v, rows, width):
        def tab(ref):
            t = ref[rows, :]
            return jnp.concatenate([t] * (width // LANES), axis=1)

        return (v * tab(cos_ref) + pltpu.roll(v, width - ROPE_PAIRS, axis=1) * tab(s1_ref)
                + pltpu.roll(v, ROPE_PAIRS, axis=1) * tab(s2_ref))

    chunk = min(seq_len, ATTN_CHUNK)

    @pl.when(qt == 0)
    def _project():
        def body(ci, carry):
            r0 = pl.multiple_of(ci * chunk, chunk)
            rows = pl.ds(r0, chunk)
            h = (x_ref[rows, :] * (1.0 + mod_ref[1:2, :]) + mod_ref[0:1, :]).astype(BF16)
            q = jnp.dot(h, wqkv_ref[:, :wq], preferred_element_type=F32)
            k = jnp.dot(h, wqkv_ref[:, wq:wq + wk], preferred_element_type=F32)
            v = jnp.dot(h, wqkv_ref[:, wq + wk:], preferred_element_type=F32)
            if norm:
                q = rms(q, gq_ref, wq)
                k = rms(k, gk_ref, wk)
            if emit_kv:
                nk_ref[rows, :] = k
                nv_ref[rows, :] = v
            if rope:
                q = rot(q, rows, wq)
                k = rot(k, rows, wk)
            q_scr[rows, :] = (q * (HEAD_DIM ** -0.5)).astype(BF16)
            k_scr[pl.ds(n_ctx + r0, chunk), :] = k.astype(BF16)
            v_scr[pl.ds(n_ctx + r0, chunk), :] = v.astype(BF16)
            return carry

        lax.fori_loop(0, seq_len // chunk, body, 0)
        if n_ctx:
            k_scr[:n_ctx, :] = ck_ref[...].astype(BF16)
            v_scr[:n_ctx, :] = cv_ref[...].astype(BF16)

    q0 = pl.multiple_of(qt * tq, tq)
    if na:
        row_start = jnp.clip(qt - WIN_R // 2, 0, DEC_SEQ // GRID_W - WIN_R)
        d0 = row_start - qt + (WIN_R - 1)
        k0 = pl.multiple_of(n_ctx + row_start * GRID_W, GRID_W)
    for hd in range(N_HEADS):
        g = hd // rep
        hs = slice(hd * HEAD_DIM, (hd + 1) * HEAD_DIM)
        gs = slice(g * HEAD_DIM, (g + 1) * HEAD_DIM)
        qh = q_scr[pl.ds(q0, tq), hs]
        dn = (((1,), (1,)), ((), ()))
        if na:
            kc, vc = k_scr[:n_ctx, gs], v_scr[:n_ctx, gs]
            kl, vl = k_scr[pl.ds(k0, WIN_R * GRID_W), gs], v_scr[pl.ds(k0, WIN_R * GRID_W), gs]
            bias = jnp.concatenate([bias_ref[hd, pl.ds(d0 + 2 * j, 1)][0] for j in range(WIN_R // 2)], axis=1)
            s = jnp.concatenate([lax.dot_general(qh, kc, dn, preferred_element_type=F32),
                                 lax.dot_general(qh, kl, dn, preferred_element_type=F32) + bias], axis=1)
        else:
            s = lax.dot_general(qh, k_scr[:, gs], dn, preferred_element_type=F32)
        e = jnp.exp(s - jnp.max(s, axis=1, keepdims=True))
        den = jnp.sum(e, axis=1, keepdims=True)
        eb = e.astype(BF16)
        if na:
            oh = (jnp.dot(eb[:, :n_ctx], vc, preferred_element_type=F32)
                  + jnp.dot(eb[:, n_ctx:], vl, preferred_element_type=F32))
        else:
            oh = jnp.dot(eb, v_scr[:, gs], preferred_element_type=F32)
        o_scr[pl.ds(q0, tq), hs] = (oh / den).astype(BF16)

    @pl.when(qt == n_qt - 1)
    def _finish():
        def body(ci, carry):
            rows = pl.ds(pl.multiple_of(ci * chunk, chunk), chunk)
            o = jnp.dot(o_scr[rows, :], wo_ref[...], preferred_element_type=F32)
            o_ref[rows, :] = _layer_norm(DEEPNORM_ALPHA * x_ref[rows, :] + mod_ref[2:3, :] * o,
                                         lng_ref[...], lnb_ref[...])
            return carry

        lax.fori_loop(0, seq_len // chunk, body, 0)


def _attn_call(x, mod, l, latent, w_qkv, w_o, ln_g, ln_b, *, n_seq, seq_len, tq, n_kv, norm_args=None,
               rope_args=None, ctx_args=None, bias=None, emit_kv=False, name="attn"):
    wq = N_HEADS * HEAD_DIM
    wk = n_kv * HEAD_DIM
    n_ctx = PAST_LEN if ctx_args is not None else 0
    n_qt = seq_len // tq
    seq_off = N_PROMPT // seq_len if latent else 0
    seq_spec = pl.BlockSpec((seq_len, D_MODEL), lambda s, t: (s, 0))
    mod_spec = pl.BlockSpec((None, None, 6, D_MODEL), lambda s, t: (l, (1 + s) if latent else 0, 0, 0))
    in_specs = [pl.BlockSpec((seq_len, D_MODEL), lambda s, t: (s + seq_off, 0)), mod_spec,
                pl.BlockSpec((D_MODEL, wq + 2 * wk), lambda s, t: (0, 0))]
    args = [x, mod, w_qkv.astype(BF16)]
    if norm_args is not None:
        gq, gk, bd = norm_args
        in_specs += [pl.BlockSpec((1, wq), lambda s, t: (0, 0)), pl.BlockSpec((1, wk), lambda s, t: (0, 0)),
                     pl.BlockSpec((wq, wq), lambda s, t: (0, 0))]
        args += [gq, gk, bd]
    if rope_args is not None:
        in_specs += [pl.BlockSpec((seq_len, LANES), lambda s, t: (0, 0))] * 3
        args += list(rope_args)
    if ctx_args is not None:
        in_specs += [pl.BlockSpec((None, n_ctx, wk), lambda s, t: (s, 0, 0))] * 2
        args += list(ctx_args)
    if bias is not None:
        in_specs += [pl.BlockSpec(bias.shape, lambda s, t: (0, 0, 0, 0))]
        args += [bias]
    in_specs += [pl.BlockSpec((D_MODEL, D_MODEL), lambda s, t: (0, 0)),
                 pl.BlockSpec((1, D_MODEL), lambda s, t: (0, 0)), pl.BlockSpec((1, D_MODEL), lambda s, t: (0, 0))]
    args += [w_o.astype(BF16), ln_g.reshape(1, D_MODEL), ln_b.reshape(1, D_MODEL)]
    out_specs = [seq_spec]
    out_shape = [jax.ShapeDtypeStruct((n_seq * seq_len, D_MODEL), F32)]
    if emit_kv:
        out_specs += [pl.BlockSpec((seq_len, wk), lambda s, t: (s, 0))] * 2
        out_shape += [jax.ShapeDtypeStruct((n_seq * seq_len, wk), F32)] * 2
    kern = functools.partial(_attn_kernel, seq_len=seq_len, tq=tq, n_kv=n_kv, norm=norm_args is not None,
                             rope=rope_args is not None, n_ctx=n_ctx, na=bias is not None, emit_kv=emit_kv)
    return pl.pallas_call(
        kern,
        grid=(n_seq, n_qt),
        in_specs=in_specs,
        out_specs=out_specs,
        out_shape=out_shape,
        scratch_shapes=[pltpu.VMEM((seq_len, wq), BF16), pltpu.VMEM((n_ctx + seq_len, wk), BF16),
                        pltpu.VMEM((n_ctx + seq_len, wk), BF16), pltpu.VMEM((seq_len, wq), BF16)],
        compiler_params=_cparams(("arbitrary", "arbitrary")),
        name=name,
    )(*args)


def _rope_tables():
    t = np.arange(DEC_SEQ)
    pos = np.stack([t // GRID_W, t % GRID_W], axis=1).astype(np.float64)
    inv = (ROPE_THETA ** (-np.arange(ROPE_PAIRS, dtype=np.float32) / ROPE_PAIRS)).astype(np.float64)
    ang = pos[:, :, None] * inv[None, None, :]
    cos, sin = np.cos(ang), np.sin(ang)
    zero = np.zeros_like(sin)
    cos_t = np.concatenate([cos, cos], axis=2).reshape(DEC_SEQ, HEAD_DIM)
    s1_t = np.concatenate([-sin, zero], axis=2).reshape(DEC_SEQ, HEAD_DIM)
    s2_t = np.concatenate([zero, sin], axis=2).reshape(DEC_SEQ, HEAD_DIM)
    return tuple(jnp.asarray(np.tile(a, (1, LANES // HEAD_DIM)), F32) for a in (cos_t, s1_t, s2_t))


def _block_diag_mean():
    a = np.kron(np.eye(N_HEADS), np.full((HEAD_DIM, HEAD_DIM), 1.0 / HEAD_DIM))
    return jnp.asarray(a, BF16)


def _na_bias_table(rpb):
    cols = np.arange(GRID_W)
    col_start = np.clip(cols - WIN_C // 2, 0, GRID_W - WIN_C)
    kc = np.arange(GRID_W)
    inside = (kc[None, :] >= col_start[:, None]) & (kc[None, :] < col_start[:, None] + WIN_C)
    off = np.clip(kc[None, :] - cols[:, None] + (WIN_C - 1), 0, 2 * WIN_C - 2)
    blocks = jnp.where(jnp.asarray(inside)[None, None], rpb[:, :, off], -1e30)
    return jnp.concatenate([blocks[:, :-1], blocks[:, 1:]], axis=-1)


MOE_TM = 256
N_TILES = N_TOK // MOE_TM
CHUNK_SHIFT, BIG_SHIFT, SUB_SHIFT = 3, 5, 8
CHUNK = 1 << CHUNK_SHIFT
BIG = 1 << BIG_SHIFT
TILE_ROWS = 1280
SUB = 1 << SUB_SHIFT
ITEM_ROWS = 2048
MAX_ITEMS = 48
ITEM_TABLE = 64
N_SEG =N_TILES * N_EXPERTS

assert TILE_ROWS >= MOE_TM * TOP_K + N_EXPERTS * (CHUNK - 1) and TILE_ROWS % SUB == 0
assert MAX_ITEMS >= N_EXPERTS + (N_ASSIGN + N_SEG * (CHUNK - 1) - 1) // (ITEM_ROWS - MOE_TM)


def _dispatch_kernel(x_ref, mod_ref, rwt_ref, rbc_ref, upper_ref, lower_ref, xsb_ref, route_ref, tab_ref):
    h2 = x_ref[...] * (1.0 + mod_ref[4:5, :]) + mod_ref[3:4, :]
    logits = lax.dot_general(rwt_ref[...], h2, (((1,), (1,)), ((), ())), precision=HIGHEST,
                             preferred_element_type=F32) + rbc_ref[...]
    sub = lax.broadcasted_iota(I32, (N_EXPERTS, MOE_TM), 0)
    vals, hots = [], []
    cur = logits
    for _ in range(TOP_K):
        m = jnp.max(cur, axis=0, keepdims=True)
        idx = jnp.min(jnp.where(cur == m, sub, N_EXPERTS), axis=0, keepdims=True)
        hot = sub == idx
        vals.append(m)
        hots.append(hot)
        cur = jnp.where(hot, -jnp.inf, cur)
    exps = [jnp.exp(v - vals[0]) for v in vals]
    den = (exps[0] + exps[1]) + (exps[2] + exps[3])
    mask = jnp.zeros((N_EXPERTS, MOE_TM), F32)
    for hot in hots:
        mask = mask + jnp.where(hot, 1.0, 0.0)
    before = jnp.dot(mask.astype(BF16), upper_ref[...], preferred_element_type=F32)
    n = jnp.sum(mask, axis=1, keepdims=True)
    p = (((n.astype(I32) + (CHUNK - 1)) >> CHUNK_SHIFT) << CHUNK_SHIFT).astype(F32)
    start = jnp.dot(lower_ref[...], jnp.broadcast_to(p, (N_EXPERTS, LANES)), precision=HIGHEST,
                    preferred_element_type=F32)[:, :1]
    base = start + before
    lps = [jnp.sum(jnp.where(hot, base, 0.0), axis=0, keepdims=True) for hot in hots]
    route_ref[...] = jnp.concatenate(lps + [e / den for e in exps], axis=0)
    lane = lax.broadcasted_iota(I32, (N_EXPERTS, LANES), 1)
    tab_ref[...] = jnp.where(lane == 0, n, jnp.where(lane == 1, p, jnp.where(lane == 2, start, 0.0)))
    jrow = lax.broadcasted_iota(I32, (TILE_ROWS, MOE_TM), 0)
    onehot = jnp.zeros((TILE_ROWS, MOE_TM), F32)
    for lp in lps:
        onehot = jnp.where(jrow == lp.astype(I32), 1.0, onehot)
    xs = jnp.dot(onehot.astype(BF16), h2.astype(BF16), preferred_element_type=F32)
    xsb_ref[...] = pltpu.pack_elementwise([xs[:, :PACKED_W], xs[:, PACKED_W:]], packed_dtype=BF16)


def _dispatch(x1, mod, l, rw, rb):
    upper = jnp.asarray(np.triu(np.ones((MOE_TM, MOE_TM)), 1), BF16)
    lower = jnp.asarray(np.tril(np.ones((N_EXPERTS, N_EXPERTS)), -1), F32)
    return pl.pallas_call(
        _dispatch_kernel,
        grid=(N_TILES,),
        in_specs=[pl.BlockSpec((MOE_TM, D_MODEL), lambda i: (i, 0)), _mod_spec(l, MOE_TM),
                  _full((N_EXPERTS, D_MODEL)), _full((N_EXPERTS, 1)), _full((MOE_TM, MOE_TM)),
                  _full((N_EXPERTS, N_EXPERTS))],
        out_specs=[pl.BlockSpec((None, TILE_ROWS, PACKED_W), lambda i: (i, 0, 0)),
                   pl.BlockSpec((None, 2 * TOP_K, MOE_TM), lambda i: (i, 0, 0)),
                   pl.BlockSpec((None, N_EXPERTS, LANES), lambda i: (i, 0, 0))],
        out_shape=[jax.ShapeDtypeStruct((N_TILES, TILE_ROWS, PACKED_W), I32),
                   jax.ShapeDtypeStruct((N_TILES, 2 * TOP_K, MOE_TM), F32),
                   jax.ShapeDtypeStruct((N_TILES, N_EXPERTS, LANES), F32)],
        compiler_params=_cparams(("arbitrary",)),
        name="moe_dispatch",
    )(x1, mod, rw.T, rb.reshape(N_EXPERTS, 1), upper, lower)


def _plan_kernel(p_ref, iexp_ref, ib0_ref, ib1_ref, irows_ref, cnt_ref):
    cnt_ref[0] = 0

    def emit(e, b0, b1, rows):
        it = cnt_ref[0]
        iexp_ref[it] = e
        ib0_ref[it] = b0
        ib1_ref[it] = b1
        irows_ref[it] = rows
        cnt_ref[0] = it + 1

    def per_expert(e, last_e):
        def per_tile(b, carry):
            b0, rows = carry
            pb = p_ref[b * N_EXPERTS + e]
            full = rows + pb > ITEM_ROWS

            @pl.when(full)
            def _():
                emit(e, b0, b, rows)

            return jnp.where(full, b, b0), jnp.where(full, pb, rows + pb)

        b0, rows = lax.fori_loop(0, N_TILES, per_tile, (jnp.int32(0), jnp.int32(0)))

        @pl.when(rows > 0)
        def _():
            emit(e, b0, jnp.int32(N_TILES), rows)

        return jnp.where(rows > 0, e, last_e)

    last_e = lax.fori_loop(0, N_EXPERTS, per_expert, jnp.int32(0))

    def idle(it, c):
        iexp_ref[it] = last_e
        ib0_ref[it] = 0
        ib1_ref[it] = 0
        irows_ref[it] = 0
        return c

    lax.fori_loop(cnt_ref[0], ITEM_TABLE, idle, 0)


def _plan(ptab):
    smem = pl.BlockSpec(memory_space=pltpu.SMEM)
    return pl.pallas_call(
        _plan_kernel,
        in_specs=[smem],
        out_specs=[smem] * 4,
        out_shape=[jax.ShapeDtypeStruct((ITEM_TABLE,), I32)] * 4,
        scratch_shapes=[pltpu.SMEM((1,), I32)],
        name="moe_plan",
    )(ptab)


def _unpack_halves(p):
    lo = pltpu.unpack_elementwise(p, index=0, packed_dtype=BF16, unpacked_dtype=F32)
    hi = pltpu.unpack_elementwise(p, index=1, packed_dtype=BF16, unpacked_dtype=F32)
    return lo.astype(BF16), hi.astype(BF16)


def _unpack_rows(p):
    return jnp.concatenate(_unpack_halves(p), axis=1)


def _pack_rows(y):
    return pltpu.pack_elementwise([y[:, :PACKED_W], y[:, PACKED_W:]], packed_dtype=BF16)


def _expert_kernel(iexp_ref, ib0_ref, ib1_ref, irows_ref, ptab_ref, stab_ref, xsb_hbm, w1_hbm, w2_hbm,
                   b1_ref, b2_ref, ysb_hbm, xg_ref, g_ref, a_ref, yb_ref, w_ref, cnt_ref,
                   in_big, in_small, out_big, out_small, w_sem, *, layer):
    it = pl.program_id(0)
    rows = irows_ref[it]
    slot = it & 1

    def weight_copies(item, phase):
        ee = iexp_ref[item]
        copies = []
        for k in range(2):
            half = pl.ds(k * (D_MODEL // 2), D_MODEL // 2)
            if phase < 2:
                src = w1_hbm.at[layer, ee, half, pl.ds(phase * D_FF, D_FF)]
            else:
                src = w2_hbm.at[layer, ee, half, :]
            copies.append(pltpu.make_async_copy(src, w_ref.at[phase, half, :], w_sem.at[phase]))
        return copies

    def start_weights(item, phase):
        @pl.when(irows_ref[item] > 0)
        def _():
            for cp in weight_copies(item, phase):
                cp.start()

    def wait_weights(phase):
        for cp in weight_copies(it, phase):
            cp.wait()

    def aligned(v):
        return v if isinstance(v, int) else pl.multiple_of(v, CHUNK)

    def copy(to_vmem, buf, b, src, dst, size):
        hbm = (xsb_hbm if to_vmem else ysb_hbm).at[b, pl.ds(aligned(src), size), :]
        if to_vmem:
            sem = in_big if size == BIG else in_small
            return pltpu.make_async_copy(hbm, xg_ref.at[buf, pl.ds(aligned(dst), size), :], sem)
        sem = out_big if size == BIG else out_small
        return pltpu.make_async_copy(yb_ref.at[pl.ds(aligned(dst), size), :], hbm, sem)

    def start_segments(item, to_vmem, buf):
        ee = iexp_ref[item]

        def per_tile(b, carry):
            dst, n_big, n_small = carry
            pb = ptab_ref[b * N_EXPERTS + ee]
            src = stab_ref[b * N_EXPERTS + ee]
            nb = pb >> BIG_SHIFT
            ns = (pb - nb * BIG) >> CHUNK_SHIFT

            def big(c, z):
                copy(to_vmem, buf, b, src + c * BIG, dst + c * BIG, BIG).start()
                return z

            def small(c, z):
                off = nb * BIG + c * CHUNK
                copy(to_vmem, buf, b, src + off, dst + off, CHUNK).start()
                return z

            lax.fori_loop(0, nb, big, 0)
            lax.fori_loop(0, ns, small, 0)
            return dst + pb, n_big + nb, n_small + ns

        zero = jnp.int32(0)
        _, n_big, n_small = lax.fori_loop(ib0_ref[item], ib1_ref[item], per_tile, (zero, zero, zero))
        return n_big, n_small

    def wait_segments(to_vmem, n_big, n_small):
        def wait_big(c, z):
            copy(to_vmem, 0, 0, 0, 0, BIG).wait()
            return z

        def wait_small(c, z):
            copy(to_vmem, 0, 0, 0, 0, CHUNK).wait()
            return z

        lax.fori_loop(0, n_big, wait_big, 0)
        lax.fori_loop(0, n_small, wait_small, 0)

    @pl.when(it == 0)
    def _init():
        xg_ref[...] = jnp.zeros_like(xg_ref)
        start_weights(0, 0)
        start_weights(0, 1)
        cnt_ref[0], cnt_ref[1] = start_segments(0, True, 0)
        cnt_ref[2] = 0
        cnt_ref[3] = 0

    def for_sub_tiles(fn):
        n_full = rows >> SUB_SHIFT
        rem = rows - (n_full << SUB_SHIFT)

        def body(i, c):
            fn(pl.ds(pl.multiple_of(i * SUB, SUB), SUB))
            return c

        lax.fori_loop(0, n_full, body, 0)
        r0 = pl.multiple_of(n_full * SUB, SUB)

        @pl.when((rem > 0) & (rem <= SUB // 2))
        def _():
            fn(pl.ds(r0, SUB // 2))

        @pl.when(rem > SUB // 2)
        def _():
            fn(pl.ds(r0, SUB))

    def matmul(lo, hi, phase):
        half = D_MODEL // 2
        return (jnp.dot(lo, w_ref[phase, :half, :].astype(BF16), preferred_element_type=F32)
                + jnp.dot(hi, w_ref[phase, half:, :].astype(BF16), preferred_element_type=F32))

    @pl.when(rows > 0)
    def _item():
        start_weights(it, 2)
        wait_segments(True, cnt_ref[0], cnt_ref[1])
        wait_weights(0)

        def gate(r):
            lo, hi = _unpack_halves(xg_ref[slot, r, :])
            g_ref[r, :] = matmul(lo, hi, 0) + b1_ref[:, :D_FF]

        for_sub_tiles(gate)

        start_weights(it + 1, 0)
        cnt_ref[0], cnt_ref[1] = start_segments(it + 1, True, 1 - slot)
        wait_weights(1)

        def up_act(r):
            lo, hi = _unpack_halves(xg_ref[slot, r, :])
            up = matmul(lo, hi, 1) + b1_ref[:, D_FF:]
            gate = jnp.minimum(g_ref[r, :], SWIGLU_LIMIT)
            up = jnp.clip(up, -SWIGLU_LIMIT, SWIGLU_LIMIT)
            a_ref[r, :] = (gate * jax.nn.sigmoid(SWIGLU_ALPHA * gate) * (up + 1.0)).astype(BF16)

        for_sub_tiles(up_act)

        start_weights(it + 1, 1)
        wait_weights(2)
        wait_segments(False, cnt_ref[2], cnt_ref[3])

        def down(r):
            y = matmul(a_ref[r, :D_FF // 2], a_ref[r, D_FF // 2:], 2) + b2_ref[...]
            yb_ref[r, :] = _pack_rows(y)

        for_sub_tiles(down)
        cnt_ref[2], cnt_ref[3] = start_segments(it, False, 0)

    @pl.when(it == MAX_ITEMS - 1)
    def _drain():
        wait_segments(False, cnt_ref[2], cnt_ref[3])


def _experts(iexp, ib0, ib1, irows, ptab, stab, xsb, l, w1, b1, w2, b2):
    n_prefetch = 6
    any_spec = pl.BlockSpec(memory_space=pl.ANY)
    return pl.pallas_call(
        functools.partial(_expert_kernel, layer=l),
        grid_spec=pltpu.PrefetchScalarGridSpec(
            num_scalar_prefetch=n_prefetch,
            grid=(MAX_ITEMS,),
            in_specs=[any_spec, any_spec, any_spec,
                      pl.BlockSpec((None, None, 1, 2 * D_FF), lambda it, ie, *_: (l, ie[it], 0, 0)),
                      pl.BlockSpec((None, None, 1, D_MODEL), lambda it, ie, *_: (l, ie[it], 0, 0))],
            out_specs=pl.BlockSpec(memory_space=pl.ANY),
            scratch_shapes=[pltpu.VMEM((2, ITEM_ROWS, PACKED_W), I32),
                            pltpu.VMEM((ITEM_ROWS, D_FF), F32),
                            pltpu.VMEM((ITEM_ROWS, D_FF), BF16),
                            pltpu.VMEM((ITEM_ROWS, PACKED_W), I32),
                            pltpu.VMEM((3, D_MODEL, D_FF), F32),
                            pltpu.SMEM((4,), I32),
                            pltpu.SemaphoreType.DMA(()),
                            pltpu.SemaphoreType.DMA(()),
                            pltpu.SemaphoreType.DMA(()),
                            pltpu.SemaphoreType.DMA(()),
                            pltpu.SemaphoreType.DMA((3,))],
        ),
        out_shape=jax.ShapeDtypeStruct((N_TILES, TILE_ROWS, PACKED_W), I32),
        input_output_aliases={n_prefetch: 0},
        compiler_params=_cparams(("arbitrary",)),
        name="moe_experts",
    )(iexp, ib0, ib1, irows, ptab, stab, xsb, w1, w2,
      b1.reshape(DEPTH, N_EXPERTS, 1, 2 * D_FF), b2.reshape(DEPTH, N_EXPERTS, 1, D_MODEL))


def _combine_kernel(x_ref, ysb_ref, route_ref, eye_ref, mod_ref, g_ref, b_ref, o_ref):
    rt = lax.dot_general(eye_ref[...], route_ref[...], (((1,), (1,)), ((), ())), precision=HIGHEST,
                         preferred_element_type=F32)
    lane = lax.broadcasted_iota(I32, (MOE_TM, TILE_ROWS), 1)
    c_hi = jnp.zeros((MOE_TM, TILE_ROWS), F32)
    c_lo = jnp.zeros((MOE_TM, TILE_ROWS), F32)
    for k in range(TOP_K):
        hit = lane == rt[:, k:k + 1].astype(I32)
        w = rt[:, TOP_K + k:TOP_K + k + 1]
        w_hi = w.astype(BF16).astype(F32)
        c_hi = jnp.where(hit, w_hi, c_hi)
        c_lo = jnp.where(hit, w - w_hi, c_lo)
    y = _unpack_rows(ysb_ref[...])
    f = (jnp.dot(c_hi.astype(BF16), y, preferred_element_type=F32)
         + jnp.dot(c_lo.astype(BF16), y, preferred_element_type=F32))
    o_ref[...] = _layer_norm(DEEPNORM_ALPHA * x_ref[...] + mod_ref[5:6, :] * f, g_ref[...], b_ref[...])


def _combine(x1, ysb, route, mod, l, g, b):
    row = pl.BlockSpec((MOE_TM, D_MODEL), lambda i: (i, 0))
    return pl.pallas_call(
        _combine_kernel,
        grid=(N_TILES,),
        in_specs=[row, pl.BlockSpec((None, TILE_ROWS, PACKED_W), lambda i: (i, 0, 0)),
                  pl.BlockSpec((None, 2 * TOP_K, MOE_TM), lambda i: (i, 0, 0)),
                  _full((MOE_TM, MOE_TM)), _mod_spec(l, MOE_TM), _full((1, D_MODEL)), _full((1, D_MODEL))],
        out_specs=row,
        out_shape=jax.ShapeDtypeStruct((N_TOK, D_MODEL), F32),
        compiler_params=_cparams(("arbitrary",)),
        name="moe_combine",
    )(x1, ysb, route, jnp.eye(MOE_TM, dtype=F32), mod, g.reshape(1, D_MODEL), b.reshape(1, D_MODEL))


def _moe_layer(x1, mod, l, rw, rb, w1, b1, w2, b2, ln_g, ln_b):
    xsb, route, tab = _dispatch(x1, mod, l, rw, rb)
    ptab = tab[:, :, 1].astype(I32).reshape(N_SEG)
    stab = tab[:, :, 2].astype(I32).reshape(N_SEG)
    iexp, ib0, ib1, irows = _plan(ptab)
    ysb = _experts(iexp, ib0, ib1, irows, ptab, stab, xsb, l, w1, b1, w2, b2)
    return _combine(x1, ysb, route, mod, l, ln_g, ln_b)


def kernel(x_prompt, x_sample, cache_k_attn, cache_v_attn, cache_k_na, cache_v_na, c, c_ctx, w_mod, b_mod, ln1_g, ln1_b, ln2_g, ln2_b, conv_w_in, conv_w, conv_b, conv_w_out, attn_w_qkv, attn_q_norm, attn_k_norm, attn_w_o, na_w_qkv, na_rpb, na_w_o, router_w, router_b, moe_w1, moe_b1, moe_w2, moe_b2):
    x = jnp.concatenate([x_prompt.reshape(N_PROMPT, D_MODEL), x_sample.reshape(N_SAMPLE, D_MODEL)], axis=0)
    cond8 = jnp.concatenate([c_ctx[None, :], c, jnp.zeros((8 - 1 - DEC_BATCH, D_MODEL), F32)], axis=0)
    mod = _adaln_all(cond8, w_mod, b_mod)
    new_kv = {}
    for l in range(DEPTH):
        kind, j = l % 3, l // 3
        if kind == 0:
            x1 = _conv_layer(x, mod, l, conv_w_in[j], conv_w[j], conv_b[j], conv_w_out[j], ln1_g[l], ln1_b[l])
        elif kind == 1:
            gq = jnp.tile(attn_q_norm[j], N_HEADS).reshape(1, N_HEADS * HEAD_DIM)
            gk = jnp.tile(attn_k_norm[j], N_KV_HEADS).reshape(1, N_KV_HEADS * HEAD_DIM)
            norm_args = (gq, gk, _block_diag_mean())
            wk = N_KV_HEADS * HEAD_DIM
            xp1, nk, nv = _attn_call(x, mod, l, False, attn_w_qkv[j], attn_w_o[j], ln1_g[l], ln1_b[l],
                                     n_seq=BATCH, seq_len=SEQ, tq=SEQ, n_kv=N_KV_HEADS, norm_args=norm_args,
                                     emit_kv=True, name="gqa_prompt")
            new_kv["k_attn"] = nk.reshape(BATCH, 1, SEQ, N_KV_HEADS, HEAD_DIM)
            new_kv["v_attn"] = nv.reshape(BATCH, 1, SEQ, N_KV_HEADS, HEAD_DIM)
            (xs1,) = _attn_call(x, mod, l, True, attn_w_qkv[j], attn_w_o[j], ln1_g[l], ln1_b[l],
                                n_seq=DEC_BATCH, seq_len=DEC_SEQ, tq=256, n_kv=N_KV_HEADS, norm_args=norm_args,
                                rope_args=_rope_tables(),
                                ctx_args=(cache_k_attn[:, j].reshape(DEC_BATCH, PAST_LEN, wk),
                                          cache_v_attn[:, j].reshape(DEC_BATCH, PAST_LEN, wk)),
                                name="gqa_sample")
            x1 = jnp.concatenate([xp1, xs1], axis=0)
        else:
            wk = N_HEADS * HEAD_DIM
            xp1, nk, nv = _attn_call(x, mod, l, False, na_w_qkv[j], na_w_o[j], ln1_g[l], ln1_b[l],
                                     n_seq=BATCH, seq_len=SEQ, tq=SEQ, n_kv=N_HEADS, emit_kv=True,
                                     name="mha_prompt")
            new_kv["k_na"] = nk.reshape(BATCH, 1, SEQ, N_HEADS, HEAD_DIM)
            new_kv["v_na"] = nv.reshape(BATCH, 1, SEQ, N_HEADS, HEAD_DIM)
            (xs1,) = _attn_call(x, mod, l, True, na_w_qkv[j], na_w_o[j], ln1_g[l], ln1_b[l],
                                n_seq=DEC_BATCH, seq_len=DEC_SEQ, tq=GRID_W, n_kv=N_HEADS,
                                ctx_args=(cache_k_na[:, j].reshape(DEC_BATCH, PAST_LEN, wk),
                                          cache_v_na[:, j].reshape(DEC_BATCH, PAST_LEN, wk)),
                                bias=_na_bias_table(na_rpb[j]), name="na_sample")
            x1 = jnp.concatenate([xp1, xs1], axis=0)
        x = _moe_layer(x1, mod, l, router_w[l], router_b[l], moe_w1, moe_b1, moe_w2, moe_b2,
                       ln2_g[l], ln2_b[l])
    y_prompt = x[:N_PROMPT].reshape(BATCH, SEQ, D_MODEL)
    y_sample = x[N_PROMPT:].reshape(DEC_BATCH, DEC_SEQ, D_MODEL)
    return (y_prompt, y_sample, new_kv["k_attn"], new_kv["v_attn"], new_kv["k_na"], new_kv["v_na"])
```

```python
import functools

import numpy as np
import jax
import jax.numpy as jnp
from jax import lax
from jax.experimental import pallas as pl
from jax.experimental.pallas import tpu as pltpu

D_MODEL = 1024
BATCH = 16
SEQ = 256
DEPTH = 4
DEC_BATCH = 2
DEC_SEQ = 1024
PAST_LEN = 256
GRID_W = 64
HEAD_DIM = 64
N_HEADS = 16
N_KV_HEADS = 4
ROPE_THETA = 10000.0
ROPE_PAIRS = HEAD_DIM // 4
WIN_R = 8
WIN_C = 16
N_EXPERTS = 32
TOP_K = 4
D_FF = D_MODEL
SWIGLU_LIMIT = 7.0
SWIGLU_ALPHA = 1.702
DEEPNORM_ALPHA = (2 * DEPTH) ** 0.25
LN_EPS = 1e-5
RMS_EPS = 1e-6

F32 = jnp.float32
BF16 = jnp.bfloat16
I32 = jnp.int32
HIGHEST = lax.Precision.HIGHEST

N_PROMPT = BATCH * SEQ
N_SAMPLE = DEC_BATCH * DEC_SEQ
N_TOK = N_PROMPT + N_SAMPLE
N_ASSIGN = N_TOK * TOP_K

LANES = 128
SUBLANES = 8
PACKED_W = D_MODEL // 2

VMEM_LIMIT = 58 * 1024 * 1024


def _cparams(sem):
    return pltpu.CompilerParams(dimension_semantics=sem, vmem_limit_bytes=VMEM_LIMIT)


def _layer_norm(x, g, b):
    mu = jnp.mean(x, -1, keepdims=True)
    xc = x - mu
    var = jnp.mean(xc * xc, -1, keepdims=True)
    return xc * lax.rsqrt(var + LN_EPS) * g + b


def _mod_row(i, tile):
    n_prompt_tiles = N_PROMPT // tile
    return jnp.where(i < n_prompt_tiles, 0, 1 + (i - n_prompt_tiles) // (DEC_SEQ // tile))


def _mod_spec(l, tile):
    return pl.BlockSpec((None, None, 6, D_MODEL), lambda i, *_: (l, _mod_row(i, tile), 0, 0))


def _full(shape):
    nd = len(shape)
    return pl.BlockSpec(shape, lambda *_: (0,) * nd)


def _stream_specs(x, tile):
    if not isinstance(x, tuple):
        return [pl.BlockSpec((tile, D_MODEL), lambda i, *_: (i, 0))], [x]
    n_a = N_PROMPT // tile
    return ([pl.BlockSpec((tile, D_MODEL), lambda i, *_: (jnp.minimum(i, n_a - 1), 0)),
             pl.BlockSpec((tile, D_MODEL), lambda i, *_: (jnp.maximum(i - n_a, 0), 0))], list(x))


def _stream_tile(x_refs, tile):
    if len(x_refs) == 1:
        return x_refs[0][...]
    return jnp.where(pl.program_id(0) < N_PROMPT // tile, x_refs[0][...], x_refs[1][...])


ADALN_TN = 1536


def _adaln_kernel(cond_ref, w_ref, b_ref, o_ref):
    c = cond_ref[...]
    s = c * jax.nn.sigmoid(c)
    o_ref[...] = jnp.dot(s, w_ref[...], precision=HIGHEST, preferred_element_type=F32) + b_ref[...]


def _adaln_all(cond8, w_mod, b_mod):
    n = 6 * D_MODEL
    out = pl.pallas_call(
        _adaln_kernel,
        grid=(DEPTH, n // ADALN_TN),
        in_specs=[
            pl.BlockSpec((8, D_MODEL), lambda l, j: (0, 0)),
            pl.BlockSpec((None, D_MODEL, ADALN_TN), lambda l, j: (l, 0, j)),
            pl.BlockSpec((None, 1, ADALN_TN), lambda l, j: (l, 0, j)),
        ],
        out_specs=pl.BlockSpec((None, 8, ADALN_TN), lambda l, j: (l, 0, j)),
        out_shape=jax.ShapeDtypeStruct((DEPTH, 8, n), F32),
        compiler_params=_cparams(("arbitrary", "arbitrary")),
        name="adaln",
    )(cond8, w_mod, b_mod.reshape(DEPTH, 1, n))
    return out.reshape(DEPTH, 8, 6, D_MODEL)


CONV_TM = 1024


def _conv_kernel(*refs, n_x):
    mod_ref, win_ref, cw_ref, cb_ref, wout_ref, g_ref, b_ref, o_ref = refs[n_x:]
    i = pl.program_id(0)
    x = _stream_tile(refs[:n_x], CONV_TM)
    h = (x * (1.0 + mod_ref[1:2, :]) + mod_ref[0:1, :]).astype(BF16)
    gc = jnp.dot(h, win_ref[:, D_MODEL:2 * D_MODEL], preferred_element_type=F32)
    xv = jnp.dot(h, win_ref[:, 2 * D_MODEL:], preferred_element_type=F32)
    u = gc * xv
    seq_len = jnp.where(i < N_PROMPT // CONV_TM, SEQ, DEC_SEQ)
    t = lax.broadcasted_iota(I32, (CONV_TM, 1), 0) & (seq_len - 1)
    u_prev = jnp.where(t == 0, 0.0, pltpu.roll(u, 1, axis=0))
    u_next = jnp.where(t == seq_len - 1, 0.0, pltpu.roll(u, CONV_TM - 1, axis=0))
    y = u_prev * cw_ref[0:1, :] + u * cw_ref[1:2, :] + u_next * cw_ref[2:3, :] + cb_ref[...]
    gb = jnp.dot(h, win_ref[:, :D_MODEL], preferred_element_type=F32)
    v = (gb * y).astype(BF16)
    o = jnp.dot(v, wout_ref[...], preferred_element_type=F32)
    o_ref[...] = _layer_norm(DEEPNORM_ALPHA * x + mod_ref[2:3, :] * o, g_ref[...], b_ref[...])


def _conv_layer(x, mod, l, w_in, cw, cb, w_out, ln_g, ln_b):
    x_specs, x_args = _stream_specs(x, CONV_TM)
    return pl.pallas_call(
        functools.partial(_conv_kernel, n_x=len(x_args)),
        grid=(N_TOK // CONV_TM,),
        in_specs=x_specs + [_mod_spec(l, CONV_TM), _full((D_MODEL, 3 * D_MODEL)), _full((3, D_MODEL)),
                            _full((1, D_MODEL)), _full((D_MODEL, D_MODEL)), _full((1, D_MODEL)),
                            _full((1, D_MODEL))],
        out_specs=pl.BlockSpec((CONV_TM, D_MODEL), lambda i: (i, 0)),
        out_shape=jax.ShapeDtypeStruct((N_TOK, D_MODEL), F32),
        compiler_params=_cparams(("arbitrary",)),
        name="conv_mixer",
    )(*x_args, mod, w_in.astype(BF16), cw, cb.reshape(1, D_MODEL), w_out.astype(BF16),
      ln_g.reshape(1, D_MODEL), ln_b.reshape(1, D_MODEL))


ATTN_CHUNK = 256


def _attn_kernel(*refs, seq_len, tq, n_kv, norm, rope, n_ctx, na, emit_kv):
    refs = list(refs)
    x_ref, mod_ref, wqkv_ref = refs[:3]
    pos = 3
    if norm:
        gq_ref, gk_ref, bd_ref = refs[pos:pos + 3]
        pos += 3
    if rope:
        cos_ref, s1_ref, s2_ref = refs[pos:pos + 3]
        pos += 3
    if n_ctx:
        ck_ref, cv_ref = refs[pos:pos + 2]
        pos += 2
    if na:
        bias_ref = refs[pos]
        pos += 1
    wo_ref, lng_ref, lnb_ref = refs[pos:pos + 3]
    pos += 3
    o_ref = refs[pos]
    pos += 1
    if emit_kv:
        nk_ref, nv_ref = refs[pos:pos + 2]
        pos += 2
    q_scr, k_scr, v_scr, o_scr = refs[pos:pos + 4]

    qt = pl.program_id(1)
    n_qt = seq_len // tq
    wq = N_HEADS * HEAD_DIM
    wk = n_kv * HEAD_DIM
    rep = N_HEADS // n_kv

    def rms(v, g_ref, width):
        ms = jnp.dot((v * v).astype(BF16), bd_ref[:width, :width], preferred_element_type=F32)
        return v * lax.rsqrt(ms + RMS_EPS) * g_ref[...]

    def rot(v, rows, width):
        def tab(ref):
            t = ref[rows, :]
            return jnp.concatenate([t] * (width // LANES), axis=1)

        return (v * tab(cos_ref) + pltpu.roll(v, width - ROPE_PAIRS, axis=1) * tab(s1_ref)
                + pltpu.roll(v, ROPE_PAIRS, axis=1) * tab(s2_ref))

    chunk = min(seq_len, ATTN_CHUNK)

    @pl.when(qt == 0)
    def _project():
        def body(ci, carry):
            r0 = pl.multiple_of(ci * chunk, chunk)
            rows = pl.ds(r0, chunk)
            h = (x_ref[rows, :] * (1.0 + mod_ref[1:2, :]) + mod_ref[0:1, :]).astype(BF16)
            q = jnp.dot(h, wqkv_ref[:, :wq], preferred_element_type=F32)
            k = jnp.dot(h, wqkv_ref[:, wq:wq + wk], preferred_element_type=F32)
            v = jnp.dot(h, wqkv_ref[:, wq + wk:], preferred_element_type=F32)
            if norm:
                q = rms(q, gq_ref, wq)
                k = rms(k, gk_ref, wk)
            if emit_kv:
                nk_ref[rows, :] = k
                nv_ref[rows, :] = v
            if rope:
                q = rot(q, rows, wq)
                k = rot(k, rows, wk)
            q_scr[rows, :] = (q * (HEAD_DIM ** -0.5)).astype(BF16)
            k_scr[pl.ds(n_ctx + r0, chunk), :] = k.astype(BF16)
            v_scr[pl.ds(n_ctx + r0, chunk), :] = v.astype(BF16)
            return carry

        lax.fori_loop(0, seq_len // chunk, body, 0)
        if n_ctx:
            k_scr[:n_ctx, :] = ck_ref[...].astype(BF16)
            v_scr[:n_ctx, :] = cv_ref[...].astype(BF16)

    q0 = pl.multiple_of(qt * tq, tq)
    if na:
        row_start = jnp.clip(qt - WIN_R // 2, 0, DEC_SEQ // GRID_W - WIN_R)
        d0 = row_start - qt + (WIN_R - 1)
        k0 = pl.multiple_of(n_ctx + row_start * GRID_W, GRID_W)
    for hd in range(N_HEADS):
        g = hd // rep
        hs = slice(hd * HEAD_DIM, (hd + 1) * HEAD_DIM)
        gs = slice(g * HEAD_DIM, (g + 1) * HEAD_DIM)
        qh = q_scr[pl.ds(q0, tq), hs]
        dn = (((1,), (1,)), ((), ()))
        if na:
            kc, vc = k_scr[:n_ctx, gs], v_scr[:n_ctx, gs]
            kl, vl = k_scr[pl.ds(k0, WIN_R * GRID_W), gs], v_scr[pl.ds(k0, WIN_R * GRID_W), gs]
            bias = jnp.concatenate([bias_ref[hd, pl.ds(d0 + 2 * j, 1)][0] for j in range(WIN_R // 2)], axis=1)
            s = jnp.concatenate([lax.dot_general(qh, kc, dn, preferred_element_type=F32),
                                 lax.dot_general(qh, kl, dn, preferred_element_type=F32) + bias], axis=1)
        else:
            s = lax.dot_general(qh, k_scr[:, gs], dn, preferred_element_type=F32)
        e = jnp.exp(s - jnp.max(s, axis=1, keepdims=True))
        den = jnp.sum(e, axis=1, keepdims=True)
        eb = e.astype(BF16)
        if na:
            oh = (jnp.dot(eb[:, :n_ctx], vc, preferred_element_type=F32)
                  + jnp.dot(eb[:, n_ctx:], vl, preferred_element_type=F32))
        else:
            oh = jnp.dot(eb, v_scr[:, gs], preferred_element_type=F32)
        o_scr[pl.ds(q0, tq), hs] = (oh / den).astype(BF16)

    @pl.when(qt == n_qt - 1)
    def _finish():
        def body(ci, carry):
            rows = pl.ds(pl.multiple_of(ci * chunk, chunk), chunk)
            o = jnp.dot(o_scr[rows, :], wo_ref[...], preferred_element_type=F32)
            o_ref[rows, :] = _layer_norm(DEEPNORM_ALPHA * x_ref[rows, :] + mod_ref[2:3, :] * o,
                                         lng_ref[...], lnb_ref[...])
            return carry

        lax.fori_loop(0, seq_len // chunk, body, 0)


def _attn_call(x, mod, l, latent, w_qkv, w_o, ln_g, ln_b, *, n_seq, seq_len, tq, n_kv, norm_args=None,
               rope_args=None, ctx_args=None, bias=None, emit_kv=False, name="attn"):
    wq = N_HEADS * HEAD_DIM
    wk = n_kv * HEAD_DIM
    n_ctx = PAST_LEN if ctx_args is not None else 0
    n_qt = seq_len // tq
    seq_off = N_PROMPT // seq_len if latent else 0
    seq_spec = pl.BlockSpec((seq_len, D_MODEL), lambda s, t: (s, 0))
    mod_spec = pl.BlockSpec((None, None, 6, D_MODEL), lambda s, t: (l, (1 + s) if latent else 0, 0, 0))
    in_specs = [pl.BlockSpec((seq_len, D_MODEL), lambda s, t: (s + seq_off, 0)), mod_spec,
                pl.BlockSpec((D_MODEL, wq + 2 * wk), lambda s, t: (0, 0))]
    args = [x, mod, w_qkv.astype(BF16)]
    if norm_args is not None:
        gq, gk, bd = norm_args
        in_specs += [pl.BlockSpec((1, wq), lambda s, t: (0, 0)), pl.BlockSpec((1, wk), lambda s, t: (0, 0)),
                     pl.BlockSpec((wq, wq), lambda s, t: (0, 0))]
        args += [gq, gk, bd]
    if rope_args is not None:
        in_specs += [pl.BlockSpec((seq_len, LANES), lambda s, t: (0, 0))] * 3
        args += list(rope_args)
    if ctx_args is not None:
        in_specs += [pl.BlockSpec((None, n_ctx, wk), lambda s, t: (s, 0, 0))] * 2
        args += list(ctx_args)
    if bias is not None:
        in_specs += [pl.BlockSpec(bias.shape, lambda s, t: (0, 0, 0, 0))]
        args += [bias]
    in_specs += [pl.BlockSpec((D_MODEL, D_MODEL), lambda s, t: (0, 0)),
                 pl.BlockSpec((1, D_MODEL), lambda s, t: (0, 0)), pl.BlockSpec((1, D_MODEL), lambda s, t: (0, 0))]
    args += [w_o.astype(BF16), ln_g.reshape(1, D_MODEL), ln_b.reshape(1, D_MODEL)]
    out_specs = [seq_spec]
    out_shape = [jax.ShapeDtypeStruct((n_seq * seq_len, D_MODEL), F32)]
    if emit_kv:
        out_specs += [pl.BlockSpec((seq_len, wk), lambda s, t: (s, 0))] * 2
        out_shape += [jax.ShapeDtypeStruct((n_seq * seq_len, wk), F32)] * 2
    kern = functools.partial(_attn_kernel, seq_len=seq_len, tq=tq, n_kv=n_kv, norm=norm_args is not None,
                             rope=rope_args is not None, n_ctx=n_ctx, na=bias is not None, emit_kv=emit_kv)
    return pl.pallas_call(
        kern,
        grid=(n_seq, n_qt),
        in_specs=in_specs,
        out_specs=out_specs,
        out_shape=out_shape,
        scratch_shapes=[pltpu.VMEM((seq_len, wq), BF16), pltpu.VMEM((n_ctx + seq_len, wk), BF16),
                        pltpu.VMEM((n_ctx + seq_len, wk), BF16), pltpu.VMEM((seq_len, wq), BF16)],
        compiler_params=_cparams(("arbitrary", "arbitrary")),
        name=name,
    )(*args)


def _rope_tables():
    t = np.arange(DEC_SEQ)
    pos = np.stack([t // GRID_W, t % GRID_W], axis=1).astype(np.float64)
    inv = (ROPE_THETA ** (-np.arange(ROPE_PAIRS, dtype=np.float32) / ROPE_PAIRS)).astype(np.float64)
    ang = pos[:, :, None] * inv[None, None, :]
    cos, sin = np.cos(ang), np.sin(ang)
    zero = np.zeros_like(sin)
    cos_t = np.concatenate([cos, cos], axis=2).reshape(DEC_SEQ, HEAD_DIM)
    s1_t = np.concatenate([-sin, zero], axis=2).reshape(DEC_SEQ, HEAD_DIM)
    s2_t = np.concatenate([zero, sin], axis=2).reshape(DEC_SEQ, HEAD_DIM)
    return tuple(jnp.asarray(np.tile(a, (1, LANES // HEAD_DIM)), F32) for a in (cos_t, s1_t, s2_t))


def _block_diag_mean():
    a = np.kron(np.eye(N_HEADS), np.full((HEAD_DIM, HEAD_DIM), 1.0 / HEAD_DIM))
    return jnp.asarray(a, BF16)


def _bias_kernel(rpb_ref, sel_ref, inside_ref, o_ref):
    v = jnp.dot(rpb_ref[...], sel_ref[...], precision=HIGHEST, preferred_element_type=F32)
    o_ref[...] = jnp.where(inside_ref[...] > 0.0, v, -1e30)


def _na_bias_table(rpb):
    n_rows, n_off = N_HEADS * (2 * WIN_R - 1), 2 * WIN_C - 1
    cols = np.arange(GRID_W)
    col_start = np.clip(cols - WIN_C // 2, 0, GRID_W - WIN_C)
    kc = np.arange(GRID_W)
    inside = (kc[None, :] >= col_start[:, None]) & (kc[None, :] < col_start[:, None] + WIN_C)
    off = np.clip(kc[None, :] - cols[:, None] + (WIN_C - 1), 0, n_off - 1)
    sel = (np.arange(LANES)[:, None] == off.reshape(1, -1)).astype(np.float32)
    rpb2 = jnp.pad(rpb.reshape(n_rows, n_off), ((0, 0), (0, LANES - n_off)))
    blocks = pl.pallas_call(
        _bias_kernel,
        out_shape=jax.ShapeDtypeStruct((n_rows, GRID_W * GRID_W), F32),
        name="na_bias",
    )(rpb2, jnp.asarray(sel), jnp.asarray(inside.reshape(1, -1).astype(np.float32)))
    blocks = blocks.reshape(N_HEADS, 2 * WIN_R - 1, GRID_W, GRID_W)
    return jnp.concatenate([blocks[:, :-1], blocks[:, 1:]], axis=-1)


MOE_TM = 256
N_TILES = N_TOK // MOE_TM
CHUNK_SHIFT, BIG_SHIFT, PART_SHIFT, SUB_SHIFT = 3, 5, 7, 9
CHUNK = 1 << CHUNK_SHIFT
BIG = 1 << BIG_SHIFT
TILE_ROWS = 1280
SUB = 1 << SUB_SHIFT
PART = 1 << PART_SHIFT
ITEM_ROWS = 2048
MAX_ITEMS = 48
ITEM_TABLE = 64
N_SEG =N_TILES * N_EXPERTS

assert TILE_ROWS >= MOE_TM * TOP_K + N_EXPERTS * (CHUNK - 1) and TILE_ROWS % MOE_TM == 0 and ITEM_ROWS % SUB == 0
assert MAX_ITEMS >= N_EXPERTS + (N_ASSIGN + N_SEG * (CHUNK - 1) - 1) // (ITEM_ROWS - MOE_TM)


def _dispatch_kernel(*refs, n_x):
    mod_ref, rwt_ref, rbc_ref, upper_ref, lower_ref, xsb_ref, route_ref, tab_ref = refs[n_x:]
    h2 = _stream_tile(refs[:n_x], MOE_TM) * (1.0 + mod_ref[4:5, :]) + mod_ref[3:4, :]
    logits = lax.dot_general(rwt_ref[...], h2, (((1,), (1,)), ((), ())), precision=HIGHEST,
                             preferred_element_type=F32) + rbc_ref[...]
    sub = lax.broadcasted_iota(I32, (N_EXPERTS, MOE_TM), 0)
    vals, hots = [], []
    cur = logits
    for _ in range(TOP_K):
        m = jnp.max(cur, axis=0, keepdims=True)
        idx = jnp.min(jnp.where(cur == m, sub, N_EXPERTS), axis=0, keepdims=True)
        hot = sub == idx
        vals.append(m)
        hots.append(hot)
        cur = jnp.where(hot, -jnp.inf, cur)
    exps = [jnp.exp(v - vals[0]) for v in vals]
    den = (exps[0] + exps[1]) + (exps[2] + exps[3])
    mask = jnp.zeros((N_EXPERTS, MOE_TM), F32)
    for hot in hots:
        mask = mask + jnp.where(hot, 1.0, 0.0)
    before = jnp.dot(mask.astype(BF16), upper_ref[...], preferred_element_type=F32)
    n = jnp.sum(mask, axis=1, keepdims=True)
    p = (((n.astype(I32) + (CHUNK - 1)) >> CHUNK_SHIFT) << CHUNK_SHIFT).astype(F32)
    start = jnp.dot(lower_ref[...], jnp.broadcast_to(p, (N_EXPERTS, LANES)), precision=HIGHEST,
                    preferred_element_type=F32)[:, :1]
    base = start + before
    lps = [jnp.sum(jnp.where(hot, base, 0.0), axis=0, keepdims=True) for hot in hots]
    route_ref[...] = jnp.concatenate(lps + [e / den for e in exps], axis=0)
    lane = lax.broadcasted_iota(I32, (N_EXPERTS, LANES), 1)
    tab_ref[...] = jnp.where(lane == 0, n, jnp.where(lane == 1, p, jnp.where(lane == 2, start, 0.0)))
    jrow = lax.broadcasted_iota(I32, (TILE_ROWS, MOE_TM), 0)
    onehot = jnp.zeros((TILE_ROWS, MOE_TM), F32)
    for lp in lps:
        onehot = jnp.where(jrow == lp.astype(I32), 1.0, onehot)
    xs = jnp.dot(onehot.astype(BF16), h2.astype(BF16), preferred_element_type=F32)
    xsb_ref[...] = pltpu.pack_elementwise([xs[:, :PACKED_W], xs[:, PACKED_W:]], packed_dtype=BF16)


def _dispatch(x1, mod, l, rw, rb):
    upper = jnp.asarray(np.triu(np.ones((MOE_TM, MOE_TM)), 1), BF16)
    lower = jnp.asarray(np.tril(np.ones((N_EXPERTS, N_EXPERTS)), -1), F32)
    x_specs, x_args = _stream_specs(x1, MOE_TM)
    return pl.pallas_call(
        functools.partial(_dispatch_kernel, n_x=len(x_args)),
        grid=(N_TILES,),
        in_specs=x_specs + [_mod_spec(l, MOE_TM), _full((N_EXPERTS, D_MODEL)), _full((N_EXPERTS, 1)),
                            _full((MOE_TM, MOE_TM)), _full((N_EXPERTS, N_EXPERTS))],
        out_specs=[pl.BlockSpec((None, TILE_ROWS, PACKED_W), lambda i: (i, 0, 0)),
                   pl.BlockSpec((None, 2 * TOP_K, MOE_TM), lambda i: (i, 0, 0)),
                   pl.BlockSpec((None, N_EXPERTS, LANES), lambda i: (i, 0, 0))],
        out_shape=[jax.ShapeDtypeStruct((N_TILES, TILE_ROWS, PACKED_W), I32),
                   jax.ShapeDtypeStruct((N_TILES, 2 * TOP_K, MOE_TM), F32),
                   jax.ShapeDtypeStruct((N_TILES, N_EXPERTS, LANES), F32)],
        compiler_params=_cparams(("arbitrary",)),
        name="moe_dispatch",
    )(*x_args, mod, rw.T, rb.reshape(N_EXPERTS, 1), upper, lower)


def _plan_kernel(p_ref, iexp_ref, ib0_ref, ib1_ref, irows_ref, cnt_ref):
    cnt_ref[0] = 0

    def emit(e, b0, b1, rows):
        it = cnt_ref[0]
        iexp_ref[it] = e
        ib0_ref[it] = b0
        ib1_ref[it] = b1
        irows_ref[it] = rows
        cnt_ref[0] = it + 1

    def per_expert(e, last_e):
        def per_tile(b, carry):
            b0, rows = carry
            pb = p_ref[b * N_EXPERTS + e]
            full = rows + pb > ITEM_ROWS

            @pl.when(full)
            def _():
                emit(e, b0, b, rows)

            return jnp.where(full, b, b0), jnp.where(full, pb, rows + pb)

        b0, rows = lax.fori_loop(0, N_TILES, per_tile, (jnp.int32(0), jnp.int32(0)))

        @pl.when(rows > 0)
        def _():
            emit(e, b0, jnp.int32(N_TILES), rows)

        return jnp.where(rows > 0, e, last_e)

    last_e = lax.fori_loop(0, N_EXPERTS, per_expert, jnp.int32(0))

    def idle(it, c):
        iexp_ref[it] = last_e
        ib0_ref[it] = 0
        ib1_ref[it] = 0
        irows_ref[it] = 0
        return c

    lax.fori_loop(cnt_ref[0], ITEM_TABLE, idle, 0)


def _plan(ptab):
    smem = pl.BlockSpec(memory_space=pltpu.SMEM)
    return pl.pallas_call(
        _plan_kernel,
        in_specs=[smem],
        out_specs=[smem] * 4,
        out_shape=[jax.ShapeDtypeStruct((ITEM_TABLE,), I32)] * 4,
        scratch_shapes=[pltpu.SMEM((1,), I32)],
        name="moe_plan",
    )(ptab)


def _unpack_halves(p):
    lo = pltpu.unpack_elementwise(p, index=0, packed_dtype=BF16, unpacked_dtype=F32)
    hi = pltpu.unpack_elementwise(p, index=1, packed_dtype=BF16, unpacked_dtype=F32)
    return lo.astype(BF16), hi.astype(BF16)


def _unpack_rows(p):
    return jnp.concatenate(_unpack_halves(p), axis=1)


def _pack_rows(y):
    return pltpu.pack_elementwise([y[:, :PACKED_W], y[:, PACKED_W:]], packed_dtype=BF16)


def _expert_kernel(iexp_ref, ib0_ref, ib1_ref, irows_ref, ptab_ref, stab_ref, xsb_hbm, w1_hbm, w2_hbm,
                   b1_ref, b2_ref, ysb_hbm, xg_ref, g_ref, a_ref, yb_ref, w_ref, cnt_ref,
                   in_big, in_small, out_big, out_small, w_sem, *, layer):
    it = pl.program_id(0)
    rows = irows_ref[it]
    slot = it & 1

    def weight_copies(item, phase):
        ee = iexp_ref[item]
        copies = []
        for k in range(2):
            half = pl.ds(k * (D_MODEL // 2), D_MODEL // 2)
            if phase < 2:
                src = w1_hbm.at[layer, ee, half, pl.ds(phase * D_FF, D_FF)]
            else:
                src = w2_hbm.at[layer, ee, half, :]
            copies.append(pltpu.make_async_copy(src, w_ref.at[phase, half, :], w_sem.at[phase]))
        return copies

    def start_weights(item, phase):
        @pl.when(irows_ref[item] > 0)
        def _():
            for cp in weight_copies(item, phase):
                cp.start()

    def wait_weights(phase):
        for cp in weight_copies(it, phase):
            cp.wait()

    def aligned(v):
        return v if isinstance(v, int) else pl.multiple_of(v, CHUNK)

    def copy(to_vmem, buf, b, src, dst, size):
        hbm = (xsb_hbm if to_vmem else ysb_hbm).at[b, pl.ds(aligned(src), size), :]
        if to_vmem:
            sem = in_big if size == BIG else in_small
            return pltpu.make_async_copy(hbm, xg_ref.at[buf, pl.ds(aligned(dst), size), :], sem)
        sem = out_big if size == BIG else out_small
        return pltpu.make_async_copy(yb_ref.at[pl.ds(aligned(dst), size), :], hbm, sem)

    def start_segments(item, to_vmem, buf):
        ee = iexp_ref[item]

        def per_tile(b, carry):
            dst, n_big, n_small = carry
            pb = ptab_ref[b * N_EXPERTS + ee]
            src = stab_ref[b * N_EXPERTS + ee]
            nb = pb >> BIG_SHIFT
            ns = (pb - nb * BIG) >> CHUNK_SHIFT

            def big(c, z):
                copy(to_vmem, buf, b, src + c * BIG, dst + c * BIG, BIG).start()
                return z

            def small(c, z):
                off = nb * BIG + c * CHUNK
                copy(to_vmem, buf, b, src + off, dst + off, CHUNK).start()
                return z

            lax.fori_loop(0, nb, big, 0)
            lax.fori_loop(0, ns, small, 0)
            return dst + pb, n_big + nb, n_small + ns

        zero = jnp.int32(0)
        _, n_big, n_small = lax.fori_loop(ib0_ref[item], ib1_ref[item], per_tile, (zero, zero, zero))
        return n_big, n_small

    def wait_segments(to_vmem, n_big, n_small):
        def wait_big(c, z):
            copy(to_vmem, 0, 0, 0, 0, BIG).wait()
            return z

        def wait_small(c, z):
            copy(to_vmem, 0, 0, 0, 0, CHUNK).wait()
            return z

        lax.fori_loop(0, n_big, wait_big, 0)
        lax.fori_loop(0, n_small, wait_small, 0)

    @pl.when(it == 0)
    def _init():
        xg_ref[...] = jnp.zeros_like(xg_ref)
        start_weights(0, 0)
        start_weights(0, 1)
        cnt_ref[0], cnt_ref[1] = start_segments(0, True, 0)
        cnt_ref[2] = 0
        cnt_ref[3] = 0

    def for_sub_tiles(fn):
        n_full = rows >> SUB_SHIFT
        rem_parts = (rows - (n_full << SUB_SHIFT) + (PART - 1)) >> PART_SHIFT

        def body(i, c):
            fn(pl.ds(pl.multiple_of(i * SUB, SUB), SUB))
            return c

        lax.fori_loop(0, n_full, body, 0)
        r0 = pl.multiple_of(n_full * SUB, SUB)
        for k in range(1, SUB // PART + 1):
            @pl.when(rem_parts == k)
            def _(k=k):
                fn(pl.ds(r0, k * PART))

    def matmul(lo, hi, phase):
        half = D_MODEL // 2
        return (jnp.dot(lo, w_ref[phase, :half, :].astype(BF16), preferred_element_type=F32)
                + jnp.dot(hi, w_ref[phase, half:, :].astype(BF16), preferred_element_type=F32))

    @pl.when(rows > 0)
    def _item():
        start_weights(it, 2)
        wait_segments(True, cnt_ref[0], cnt_ref[1])
        wait_weights(0)

        def gate(r):
            lo, hi = _unpack_halves(xg_ref[slot, r, :])
            g_ref[r, :] = matmul(lo, hi, 0) + b1_ref[:, :D_FF]

        for_sub_tiles(gate)

        start_weights(it + 1, 0)
        cnt_ref[0], cnt_ref[1] = start_segments(it + 1, True, 1 - slot)
        wait_weights(1)

        def up_act(r):
            lo, hi = _unpack_halves(xg_ref[slot, r, :])
            up = matmul(lo, hi, 1) + b1_ref[:, D_FF:]
            gate = jnp.minimum(g_ref[r, :], SWIGLU_LIMIT)
            up = jnp.clip(up, -SWIGLU_LIMIT, SWIGLU_LIMIT)
            a_ref[r, :] = (gate * jax.nn.sigmoid(SWIGLU_ALPHA * gate) * (up + 1.0)).astype(BF16)

        for_sub_tiles(up_act)

        start_weights(it + 1, 1)
        wait_weights(2)
        wait_segments(False, cnt_ref[2], cnt_ref[3])

        def down(r):
            y = matmul(a_ref[r, :D_FF // 2], a_ref[r, D_FF // 2:], 2) + b2_ref[...]
            yb_ref[r, :] = _pack_rows(y)

        for_sub_tiles(down)
        cnt_ref[2], cnt_ref[3] = start_segments(it, False, 0)

    @pl.when(it == MAX_ITEMS - 1)
    def _drain():
        wait_segments(False, cnt_ref[2], cnt_ref[3])


def _experts(iexp, ib0, ib1, irows, ptab, stab, xsb, l, w1, b1, w2, b2):
    n_prefetch = 6
    any_spec = pl.BlockSpec(memory_space=pl.ANY)
    return pl.pallas_call(
        functools.partial(_expert_kernel, layer=l),
        grid_spec=pltpu.PrefetchScalarGridSpec(
            num_scalar_prefetch=n_prefetch,
            grid=(MAX_ITEMS,),
            in_specs=[any_spec, any_spec, any_spec,
                      pl.BlockSpec((None, None, 1, 2 * D_FF), lambda it, ie, *_: (l, ie[it], 0, 0)),
                      pl.BlockSpec((None, None, 1, D_MODEL), lambda it, ie, *_: (l, ie[it], 0, 0))],
            out_specs=pl.BlockSpec(memory_space=pl.ANY),
            scratch_shapes=[pltpu.VMEM((2, ITEM_ROWS, PACKED_W), I32),
                            pltpu.VMEM((ITEM_ROWS, D_FF), F32),
                            pltpu.VMEM((ITEM_ROWS, D_FF), BF16),
                            pltpu.VMEM((ITEM_ROWS, PACKED_W), I32),
                            pltpu.VMEM((3, D_MODEL, D_FF), F32),
                            pltpu.SMEM((4,), I32),
                            pltpu.SemaphoreType.DMA(()),
                            pltpu.SemaphoreType.DMA(()),
                            pltpu.SemaphoreType.DMA(()),
                            pltpu.SemaphoreType.DMA(()),
                            pltpu.SemaphoreType.DMA((3,))],
        ),
        out_shape=jax.ShapeDtypeStruct((N_TILES, TILE_ROWS, PACKED_W), I32),
        input_output_aliases={n_prefetch: 0},
        compiler_params=_cparams(("arbitrary",)),
        name="moe_experts",
    )(iexp, ib0, ib1, irows, ptab, stab, xsb, w1, w2,
      b1.reshape(DEPTH, N_EXPERTS, 1, 2 * D_FF), b2.reshape(DEPTH, N_EXPERTS, 1, D_MODEL))


def _combine_kernel(*refs, n_x, n_out):
    ysb_ref, route_ref, eye_ref, mod_ref, g_ref, b_ref = refs[n_x:len(refs) - n_out]
    o_refs = refs[len(refs) - n_out:]
    rt = lax.dot_general(eye_ref[...], route_ref[...], (((1,), (1,)), ((), ())), precision=HIGHEST,
                         preferred_element_type=F32)
    lane = lax.broadcasted_iota(I32, (MOE_TM, TILE_ROWS), 1)
    c = jnp.zeros((MOE_TM, TILE_ROWS), F32)
    for k in range(TOP_K):
        c = jnp.where(lane == rt[:, k:k + 1].astype(I32), rt[:, TOP_K + k:TOP_K + k + 1], c)
    f = jnp.dot(c.astype(BF16), _unpack_rows(ysb_ref[...]), preferred_element_type=F32)
    x = _stream_tile(refs[:n_x], MOE_TM)
    out = _layer_norm(DEEPNORM_ALPHA * x + mod_ref[5:6, :] * f, g_ref[...], b_ref[...])
    if n_out == 1:
        o_refs[0][...] = out
    else:
        is_ctx = pl.program_id(0) < N_PROMPT // MOE_TM

        @pl.when(is_ctx)
        def _():
            o_refs[0][...] = out

        @pl.when(jnp.logical_not(is_ctx))
        def _():
            o_refs[1][...] = out


def _combine(x1, ysb, route, mod, l, g, b, split_out):
    x_specs, x_args = _stream_specs(x1, MOE_TM)
    if split_out:
        out_specs, _ = _stream_specs((None, None), MOE_TM)
        out_shape = [jax.ShapeDtypeStruct((N_PROMPT, D_MODEL), F32), jax.ShapeDtypeStruct((N_SAMPLE, D_MODEL), F32)]
    else:
        out_specs, _ = _stream_specs(None, MOE_TM)
        out_shape = [jax.ShapeDtypeStruct((N_TOK, D_MODEL), F32)]
    return pl.pallas_call(
        functools.partial(_combine_kernel, n_x=len(x_args), n_out=len(out_shape)),
        grid=(N_TILES,),
        in_specs=x_specs + [pl.BlockSpec((None, TILE_ROWS, PACKED_W), lambda i: (i, 0, 0)),
                            pl.BlockSpec((None, 2 * TOP_K, MOE_TM), lambda i: (i, 0, 0)),
                            _full((MOE_TM, MOE_TM)), _mod_spec(l, MOE_TM), _full((1, D_MODEL)),
                            _full((1, D_MODEL))],
        out_specs=out_specs,
        out_shape=out_shape,
        compiler_params=_cparams(("arbitrary",)),
        name="moe_combine",
    )(*x_args, ysb, route, jnp.eye(MOE_TM, dtype=F32), mod, g.reshape(1, D_MODEL), b.reshape(1, D_MODEL))


def _moe_layer(x1, mod, l, rw, rb, w1, b1, w2, b2, ln_g, ln_b, split_out=False):
    xsb, route, tab = _dispatch(x1, mod, l, rw, rb)
    ptab = tab[:, :, 1].astype(I32).reshape(N_SEG)
    stab = tab[:, :, 2].astype(I32).reshape(N_SEG)
    iexp, ib0, ib1, irows = _plan(ptab)
    ysb = _experts(iexp, ib0, ib1, irows, ptab, stab, xsb, l, w1, b1, w2, b2)
    return _combine(x1, ysb, route, mod, l, ln_g, ln_b, split_out)


def kernel(x_prompt, x_sample, cache_k_attn, cache_v_attn, cache_k_na, cache_v_na, c, c_ctx, w_mod, b_mod, ln1_g, ln1_b, ln2_g, ln2_b, conv_w_in, conv_w, conv_b, conv_w_out, attn_w_qkv, attn_q_norm, attn_k_norm, attn_w_o, na_w_qkv, na_rpb, na_w_o, router_w, router_b, moe_w1, moe_b1, moe_w2, moe_b2):
    x = (x_prompt.reshape(N_PROMPT, D_MODEL), x_sample.reshape(N_SAMPLE, D_MODEL))
    cond8 = jnp.concatenate([c_ctx[None, :], c, jnp.zeros((8 - 1 - DEC_BATCH, D_MODEL), F32)], axis=0)
    mod = _adaln_all(cond8, w_mod, b_mod)
    new_kv = {}
    for l in range(DEPTH):
        kind, j = l % 3, l // 3
        if kind == 0:
            x1 = _conv_layer(x, mod, l, conv_w_in[j], conv_w[j], conv_b[j], conv_w_out[j], ln1_g[l], ln1_b[l])
        elif kind == 1:
            gq = jnp.tile(attn_q_norm[j], N_HEADS).reshape(1, N_HEADS * HEAD_DIM)
            gk = jnp.tile(attn_k_norm[j], N_KV_HEADS).reshape(1, N_KV_HEADS * HEAD_DIM)
            norm_args = (gq, gk, _block_diag_mean())
            wk = N_KV_HEADS * HEAD_DIM
            xp1, nk, nv = _attn_call(x, mod, l, False, attn_w_qkv[j], attn_w_o[j], ln1_g[l], ln1_b[l],
                                     n_seq=BATCH, seq_len=SEQ, tq=SEQ, n_kv=N_KV_HEADS, norm_args=norm_args,
                                     emit_kv=True, name="gqa_prompt")
            new_kv["k_attn"] = nk.reshape(BATCH, 1, SEQ, N_KV_HEADS, HEAD_DIM)
            new_kv["v_attn"] = nv.reshape(BATCH, 1, SEQ, N_KV_HEADS, HEAD_DIM)
            (xs1,) = _attn_call(x, mod, l, True, attn_w_qkv[j], attn_w_o[j], ln1_g[l], ln1_b[l],
                                n_seq=DEC_BATCH, seq_len=DEC_SEQ, tq=256, n_kv=N_KV_HEADS, norm_args=norm_args,
                                rope_args=_rope_tables(),
                                ctx_args=(cache_k_attn[:, j].reshape(DEC_BATCH, PAST_LEN, wk),
                                          cache_v_attn[:, j].reshape(DEC_BATCH, PAST_LEN, wk)),
                                name="gqa_sample")
            x1 = (xp1, xs1)
        else:
            wk = N_HEADS * HEAD_DIM
            xp1, nk, nv = _attn_call(x, mod, l, False, na_w_qkv[j], na_w_o[j], ln1_g[l], ln1_b[l],
                                     n_seq=BATCH, seq_len=SEQ, tq=SEQ, n_kv=N_HEADS, emit_kv=True,
                                     name="mha_prompt")
            new_kv["k_na"] = nk.reshape(BATCH, 1, SEQ, N_HEADS, HEAD_DIM)
            new_kv["v_na"] = nv.reshape(BATCH, 1, SEQ, N_HEADS, HEAD_DIM)
            (xs1,) = _attn_call(x, mod, l, True, na_w_qkv[j], na_w_o[j], ln1_g[l], ln1_b[l],
                                n_seq=DEC_BATCH, seq_len=DEC_SEQ, tq=GRID_W, n_kv=N_HEADS,
                                ctx_args=(cache_k_na[:, j].reshape(DEC_BATCH, PAST_LEN, wk),
                                          cache_v_na[:, j].reshape(DEC_BATCH, PAST_LEN, wk)),
                                bias=_na_bias_table(na_rpb[j]), name="na_sample")
            x1 = (xp1, xs1)
        out = _moe_layer(x1, mod, l, router_w[l], router_b[l], moe_w1, moe_b1, moe_w2, moe_b2,
                         ln2_g[l], ln2_b[l], split_out=l == DEPTH - 1)
        x = out[0]
    y_prompt = out[0].reshape(BATCH, SEQ, D_MODEL)
    y_sample = out[1].reshape(DEC_BATCH, DEC_SEQ, D_MODEL)
    return (y_prompt, y_sample, new_kv["k_attn"], new_kv["v_attn"], new_kv["k_na"], new_kv["v_na"])
```

```python
import functools

import numpy as np
import jax
import jax.numpy as jnp
from jax import lax
from jax.experimental import pallas as pl
from jax.experimental.pallas import tpu as pltpu

D_MODEL = 1024
BATCH = 16
SEQ = 256
DEPTH = 4
DEC_BATCH = 2
DEC_SEQ = 1024
PAST_LEN = 256
GRID_W = 64
HEAD_DIM = 64
N_HEADS = 16
N_KV_HEADS = 4
ROPE_THETA = 10000.0
ROPE_PAIRS = HEAD_DIM // 4
WIN_R = 8
WIN_C = 16
N_EXPERTS = 32
TOP_K = 4
D_FF = D_MODEL
SWIGLU_LIMIT = 7.0
SWIGLU_ALPHA = 1.702
DEEPNORM_ALPHA = (2 * DEPTH) ** 0.25
LN_EPS = 1e-5
RMS_EPS = 1e-6

F32 = jnp.float32
BF16 = jnp.bfloat16
I32 = jnp.int32
HIGHEST = lax.Precision.HIGHEST

N_PROMPT = BATCH * SEQ
N_SAMPLE = DEC_BATCH * DEC_SEQ
N_TOK = N_PROMPT + N_SAMPLE
N_ASSIGN = N_TOK * TOP_K

LANES = 128
SUBLANES = 8
PACKED_W = D_MODEL // 2

VMEM_LIMIT = 58 * 1024 * 1024


def _cparams(sem):
    return pltpu.CompilerParams(dimension_semantics=sem, vmem_limit_bytes=VMEM_LIMIT)


def _layer_norm(x, g, b):
    mu = jnp.mean(x, -1, keepdims=True)
    xc = x - mu
    var = jnp.mean(xc * xc, -1, keepdims=True)
    return xc * lax.rsqrt(var + LN_EPS) * g + b


def _mod_row(i, tile):
    n_prompt_tiles = N_PROMPT // tile
    return jnp.where(i < n_prompt_tiles, 0, 1 + (i - n_prompt_tiles) // (DEC_SEQ // tile))


def _mod_spec(l, tile):
    return pl.BlockSpec((None, None, 6, D_MODEL), lambda i, *_: (l, _mod_row(i, tile), 0, 0))


def _full(shape):
    nd = len(shape)
    return pl.BlockSpec(shape, lambda *_: (0,) * nd)


def _stream_specs(x, tile):
    if not isinstance(x, tuple):
        return [pl.BlockSpec((tile, D_MODEL), lambda i, *_: (i, 0))], [x]
    n_a = N_PROMPT // tile
    return ([pl.BlockSpec((tile, D_MODEL), lambda i, *_: (jnp.minimum(i, n_a - 1), 0)),
             pl.BlockSpec((tile, D_MODEL), lambda i, *_: (jnp.maximum(i - n_a, 0), 0))], list(x))


def _stream_tile(x_refs, tile):
    if len(x_refs) == 1:
        return x_refs[0][...]
    return jnp.where(pl.program_id(0) < N_PROMPT // tile, x_refs[0][...], x_refs[1][...])


ADALN_TN = 1536


def _adaln_kernel(cond_ref, wa_ref, wb_ref, b_ref, o_ref):
    c = cond_ref[...]
    s = c * jax.nn.sigmoid(c)
    half = D_MODEL // 2
    o_ref[...] = (jnp.dot(s[:, :half], wa_ref[...], precision=HIGHEST, preferred_element_type=F32)
                  + jnp.dot(s[:, half:], wb_ref[...], precision=HIGHEST, preferred_element_type=F32)
                  + b_ref[...])


def _adaln_all(cond8, w_mod, b_mod):
    n = 6 * D_MODEL
    out = pl.pallas_call(
        _adaln_kernel,
        grid=(DEPTH, n // ADALN_TN),
        in_specs=[
            pl.BlockSpec((8, D_MODEL), lambda l, j: (0, 0)),
            pl.BlockSpec((None, D_MODEL // 2, ADALN_TN), lambda l, j: (l, 0, j)),
            pl.BlockSpec((None, D_MODEL // 2, ADALN_TN), lambda l, j: (l, 1, j)),
            pl.BlockSpec((None, 1, ADALN_TN), lambda l, j: (l, 0, j)),
        ],
        out_specs=pl.BlockSpec((None, 8, ADALN_TN), lambda l, j: (l, 0, j)),
        out_shape=jax.ShapeDtypeStruct((DEPTH, 8, n), F32),
        compiler_params=_cparams(("arbitrary", "arbitrary")),
        name="adaln",
    )(cond8, w_mod, w_mod, b_mod.reshape(DEPTH, 1, n))
    return out.reshape(DEPTH, 8, 6, D_MODEL)


CONV_TM = 1024


def _conv_kernel(*refs, n_x):
    mod_ref, win_ref, cw_ref, cb_ref, wout_ref, g_ref, b_ref, o_ref = refs[n_x:]
    i = pl.program_id(0)
    x = _stream_tile(refs[:n_x], CONV_TM)
    h = (x * (1.0 + mod_ref[1:2, :]) + mod_ref[0:1, :]).astype(BF16)
    gc = jnp.dot(h, win_ref[:, D_MODEL:2 * D_MODEL], preferred_element_type=F32)
    xv = jnp.dot(h, win_ref[:, 2 * D_MODEL:], preferred_element_type=F32)
    u = gc * xv
    seq_len = jnp.where(i < N_PROMPT // CONV_TM, SEQ, DEC_SEQ)
    t = lax.broadcasted_iota(I32, (CONV_TM, 1), 0) & (seq_len - 1)
    u_prev = jnp.where(t == 0, 0.0, pltpu.roll(u, 1, axis=0))
    u_next = jnp.where(t == seq_len - 1, 0.0, pltpu.roll(u, CONV_TM - 1, axis=0))
    y = u_prev * cw_ref[0:1, :] + u * cw_ref[1:2, :] + u_next * cw_ref[2:3, :] + cb_ref[...]
    gb = jnp.dot(h, win_ref[:, :D_MODEL], preferred_element_type=F32)
    v = (gb * y).astype(BF16)
    o = jnp.dot(v, wout_ref[...], preferred_element_type=F32)
    o_ref[...] = _layer_norm(DEEPNORM_ALPHA * x + mod_ref[2:3, :] * o, g_ref[...], b_ref[...])


def _conv_layer(x, mod, l, w_in, cw, cb, w_out, ln_g, ln_b):
    x_specs, x_args = _stream_specs(x, CONV_TM)
    return pl.pallas_call(
        functools.partial(_conv_kernel, n_x=len(x_args)),
        grid=(N_TOK // CONV_TM,),
        in_specs=x_specs + [_mod_spec(l, CONV_TM), _full((D_MODEL, 3 * D_MODEL)), _full((3, D_MODEL)),
                            _full((1, D_MODEL)), _full((D_MODEL, D_MODEL)), _full((1, D_MODEL)),
                            _full((1, D_MODEL))],
        out_specs=pl.BlockSpec((CONV_TM, D_MODEL), lambda i: (i, 0)),
        out_shape=jax.ShapeDtypeStruct((N_TOK, D_MODEL), F32),
        compiler_params=_cparams(("arbitrary",)),
        name="conv_mixer",
    )(*x_args, mod, w_in.astype(BF16), cw, cb.reshape(1, D_MODEL), w_out.astype(BF16),
      ln_g.reshape(1, D_MODEL), ln_b.reshape(1, D_MODEL))


ATTN_CHUNK = 256


def _attn_kernel(*refs, seq_len, tq, n_kv, norm, rope, n_ctx, na, emit_kv):
    refs = list(refs)
    x_ref, mod_ref, wqkv_ref = refs[:3]
    pos = 3
    if norm:
        gq_ref, gk_ref, bd_ref = refs[pos:pos + 3]
        pos += 3
    if rope:
        cos_ref, s1_ref, s2_ref = refs[pos:pos + 3]
        pos += 3
    if n_ctx:
        ck_ref, cv_ref = refs[pos:pos + 2]
        pos += 2
    if na:
        bias_ref = refs[pos]
        pos += 1
    wo_ref, lng_ref, lnb_ref = refs[pos:pos + 3]
    pos += 3
    o_ref = refs[pos]
    pos += 1
    if emit_kv:
        nk_ref, nv_ref = refs[pos:pos + 2]
        pos += 2
    q_scr, k_scr, v_scr, o_scr = refs[pos:pos + 4]

    qt = pl.program_id(1)
    n_qt = seq_len // tq
    wq = N_HEADS * HEAD_DIM
    wk = n_kv * HEAD_DIM
    rep = N_HEADS // n_kv

    def rms(v, g_ref, width):
        ms = jnp.dot((v * v).astype(BF16), bd_ref[:width, :width], preferred_element_type=F32)
        return v * lax.rsqrt(ms + RMS_EPS) * g_ref[...]

    def rot(v, rows, width):
        def tab(ref):
            t = ref[rows, :]
            return jnp.concatenate([t] * (width // LANES), axis=1)

        return (v * tab(cos_ref) + pltpu.roll(v, width - ROPE_PAIRS, axis=1) * tab(s1_ref)
                + pltpu.roll(v, ROPE_PAIRS, axis=1) * tab(s2_ref))

    chunk = min(seq_len, ATTN_CHUNK)

    @pl.when(qt == 0)
    def _project():
        def body(ci, carry):
            r0 = pl.multiple_of(ci * chunk, chunk)
            rows = pl.ds(r0, chunk)
            h = (x_ref[rows, :] * (1.0 + mod_ref[1:2, :]) + mod_ref[0:1, :]).astype(BF16)
            q = jnp.dot(h, wqkv_ref[:, :wq], preferred_element_type=F32)
            k = jnp.dot(h, wqkv_ref[:, wq:wq + wk], preferred_element_type=F32)
            v = jnp.dot(h, wqkv_ref[:, wq + wk:], preferred_element_type=F32)
            if norm:
                q = rms(q, gq_ref, wq)
                k = rms(k, gk_ref, wk)
            if emit_kv:
                nk_ref[rows, :] = k
                nv_ref[rows, :] = v
            if rope:
                q = rot(q, rows, wq)
                k = rot(k, rows, wk)
            q_scr[rows, :] = (q * (HEAD_DIM ** -0.5)).astype(BF16)
            k_scr[pl.ds(n_ctx + r0, chunk), :] = k.astype(BF16)
            v_scr[pl.ds(n_ctx + r0, chunk), :] = v.astype(BF16)
            return carry

        lax.fori_loop(0, seq_len // chunk, body, 0)
        if n_ctx:
            k_scr[:n_ctx, :] = ck_ref[...].astype(BF16)
            v_scr[:n_ctx, :] = cv_ref[...].astype(BF16)

    q0 = pl.multiple_of(qt * tq, tq)
    if na:
        row_start = jnp.clip(qt - WIN_R // 2, 0, DEC_SEQ // GRID_W - WIN_R)
        d0 = row_start - qt + (WIN_R - 1)
        k0 = pl.multiple_of(n_ctx + row_start * GRID_W, GRID_W)
    for hd in range(N_HEADS):
        g = hd // rep
        hs = slice(hd * HEAD_DIM, (hd + 1) * HEAD_DIM)
        gs = slice(g * HEAD_DIM, (g + 1) * HEAD_DIM)
        qh = q_scr[pl.ds(q0, tq), hs]
        dn = (((1,), (1,)), ((), ()))
        if na:
            kc, vc = k_scr[:n_ctx, gs], v_scr[:n_ctx, gs]
            kl, vl = k_scr[pl.ds(k0, WIN_R * GRID_W), gs], v_scr[pl.ds(k0, WIN_R * GRID_W), gs]
            bias = jnp.concatenate([bias_ref[hd, pl.ds(d0 + 2 * j, 1)][0] for j in range(WIN_R // 2)], axis=1)
            s = jnp.concatenate([lax.dot_general(qh, kc, dn, preferred_element_type=F32),
                                 lax.dot_general(qh, kl, dn, preferred_element_type=F32) + bias], axis=1)
        else:
            s = lax.dot_general(qh, k_scr[:, gs], dn, preferred_element_type=F32)
        e = jnp.exp(s - jnp.max(s, axis=1, keepdims=True))
        den = jnp.sum(e, axis=1, keepdims=True)
        eb = e.astype(BF16)
        if na:
            oh = (jnp.dot(eb[:, :n_ctx], vc, preferred_element_type=F32)
                  + jnp.dot(eb[:, n_ctx:], vl, preferred_element_type=F32))
        else:
            oh = jnp.dot(eb, v_scr[:, gs], preferred_element_type=F32)
        o_scr[pl.ds(q0, tq), hs] = (oh / den).astype(BF16)

    @pl.when(qt == n_qt - 1)
    def _finish():
        def body(ci, carry):
            rows = pl.ds(pl.multiple_of(ci * chunk, chunk), chunk)
            o = jnp.dot(o_scr[rows, :], wo_ref[...], preferred_element_type=F32)
            o_ref[rows, :] = _layer_norm(DEEPNORM_ALPHA * x_ref[rows, :] + mod_ref[2:3, :] * o,
                                         lng_ref[...], lnb_ref[...])
            return carry

        lax.fori_loop(0, seq_len // chunk, body, 0)


def _attn_call(x, mod, l, latent, w_qkv, w_o, ln_g, ln_b, *, n_seq, seq_len, tq, n_kv, norm_args=None,
               rope_args=None, ctx_args=None, bias=None, emit_kv=False, name="attn"):
    wq = N_HEADS * HEAD_DIM
    wk = n_kv * HEAD_DIM
    n_ctx = PAST_LEN if ctx_args is not None else 0
    n_qt = seq_len // tq
    seq_off = N_PROMPT // seq_len if latent else 0
    seq_spec = pl.BlockSpec((seq_len, D_MODEL), lambda s, t: (s, 0))
    mod_spec = pl.BlockSpec((None, None, 6, D_MODEL), lambda s, t: (l, (1 + s) if latent else 0, 0, 0))
    in_specs = [pl.BlockSpec((seq_len, D_MODEL), lambda s, t: (s + seq_off, 0)), mod_spec,
                pl.BlockSpec((D_MODEL, wq + 2 * wk), lambda s, t: (0, 0))]
    args = [x, mod, w_qkv.astype(BF16)]
    if norm_args is not None:
        gq, gk, bd = norm_args
        in_specs += [pl.BlockSpec((1, wq), lambda s, t: (0, 0)), pl.BlockSpec((1, wk), lambda s, t: (0, 0)),
                     pl.BlockSpec((wq, wq), lambda s, t: (0, 0))]
        args += [gq, gk, bd]
    if rope_args is not None:
        in_specs += [pl.BlockSpec((seq_len, LANES), lambda s, t: (0, 0))] * 3
        args += list(rope_args)
    if ctx_args is not None:
        in_specs += [pl.BlockSpec((None, n_ctx, wk), lambda s, t: (s, 0, 0))] * 2
        args += list(ctx_args)
    if bias is not None:
        in_specs += [pl.BlockSpec(bias.shape, lambda s, t: (0, 0, 0, 0))]
        args += [bias]
    in_specs += [pl.BlockSpec((D_MODEL, D_MODEL), lambda s, t: (0, 0)),
                 pl.BlockSpec((1, D_MODEL), lambda s, t: (0, 0)), pl.BlockSpec((1, D_MODEL), lambda s, t: (0, 0))]
    args += [w_o.astype(BF16), ln_g.reshape(1, D_MODEL), ln_b.reshape(1, D_MODEL)]
    out_specs = [seq_spec]
    out_shape = [jax.ShapeDtypeStruct((n_seq * seq_len, D_MODEL), F32)]
    if emit_kv:
        out_specs += [pl.BlockSpec((seq_len, wk), lambda s, t: (s, 0))] * 2
        out_shape += [jax.ShapeDtypeStruct((n_seq * seq_len, wk), F32)] * 2
    kern = functools.partial(_attn_kernel, seq_len=seq_len, tq=tq, n_kv=n_kv, norm=norm_args is not None,
                             rope=rope_args is not None, n_ctx=n_ctx, na=bias is not None, emit_kv=emit_kv)
    return pl.pallas_call(
        kern,
        grid=(n_seq, n_qt),
        in_specs=in_specs,
        out_specs=out_specs,
        out_shape=out_shape,
        scratch_shapes=[pltpu.VMEM((seq_len, wq), BF16), pltpu.VMEM((n_ctx + seq_len, wk), BF16),
                        pltpu.VMEM((n_ctx + seq_len, wk), BF16), pltpu.VMEM((seq_len, wq), BF16)],
        compiler_params=_cparams(("arbitrary", "arbitrary")),
        name=name,
    )(*args)


def _rope_tables():
    t = np.arange(DEC_SEQ)
    pos = np.stack([t // GRID_W, t % GRID_W], axis=1).astype(np.float64)
    inv = (ROPE_THETA ** (-np.arange(ROPE_PAIRS, dtype=np.float32) / ROPE_PAIRS)).astype(np.float64)
    ang = pos[:, :, None] * inv[None, None, :]
    cos, sin = np.cos(ang), np.sin(ang)
    zero = np.zeros_like(sin)
    cos_t = np.concatenate([cos, cos], axis=2).reshape(DEC_SEQ, HEAD_DIM)
    s1_t = np.concatenate([-sin, zero], axis=2).reshape(DEC_SEQ, HEAD_DIM)
    s2_t = np.concatenate([zero, sin], axis=2).reshape(DEC_SEQ, HEAD_DIM)
    return tuple(jnp.asarray(np.tile(a, (1, LANES // HEAD_DIM)), F32) for a in (cos_t, s1_t, s2_t))


def _block_diag_mean():
    a = np.kron(np.eye(N_HEADS), np.full((HEAD_DIM, HEAD_DIM), 1.0 / HEAD_DIM))
    return jnp.asarray(a, BF16)


def _bias_kernel(rpb_ref, sel_ref, inside_ref, o_ref):
    v = jnp.dot(rpb_ref[...], sel_ref[...], precision=HIGHEST, preferred_element_type=F32)
    o_ref[...] = jnp.where(inside_ref[...] > 0.0, v, -1e30)


def _na_bias_table(rpb):
    n_rows, n_off = N_HEADS * (2 * WIN_R - 1), 2 * WIN_C - 1
    cols = np.arange(GRID_W)
    col_start = np.clip(cols - WIN_C // 2, 0, GRID_W - WIN_C)
    kc = np.arange(GRID_W)
    inside = (kc[None, :] >= col_start[:, None]) & (kc[None, :] < col_start[:, None] + WIN_C)
    off = np.clip(kc[None, :] - cols[:, None] + (WIN_C - 1), 0, n_off - 1)
    sel = (np.arange(LANES)[:, None] == off.reshape(1, -1)).astype(np.float32)
    rpb2 = jnp.pad(rpb.reshape(n_rows, n_off), ((0, 0), (0, LANES - n_off)))
    blocks = pl.pallas_call(
        _bias_kernel,
        out_shape=jax.ShapeDtypeStruct((n_rows, GRID_W * GRID_W), F32),
        name="na_bias",
    )(rpb2, jnp.asarray(sel), jnp.asarray(inside.reshape(1, -1).astype(np.float32)))
    blocks = blocks.reshape(N_HEADS, 2 * WIN_R - 1, GRID_W, GRID_W)
    return jnp.concatenate([blocks[:, :-1], blocks[:, 1:]], axis=-1)


MOE_TM = 256
N_TILES = N_TOK // MOE_TM
CHUNK_SHIFT, BIG_SHIFT, PART_SHIFT, SUB_SHIFT = 3, 5, 7, 9
CHUNK = 1 << CHUNK_SHIFT
BIG = 1 << BIG_SHIFT
TILE_ROWS = 1280
SUB = 1 << SUB_SHIFT
PART = 1 << PART_SHIFT
ITEM_ROWS = 2048
MAX_ITEMS = 48
ITEM_TABLE = 64
MAX_ROWS = N_ASSIGN + N_TILES * N_EXPERTS * (CHUNK - 1)
BIG_LIST = 1024
SMALL_LIST = 2560
N_SEG =N_TILES * N_EXPERTS

assert TILE_ROWS >= MOE_TM * TOP_K + N_EXPERTS * (CHUNK - 1) and TILE_ROWS % MOE_TM == 0 and ITEM_ROWS % SUB == 0
assert MAX_ITEMS >= N_EXPERTS + (N_ASSIGN + N_SEG * (CHUNK - 1) - 1) // (ITEM_ROWS - MOE_TM)
assert BIG_LIST >= MAX_ROWS // BIG and SMALL_LIST >= N_SEG * (BIG // CHUNK - 1) and ITEM_ROWS <= 1 << 11


def _dispatch_kernel(*refs, n_x):
    mod_ref, rwt_ref, rbc_ref, upper_ref, lower_ref, xsb_ref, route_ref, tab_ref = refs[n_x:]
    h2 = _stream_tile(refs[:n_x], MOE_TM) * (1.0 + mod_ref[4:5, :]) + mod_ref[3:4, :]
    def split(v):
        hi = v.astype(BF16)
        return hi, (v - hi.astype(F32)).astype(BF16)

    def nt_dot(a, b):
        return lax.dot_general(a, b, (((1,), (1,)), ((), ())), preferred_element_type=F32)

    (w_hi, w_lo), (h_hi, h_lo) = split(rwt_ref[...]), split(h2)
    logits = (nt_dot(w_hi, h_hi) + nt_dot(w_hi, h_lo) + nt_dot(w_lo, h_hi)) + rbc_ref[...]
    sub = lax.broadcasted_iota(I32, (N_EXPERTS, MOE_TM), 0)
    vals, hots = [], []
    cur = logits
    for _ in range(TOP_K):
        m = jnp.max(cur, axis=0, keepdims=True)
        idx = jnp.min(jnp.where(cur == m, sub, N_EXPERTS), axis=0, keepdims=True)
        hot = sub == idx
        vals.append(m)
        hots.append(hot)
        cur = jnp.where(hot, -jnp.inf, cur)
    exps = [jnp.exp(v - vals[0]) for v in vals]
    den = (exps[0] + exps[1]) + (exps[2] + exps[3])
    mask = jnp.zeros((N_EXPERTS, MOE_TM), F32)
    for hot in hots:
        mask = mask + jnp.where(hot, 1.0, 0.0)
    before = jnp.dot(mask.astype(BF16), upper_ref[...], preferred_element_type=F32)
    n = jnp.sum(mask, axis=1, keepdims=True)
    p = (((n.astype(I32) + (CHUNK - 1)) >> CHUNK_SHIFT) << CHUNK_SHIFT).astype(F32)
    start = jnp.dot(lower_ref[...], jnp.broadcast_to(p, (N_EXPERTS, LANES)), precision=HIGHEST,
                    preferred_element_type=F32)[:, :1]
    base = start + before
    lps = [jnp.sum(jnp.where(hot, base, 0.0), axis=0, keepdims=True) for hot in hots]
    route_ref[...] = jnp.concatenate(lps + [e / den for e in exps], axis=0)
    lane = lax.broadcasted_iota(I32, (N_EXPERTS, LANES), 1)
    tab_ref[...] = jnp.where(lane == 0, n, jnp.where(lane == 1, p, jnp.where(lane == 2, start, 0.0)))
    jrow = lax.broadcasted_iota(I32, (TILE_ROWS, MOE_TM), 0)
    onehot = jnp.zeros((TILE_ROWS, MOE_TM), F32)
    for lp in lps:
        onehot = jnp.where(jrow == lp.astype(I32), 1.0, onehot)
    xs = jnp.dot(onehot.astype(BF16), h_hi, preferred_element_type=F32)
    xsb_ref[...] = pltpu.pack_elementwise([xs[:, :PACKED_W], xs[:, PACKED_W:]], packed_dtype=BF16)


def _dispatch(x1, mod, l, rw, rb):
    upper = jnp.asarray(np.triu(np.ones((MOE_TM, MOE_TM)), 1), BF16)
    lower = jnp.asarray(np.tril(np.ones((N_EXPERTS, N_EXPERTS)), -1), F32)
    x_specs, x_args = _stream_specs(x1, MOE_TM)
    return pl.pallas_call(
        functools.partial(_dispatch_kernel, n_x=len(x_args)),
        grid=(N_TILES,),
        in_specs=x_specs + [_mod_spec(l, MOE_TM), _full((N_EXPERTS, D_MODEL)), _full((N_EXPERTS, 1)),
                            _full((MOE_TM, MOE_TM)), _full((N_EXPERTS, N_EXPERTS))],
        out_specs=[pl.BlockSpec((None, TILE_ROWS, PACKED_W), lambda i: (i, 0, 0)),
                   pl.BlockSpec((None, 2 * TOP_K, MOE_TM), lambda i: (i, 0, 0)),
                   pl.BlockSpec((None, N_EXPERTS, LANES), lambda i: (i, 0, 0))],
        out_shape=[jax.ShapeDtypeStruct((N_TILES, TILE_ROWS, PACKED_W), I32),
                   jax.ShapeDtypeStruct((N_TILES, 2 * TOP_K, MOE_TM), F32),
                   jax.ShapeDtypeStruct((N_TILES, N_EXPERTS, LANES), F32)],
        compiler_params=_cparams(("arbitrary",)),
        name="moe_dispatch",
    )(*x_args, mod, rw.T, rb.reshape(N_EXPERTS, 1), upper, lower)


def _plan_kernel(p_ref, s_ref, iexp_ref, ib0_ref, ib1_ref, irows_ref, ibig_ref, ismall_ref, big_ref, small_ref,
                 cnt_ref):
    cnt_ref[0] = 0

    def emit(e, b0, b1, rows):
        it = cnt_ref[0]
        iexp_ref[it] = e
        ib0_ref[it] = b0
        ib1_ref[it] = b1
        irows_ref[it] = rows
        cnt_ref[0] = it + 1

    def per_expert(e, last_e):
        def per_tile(b, carry):
            b0, rows = carry
            pb = p_ref[b * N_EXPERTS + e]
            full = rows + pb > ITEM_ROWS

            @pl.when(full)
            def _():
                emit(e, b0, b, rows)

            return jnp.where(full, b, b0), jnp.where(full, pb, rows + pb)

        b0, rows = lax.fori_loop(0, N_TILES, per_tile, (jnp.int32(0), jnp.int32(0)))

        @pl.when(rows > 0)
        def _():
            emit(e, b0, jnp.int32(N_TILES), rows)

        return jnp.where(rows > 0, e, last_e)

    last_e = lax.fori_loop(0, N_EXPERTS, per_expert, jnp.int32(0))

    def idle(it, c):
        iexp_ref[it] = last_e
        ib0_ref[it] = 0
        ib1_ref[it] = 0
        irows_ref[it] = 0
        return c

    lax.fori_loop(cnt_ref[0], ITEM_TABLE, idle, 0)

    def per_item(it, carry):
        n_big, n_small = carry
        ibig_ref[it] = n_big
        ismall_ref[it] = n_small
        e = iexp_ref[it]

        def per_tile(b, c):
            dst, n_big, n_small = c
            pb = p_ref[b * N_EXPERTS + e]
            word = (b << 22) | (s_ref[b * N_EXPERTS + e] << 11) | dst
            nb = pb >> BIG_SHIFT
            ns = (pb - nb * BIG) >> CHUNK_SHIFT

            def big(k, z):
                big_ref[n_big + k] = word + k * (BIG << 11 | BIG)
                return z

            def small(k, z):
                small_ref[n_small + k] = word + nb * (BIG << 11 | BIG) + k * (CHUNK << 11 | CHUNK)
                return z

            lax.fori_loop(0, nb, big, 0)
            lax.fori_loop(0, ns, small, 0)
            return dst + pb, n_big + nb, n_small + ns

        _, n_big, n_small = lax.fori_loop(ib0_ref[it], ib1_ref[it], per_tile, (jnp.int32(0), n_big, n_small))
        return n_big, n_small

    n_big, n_small = lax.fori_loop(0, ITEM_TABLE, per_item, (jnp.int32(0), jnp.int32(0)))

    def clear_big(k, z):
        big_ref[k] = 0
        return z

    def clear_small(k, z):
        small_ref[k] = 0
        return z

    lax.fori_loop(n_big, BIG_LIST, clear_big, 0)
    lax.fori_loop(n_small, SMALL_LIST, clear_small, 0)


def _plan(ptab, stab):
    smem = pl.BlockSpec(memory_space=pltpu.SMEM)
    table = jax.ShapeDtypeStruct((ITEM_TABLE,), I32)
    return pl.pallas_call(
        _plan_kernel,
        in_specs=[smem, smem],
        out_specs=[smem] * 8,
        out_shape=[table] * 6 + [jax.ShapeDtypeStruct((BIG_LIST,), I32), jax.ShapeDtypeStruct((SMALL_LIST,), I32)],
        scratch_shapes=[pltpu.SMEM((1,), I32)],
        name="moe_plan",
    )(ptab, stab)


def _unpack_halves(p):
    lo = pltpu.unpack_elementwise(p, index=0, packed_dtype=BF16, unpacked_dtype=F32)
    hi = pltpu.unpack_elementwise(p, index=1, packed_dtype=BF16, unpacked_dtype=F32)
    return lo.astype(BF16), hi.astype(BF16)


def _unpack_rows(p):
    return jnp.concatenate(_unpack_halves(p), axis=1)


def _pack_rows(y):
    return pltpu.pack_elementwise([y[:, :PACKED_W], y[:, PACKED_W:]], packed_dtype=BF16)


def _expert_kernel(iexp_ref, irows_ref, ibig_ref, ismall_ref, big_ref, small_ref, xsb_hbm, w1_hbm, w2_hbm,
                   b1_ref, b2_ref, ysb_hbm, xg_ref, g_ref, a_ref, yb_ref, w_ref, cnt_ref,
                   in_big, in_small, out_big, out_small, w_sem, *, layer):
    it = pl.program_id(0)
    rows = irows_ref[it]
    slot = it & 1

    def weight_copies(item, phase):
        ee = iexp_ref[item]
        copies = []
        for k in range(2):
            half = pl.ds(k * (D_MODEL // 2), D_MODEL // 2)
            if phase < 2:
                src = w1_hbm.at[layer, ee, half, pl.ds(phase * D_FF, D_FF)]
            else:
                src = w2_hbm.at[layer, ee, half, :]
            copies.append(pltpu.make_async_copy(src, w_ref.at[phase, half, :], w_sem.at[phase]))
        return copies

    def start_weights(item, phase):
        @pl.when(irows_ref[item] > 0)
        def _():
            for cp in weight_copies(item, phase):
                cp.start()

    def wait_weights(phase):
        for cp in weight_copies(it, phase):
            cp.wait()

    def aligned(v):
        return v if isinstance(v, int) else pl.multiple_of(v, CHUNK)

    def copy(to_vmem, buf, b, src, dst, size):
        hbm = (xsb_hbm if to_vmem else ysb_hbm).at[b, pl.ds(aligned(src), size), :]
        if to_vmem:
            sem = in_big if size == BIG else in_small
            return pltpu.make_async_copy(hbm, xg_ref.at[buf, pl.ds(aligned(dst), size), :], sem)
        sem = out_big if size == BIG else out_small
        return pltpu.make_async_copy(yb_ref.at[pl.ds(aligned(dst), size), :], hbm, sem)

    def start_segments(item, to_vmem, buf):
        def start_list(lst_ref, lo, hi, size):
            def body(i, z):
                word = lst_ref[i]
                copy(to_vmem, buf, word >> 22, (word >> 11) & 2047, word & 2047, size).start()
                return z

            lax.fori_loop(lo, hi, body, 0)
            return hi - lo

        return (start_list(big_ref, ibig_ref[item], ibig_ref[item + 1], BIG),
                start_list(small_ref, ismall_ref[item], ismall_ref[item + 1], CHUNK))

    def wait_segments(to_vmem, n_big, n_small):
        def wait_big(c, z):
            copy(to_vmem, 0, 0, 0, 0, BIG).wait()
            return z

        def wait_small(c, z):
            copy(to_vmem, 0, 0, 0, 0, CHUNK).wait()
            return z

        lax.fori_loop(0, n_big, wait_big, 0)
        lax.fori_loop(0, n_small, wait_small, 0)

    @pl.when(it == 0)
    def _init():
        xg_ref[...] = jnp.zeros_like(xg_ref)
        start_weights(0, 0)
        start_weights(0, 1)
        cnt_ref[0], cnt_ref[1] = start_segments(0, True, 0)
        cnt_ref[2] = 0
        cnt_ref[3] = 0

    def for_sub_tiles(fn):
        n_full = rows >> SUB_SHIFT
        rem_parts = (rows - (n_full << SUB_SHIFT) + (PART - 1)) >> PART_SHIFT

        def body(i, c):
            fn(pl.ds(pl.multiple_of(i * SUB, SUB), SUB))
            return c

        lax.fori_loop(0, n_full, body, 0)
        r0 = pl.multiple_of(n_full * SUB, SUB)
        for k in range(1, SUB // PART + 1):
            @pl.when(rem_parts == k)
            def _(k=k):
                fn(pl.ds(r0, k * PART))

    def matmul(lo, hi, phase):
        half = D_MODEL // 2
        return (jnp.dot(lo, w_ref[phase, :half, :].astype(BF16), preferred_element_type=F32)
                + jnp.dot(hi, w_ref[phase, half:, :].astype(BF16), preferred_element_type=F32))

    @pl.when(rows > 0)
    def _item():
        start_weights(it, 2)
        wait_segments(True, cnt_ref[0], cnt_ref[1])
        wait_weights(0)

        def gate(r):
            lo, hi = _unpack_halves(xg_ref[slot, r, :])
            gate = jnp.minimum(matmul(lo, hi, 0) + b1_ref[:, :D_FF], SWIGLU_LIMIT)
            g_ref[r, :] = gate * jax.nn.sigmoid(SWIGLU_ALPHA * gate)

        for_sub_tiles(gate)

        start_weights(it + 1, 0)
        cnt_ref[0], cnt_ref[1] = start_segments(it + 1, True, 1 - slot)
        wait_weights(1)

        def up_act(r):
            lo, hi = _unpack_halves(xg_ref[slot, r, :])
            up = matmul(lo, hi, 1) + b1_ref[:, D_FF:]
            up = jnp.clip(up, -SWIGLU_LIMIT, SWIGLU_LIMIT)
            a_ref[r, :] = (g_ref[r, :] * (up + 1.0)).astype(BF16)

        for_sub_tiles(up_act)

        start_weights(it + 1, 1)
        wait_weights(2)
        wait_segments(False, cnt_ref[2], cnt_ref[3])

        def down(r):
            y = matmul(a_ref[r, :D_FF // 2], a_ref[r, D_FF // 2:], 2) + b2_ref[...]
            yb_ref[r, :] = _pack_rows(y)

        for_sub_tiles(down)
        cnt_ref[2], cnt_ref[3] = start_segments(it, False, 0)

    @pl.when(it == MAX_ITEMS - 1)
    def _drain():
        wait_segments(False, cnt_ref[2], cnt_ref[3])


def _experts(iexp, irows, ibig, ismall, big, small, xsb, l, w1, b1, w2, b2):
    n_prefetch = 6
    any_spec = pl.BlockSpec(memory_space=pl.ANY)
    return pl.pallas_call(
        functools.partial(_expert_kernel, layer=l),
        grid_spec=pltpu.PrefetchScalarGridSpec(
            num_scalar_prefetch=n_prefetch,
            grid=(MAX_ITEMS,),
            in_specs=[any_spec, any_spec, any_spec,
                      pl.BlockSpec((None, None, 1, 2 * D_FF), lambda it, ie, *_: (l, ie[it], 0, 0)),
                      pl.BlockSpec((None, None, 1, D_MODEL), lambda it, ie, *_: (l, ie[it], 0, 0))],
            out_specs=pl.BlockSpec(memory_space=pl.ANY),
            scratch_shapes=[pltpu.VMEM((2, ITEM_ROWS, PACKED_W), I32),
                            pltpu.VMEM((ITEM_ROWS, D_FF), F32),
                            pltpu.VMEM((ITEM_ROWS, D_FF), BF16),
                            pltpu.VMEM((ITEM_ROWS, PACKED_W), I32),
                            pltpu.VMEM((3, D_MODEL, D_FF), F32),
                            pltpu.SMEM((4,), I32),
                            pltpu.SemaphoreType.DMA(()),
                            pltpu.SemaphoreType.DMA(()),
                            pltpu.SemaphoreType.DMA(()),
                            pltpu.SemaphoreType.DMA(()),
                            pltpu.SemaphoreType.DMA((3,))],
        ),
        out_shape=jax.ShapeDtypeStruct((N_TILES, TILE_ROWS, PACKED_W), I32),
        input_output_aliases={n_prefetch: 0},
        compiler_params=_cparams(("arbitrary",)),
        name="moe_experts",
    )(iexp, irows, ibig, ismall, big, small, xsb, w1, w2,
      b1.reshape(DEPTH, N_EXPERTS, 1, 2 * D_FF), b2.reshape(DEPTH, N_EXPERTS, 1, D_MODEL))


def _combine_kernel(*refs, n_x, n_out):
    ysb_ref, route_ref, eye_ref, mod_ref, g_ref, b_ref = refs[n_x:len(refs) - n_out]
    o_refs = refs[len(refs) - n_out:]
    rt = lax.dot_general(eye_ref[...], route_ref[...], (((1,), (1,)), ((), ())), precision=HIGHEST,
                         preferred_element_type=F32)
    lane = lax.broadcasted_iota(I32, (MOE_TM, TILE_ROWS), 1)
    c = jnp.zeros((MOE_TM, TILE_ROWS), F32)
    for k in range(TOP_K):
        c = jnp.where(lane == rt[:, k:k + 1].astype(I32), rt[:, TOP_K + k:TOP_K + k + 1], c)
    f = jnp.dot(c.astype(BF16), _unpack_rows(ysb_ref[...]), preferred_element_type=F32)
    x = _stream_tile(refs[:n_x], MOE_TM)
    out = _layer_norm(DEEPNORM_ALPHA * x + mod_ref[5:6, :] * f, g_ref[...], b_ref[...])
    if n_out == 1:
        o_refs[0][...] = out
    else:
        is_ctx = pl.program_id(0) < N_PROMPT // MOE_TM

        @pl.when(is_ctx)
        def _():
            o_refs[0][...] = out

        @pl.when(jnp.logical_not(is_ctx))
        def _():
            o_refs[1][...] = out


def _combine(x1, ysb, route, mod, l, g, b, split_out):
    x_specs, x_args = _stream_specs(x1, MOE_TM)
    if split_out:
        out_specs, _ = _stream_specs((None, None), MOE_TM)
        out_shape = [jax.ShapeDtypeStruct((N_PROMPT, D_MODEL), F32), jax.ShapeDtypeStruct((N_SAMPLE, D_MODEL), F32)]
    else:
        out_specs, _ = _stream_specs(None, MOE_TM)
        out_shape = [jax.ShapeDtypeStruct((N_TOK, D_MODEL), F32)]
    return pl.pallas_call(
        functools.partial(_combine_kernel, n_x=len(x_args), n_out=len(out_shape)),
        grid=(N_TILES,),
        in_specs=x_specs + [pl.BlockSpec((None, TILE_ROWS, PACKED_W), lambda i: (i, 0, 0)),
                            pl.BlockSpec((None, 2 * TOP_K, MOE_TM), lambda i: (i, 0, 0)),
                            _full((MOE_TM, MOE_TM)), _mod_spec(l, MOE_TM), _full((1, D_MODEL)),
                            _full((1, D_MODEL))],
        out_specs=out_specs,
        out_shape=out_shape,
        compiler_params=_cparams(("arbitrary",)),
        name="moe_combine",
    )(*x_args, ysb, route, jnp.eye(MOE_TM, dtype=F32), mod, g.reshape(1, D_MODEL), b.reshape(1, D_MODEL))


def _moe_layer(x1, mod, l, rw, rb, w1, b1, w2, b2, ln_g, ln_b, split_out=False):
    xsb, route, tab = _dispatch(x1, mod, l, rw, rb)
    ptab = tab[:, :, 1].astype(I32).reshape(N_SEG)
    stab = tab[:, :, 2].astype(I32).reshape(N_SEG)
    iexp, _, _, irows, ibig, ismall, big, small = _plan(ptab, stab)
    ysb = _experts(iexp, irows, ibig, ismall, big, small, xsb, l, w1, b1, w2, b2)
    return _combine(x1, ysb, route, mod, l, ln_g, ln_b, split_out)


def kernel(x_prompt, x_sample, cache_k_attn, cache_v_attn, cache_k_na, cache_v_na, c, c_ctx, w_mod, b_mod, ln1_g, ln1_b, ln2_g, ln2_b, conv_w_in, conv_w, conv_b, conv_w_out, attn_w_qkv, attn_q_norm, attn_k_norm, attn_w_o, na_w_qkv, na_rpb, na_w_o, router_w, router_b, moe_w1, moe_b1, moe_w2, moe_b2):
    x = (x_prompt.reshape(N_PROMPT, D_MODEL), x_sample.reshape(N_SAMPLE, D_MODEL))
    cond8 = jnp.concatenate([c_ctx[None, :], c, jnp.zeros((8 - 1 - DEC_BATCH, D_MODEL), F32)], axis=0)
    mod = _adaln_all(cond8, w_mod, b_mod)
    new_kv = {}
    for l in range(DEPTH):
        kind, j = l % 3, l // 3
        if kind == 0:
            x1 = _conv_layer(x, mod, l, conv_w_in[j], conv_w[j], conv_b[j], conv_w_out[j], ln1_g[l], ln1_b[l])
        elif kind == 1:
            gq = jnp.tile(attn_q_norm[j], N_HEADS).reshape(1, N_HEADS * HEAD_DIM)
            gk = jnp.tile(attn_k_norm[j], N_KV_HEADS).reshape(1, N_KV_HEADS * HEAD_DIM)
            norm_args = (gq, gk, _block_diag_mean())
            wk = N_KV_HEADS * HEAD_DIM
            xp1, nk, nv = _attn_call(x, mod, l, False, attn_w_qkv[j], attn_w_o[j], ln1_g[l], ln1_b[l],
                                     n_seq=BATCH, seq_len=SEQ, tq=SEQ, n_kv=N_KV_HEADS, norm_args=norm_args,
                                     emit_kv=True, name="gqa_prompt")
            new_kv["k_attn"] = nk.reshape(BATCH, 1, SEQ, N_KV_HEADS, HEAD_DIM)
            new_kv["v_attn"] = nv.reshape(BATCH, 1, SEQ, N_KV_HEADS, HEAD_DIM)
            (xs1,) = _attn_call(x, mod, l, True, attn_w_qkv[j], attn_w_o[j], ln1_g[l], ln1_b[l],
                                n_seq=DEC_BATCH, seq_len=DEC_SEQ, tq=256, n_kv=N_KV_HEADS, norm_args=norm_args,
                                rope_args=_rope_tables(),
                                ctx_args=(cache_k_attn[:, j].reshape(DEC_BATCH, PAST_LEN, wk),
                                          cache_v_attn[:, j].reshape(DEC_BATCH, PAST_LEN, wk)),
                                name="gqa_sample")
            x1 = (xp1, xs1)
        else:
            wk = N_HEADS * HEAD_DIM
            xp1, nk, nv = _attn_call(x, mod, l, False, na_w_qkv[j], na_w_o[j], ln1_g[l], ln1_b[l],
                                     n_seq=BATCH, seq_len=SEQ, tq=SEQ, n_kv=N_HEADS, emit_kv=True,
                                     name="mha_prompt")
            new_kv["k_na"] = nk.reshape(BATCH, 1, SEQ, N_HEADS, HEAD_DIM)
            new_kv["v_na"] = nv.reshape(BATCH, 1, SEQ, N_HEADS, HEAD_DIM)
            (xs1,) = _attn_call(x, mod, l, True, na_w_qkv[j], na_w_o[j], ln1_g[l], ln1_b[l],
                                n_seq=DEC_BATCH, seq_len=DEC_SEQ, tq=GRID_W, n_kv=N_HEADS,
                                ctx_args=(cache_k_na[:, j].reshape(DEC_BATCH, PAST_LEN, wk),
                                          cache_v_na[:, j].reshape(DEC_BATCH, PAST_LEN, wk)),
                                bias=_na_bias_table(na_rpb[j]), name="na_sample")
            x1 = (xp1, xs1)
        out = _moe_layer(x1, mod, l, router_w[l], router_b[l], moe_w1, moe_b1, moe_w2, moe_b2,
                         ln2_g[l], ln2_b[l], split_out=l == DEPTH - 1)
        x = out[0]
    y_prompt = out[0].reshape(BATCH, SEQ, D_MODEL)
    y_sample = out[1].reshape(DEC_BATCH, DEC_SEQ, D_MODEL)
    return (y_prompt, y_sample, new_kv["k_attn"], new_kv["v_attn"], new_kv["k_na"], new_kv["v_na"])
```

```python
import functools

import numpy as np
import jax
import jax.numpy as jnp
from jax import lax
from jax.experimental import pallas as pl
from jax.experimental.pallas import tpu as pltpu

D_MODEL = 1024
BATCH = 16
SEQ = 256
DEPTH = 4
DEC_BATCH = 2
DEC_SEQ = 1024
PAST_LEN = 256
GRID_W = 64
HEAD_DIM = 64
N_HEADS = 16
N_KV_HEADS = 4
ROPE_THETA = 10000.0
ROPE_PAIRS = HEAD_DIM // 4
WIN_R = 8
WIN_C = 16
N_EXPERTS = 32
TOP_K = 4
D_FF = D_MODEL
SWIGLU_LIMIT = 7.0
SWIGLU_ALPHA = 1.702
DEEPNORM_ALPHA = (2 * DEPTH) ** 0.25
LN_EPS = 1e-5
RMS_EPS = 1e-6

F32 = jnp.float32
BF16 = jnp.bfloat16
I32 = jnp.int32
HIGHEST = lax.Precision.HIGHEST

N_PROMPT = BATCH * SEQ
N_SAMPLE = DEC_BATCH * DEC_SEQ
N_TOK = N_PROMPT + N_SAMPLE
N_ASSIGN = N_TOK * TOP_K

LANES = 128
SUBLANES = 8
PACKED_W = D_MODEL // 2

VMEM_LIMIT = 58 * 1024 * 1024


def _cparams(sem):
    return pltpu.CompilerParams(dimension_semantics=sem, vmem_limit_bytes=VMEM_LIMIT)


def _layer_norm(x, g, b):
    mu = jnp.mean(x, -1, keepdims=True)
    xc = x - mu
    var = jnp.mean(xc * xc, -1, keepdims=True)
    return xc * lax.rsqrt(var + LN_EPS) * g + b


def _mod_row(i, tile):
    n_prompt_tiles = N_PROMPT // tile
    return jnp.where(i < n_prompt_tiles, 0, 1 + (i - n_prompt_tiles) // (DEC_SEQ // tile))


def _mod_spec(l, tile):
    return pl.BlockSpec((None, None, 6, D_MODEL), lambda i, *_: (l, _mod_row(i, tile), 0, 0))


def _full(shape):
    nd = len(shape)
    return pl.BlockSpec(shape, lambda *_: (0,) * nd)


def _stream_specs(x, tile):
    if not isinstance(x, tuple):
        return [pl.BlockSpec((tile, D_MODEL), lambda i, *_: (i, 0))], [x]
    n_a = N_PROMPT // tile
    return ([pl.BlockSpec((tile, D_MODEL), lambda i, *_: (jnp.minimum(i, n_a - 1), 0)),
             pl.BlockSpec((tile, D_MODEL), lambda i, *_: (jnp.maximum(i - n_a, 0), 0))], list(x))


def _stream_tile(x_refs, tile):
    if len(x_refs) == 1:
        return x_refs[0][...]
    return jnp.where(pl.program_id(0) < N_PROMPT // tile, x_refs[0][...], x_refs[1][...])


ADALN_TN = 1536


def _adaln_kernel(cond_ref, wa_ref, wb_ref, b_ref, o_ref):
    c = cond_ref[...]
    s = c * jax.nn.sigmoid(c)
    half = D_MODEL // 2
    o_ref[...] = (jnp.dot(s[:, :half], wa_ref[...], precision=HIGHEST, preferred_element_type=F32)
                  + jnp.dot(s[:, half:], wb_ref[...], precision=HIGHEST, preferred_element_type=F32)
                  + b_ref[...])


def _adaln_all(cond8, w_mod, b_mod):
    n = 6 * D_MODEL
    out = pl.pallas_call(
        _adaln_kernel,
        grid=(DEPTH, n // ADALN_TN),
        in_specs=[
            pl.BlockSpec((8, D_MODEL), lambda l, j: (0, 0)),
            pl.BlockSpec((None, D_MODEL // 2, ADALN_TN), lambda l, j: (l, 0, j)),
            pl.BlockSpec((None, D_MODEL // 2, ADALN_TN), lambda l, j: (l, 1, j)),
            pl.BlockSpec((None, 1, ADALN_TN), lambda l, j: (l, 0, j)),
        ],
        out_specs=pl.BlockSpec((None, 8, ADALN_TN), lambda l, j: (l, 0, j)),
        out_shape=jax.ShapeDtypeStruct((DEPTH, 8, n), F32),
        compiler_params=_cparams(("arbitrary", "arbitrary")),
        name="adaln",
    )(cond8, w_mod, w_mod, b_mod.reshape(DEPTH, 1, n))
    return out.reshape(DEPTH, 8, 6, D_MODEL)


CONV_TM = 1024


def _conv_kernel(*refs, n_x):
    mod_ref, win_ref, cw_ref, cb_ref, wout_ref, g_ref, b_ref, o_ref = refs[n_x:]
    i = pl.program_id(0)
    x = _stream_tile(refs[:n_x], CONV_TM)
    h = (x * (1.0 + mod_ref[1:2, :]) + mod_ref[0:1, :]).astype(BF16)
    gc = jnp.dot(h, win_ref[:, D_MODEL:2 * D_MODEL], preferred_element_type=F32)
    xv = jnp.dot(h, win_ref[:, 2 * D_MODEL:], preferred_element_type=F32)
    u = gc * xv
    seq_len = jnp.where(i < N_PROMPT // CONV_TM, SEQ, DEC_SEQ)
    t = lax.broadcasted_iota(I32, (CONV_TM, 1), 0) & (seq_len - 1)
    u_prev = jnp.where(t == 0, 0.0, pltpu.roll(u, 1, axis=0))
    u_next = jnp.where(t == seq_len - 1, 0.0, pltpu.roll(u, CONV_TM - 1, axis=0))
    y = u_prev * cw_ref[0:1, :] + u * cw_ref[1:2, :] + u_next * cw_ref[2:3, :] + cb_ref[...]
    gb = jnp.dot(h, win_ref[:, :D_MODEL], preferred_element_type=F32)
    v = (gb * y).astype(BF16)
    o = jnp.dot(v, wout_ref[...], preferred_element_type=F32)
    o_ref[...] = _layer_norm(DEEPNORM_ALPHA * x + mod_ref[2:3, :] * o, g_ref[...], b_ref[...])


def _conv_layer(x, mod, l, w_in, cw, cb, w_out, ln_g, ln_b):
    x_specs, x_args = _stream_specs(x, CONV_TM)
    return pl.pallas_call(
        functools.partial(_conv_kernel, n_x=len(x_args)),
        grid=(N_TOK // CONV_TM,),
        in_specs=x_specs + [_mod_spec(l, CONV_TM), _full((D_MODEL, 3 * D_MODEL)), _full((3, D_MODEL)),
                            _full((1, D_MODEL)), _full((D_MODEL, D_MODEL)), _full((1, D_MODEL)),
                            _full((1, D_MODEL))],
        out_specs=pl.BlockSpec((CONV_TM, D_MODEL), lambda i: (i, 0)),
        out_shape=jax.ShapeDtypeStruct((N_TOK, D_MODEL), F32),
        compiler_params=_cparams(("arbitrary",)),
        name="conv_mixer",
    )(*x_args, mod, w_in.astype(BF16), cw, cb.reshape(1, D_MODEL), w_out.astype(BF16),
      ln_g.reshape(1, D_MODEL), ln_b.reshape(1, D_MODEL))


ATTN_CHUNK = 256


def _attn_kernel(*refs, seq_len, tq, n_kv, norm, rope, n_ctx, na, emit_kv):
    refs = list(refs)
    x_ref, mod_ref, wqkv_ref = refs[:3]
    pos = 3
    if norm:
        gq_ref, gk_ref, bd_ref = refs[pos:pos + 3]
        pos += 3
    if rope:
        cos_ref, s1_ref, s2_ref = refs[pos:pos + 3]
        pos += 3
    if n_ctx:
        ck_ref, cv_ref = refs[pos:pos + 2]
        pos += 2
    if na:
        bias_ref = refs[pos]
        pos += 1
    wo_ref, lng_ref, lnb_ref = refs[pos:pos + 3]
    pos += 3
    o_ref = refs[pos]
    pos += 1
    if emit_kv:
        nk_ref, nv_ref = refs[pos:pos + 2]
        pos += 2
    q_scr, k_scr, v_scr, o_scr = refs[pos:pos + 4]

    qt = pl.program_id(1)
    n_qt = seq_len // tq
    wq = N_HEADS * HEAD_DIM
    wk = n_kv * HEAD_DIM
    rep = N_HEADS // n_kv

    def rms(v, g_ref, width):
        ms = jnp.dot((v * v).astype(BF16), bd_ref[:width, :width], preferred_element_type=F32)
        return v * lax.rsqrt(ms + RMS_EPS) * g_ref[...]

    def rot(v, rows, width):
        def tab(ref):
            t = ref[rows, :]
            return jnp.concatenate([t] * (width // LANES), axis=1)

        return (v * tab(cos_ref) + pltpu.roll(v, width - ROPE_PAIRS, axis=1) * tab(s1_ref)
                + pltpu.roll(v, ROPE_PAIRS, axis=1) * tab(s2_ref))

    chunk = min(seq_len, ATTN_CHUNK)

    @pl.when(qt == 0)
    def _project():
        def body(ci, carry):
            r0 = pl.multiple_of(ci * chunk, chunk)
            rows = pl.ds(r0, chunk)
            h = (x_ref[rows, :] * (1.0 + mod_ref[1:2, :]) + mod_ref[0:1, :]).astype(BF16)
            q = jnp.dot(h, wqkv_ref[:, :wq], preferred_element_type=F32)
            k = jnp.dot(h, wqkv_ref[:, wq:wq + wk], preferred_element_type=F32)
            v = jnp.dot(h, wqkv_ref[:, wq + wk:], preferred_element_type=F32)
            if norm:
                q = rms(q, gq_ref, wq)
                k = rms(k, gk_ref, wk)
            if emit_kv:
                nk_ref[rows, :] = k
                nv_ref[rows, :] = v
            if rope:
                q = rot(q, rows, wq)
                k = rot(k, rows, wk)
            q_scr[rows, :] = (q * (HEAD_DIM ** -0.5)).astype(BF16)
            k_scr[pl.ds(n_ctx + r0, chunk), :] = k.astype(BF16)
            v_scr[pl.ds(n_ctx + r0, chunk), :] = v.astype(BF16)
            return carry

        lax.fori_loop(0, seq_len // chunk, body, 0)
        if n_ctx:
            k_scr[:n_ctx, :] = ck_ref[...].astype(BF16)
            v_scr[:n_ctx, :] = cv_ref[...].astype(BF16)

    q0 = pl.multiple_of(qt * tq, tq)
    if na:
        row_start = jnp.clip(qt - WIN_R // 2, 0, DEC_SEQ // GRID_W - WIN_R)
        d0 = row_start - qt + (WIN_R - 1)
        k0 = pl.multiple_of(n_ctx + row_start * GRID_W, GRID_W)
    for hd in range(N_HEADS):
        g = hd // rep
        hs = slice(hd * HEAD_DIM, (hd + 1) * HEAD_DIM)
        gs = slice(g * HEAD_DIM, (g + 1) * HEAD_DIM)
        qh = q_scr[pl.ds(q0, tq), hs]
        dn = (((1,), (1,)), ((), ()))
        if na:
            kc, vc = k_scr[:n_ctx, gs], v_scr[:n_ctx, gs]
            kl, vl = k_scr[pl.ds(k0, WIN_R * GRID_W), gs], v_scr[pl.ds(k0, WIN_R * GRID_W), gs]
            bias = jnp.concatenate([bias_ref[hd, pl.ds(d0 + 2 * j, 1)][0] for j in range(WIN_R // 2)], axis=1)
            s = jnp.concatenate([lax.dot_general(qh, kc, dn, preferred_element_type=F32),
                                 lax.dot_general(qh, kl, dn, preferred_element_type=F32) + bias], axis=1)
        else:
            s = lax.dot_general(qh, k_scr[:, gs], dn, preferred_element_type=F32)
        e = jnp.exp(s - jnp.max(s, axis=1, keepdims=True))
        den = jnp.sum(e, axis=1, keepdims=True)
        eb = e.astype(BF16)
        if na:
            oh = (jnp.dot(eb[:, :n_ctx], vc, preferred_element_type=F32)
                  + jnp.dot(eb[:, n_ctx:], vl, preferred_element_type=F32))
        else:
            oh = jnp.dot(eb, v_scr[:, gs], preferred_element_type=F32)
        o_scr[pl.ds(q0, tq), hs] = (oh / den).astype(BF16)

    @pl.when(qt == n_qt - 1)
    def _finish():
        def body(ci, carry):
            rows = pl.ds(pl.multiple_of(ci * chunk, chunk), chunk)
            o = jnp.dot(o_scr[rows, :], wo_ref[...], preferred_element_type=F32)
            o_ref[rows, :] = _layer_norm(DEEPNORM_ALPHA * x_ref[rows, :] + mod_ref[2:3, :] * o,
                                         lng_ref[...], lnb_ref[...])
            return carry

        lax.fori_loop(0, seq_len // chunk, body, 0)


def _attn_call(x, mod, l, latent, w_qkv, w_o, ln_g, ln_b, *, n_seq, seq_len, tq, n_kv, norm_args=None,
               rope_args=None, ctx_args=None, bias=None, emit_kv=False, name="attn"):
    wq = N_HEADS * HEAD_DIM
    wk = n_kv * HEAD_DIM
    n_ctx = PAST_LEN if ctx_args is not None else 0
    n_qt = seq_len // tq
    seq_off = N_PROMPT // seq_len if latent else 0
    seq_spec = pl.BlockSpec((seq_len, D_MODEL), lambda s, t: (s, 0))
    mod_spec = pl.BlockSpec((None, None, 6, D_MODEL), lambda s, t: (l, (1 + s) if latent else 0, 0, 0))
    in_specs = [pl.BlockSpec((seq_len, D_MODEL), lambda s, t: (s + seq_off, 0)), mod_spec,
                pl.BlockSpec((D_MODEL, wq + 2 * wk), lambda s, t: (0, 0))]
    args = [x, mod, w_qkv.astype(BF16)]
    if norm_args is not None:
        gq, gk, bd = norm_args
        in_specs += [pl.BlockSpec((1, wq), lambda s, t: (0, 0)), pl.BlockSpec((1, wk), lambda s, t: (0, 0)),
                     pl.BlockSpec((wq, wq), lambda s, t: (0, 0))]
        args += [gq, gk, bd]
    if rope_args is not None:
        in_specs += [pl.BlockSpec((seq_len, LANES), lambda s, t: (0, 0))] * 3
        args += list(rope_args)
    if ctx_args is not None:
        in_specs += [pl.BlockSpec((None, n_ctx, wk), lambda s, t: (s, 0, 0))] * 2
        args += list(ctx_args)
    if bias is not None:
        in_specs += [pl.BlockSpec(bias.shape, lambda s, t: (0, 0, 0, 0))]
        args += [bias]
    in_specs += [pl.BlockSpec((D_MODEL, D_MODEL), lambda s, t: (0, 0)),
                 pl.BlockSpec((1, D_MODEL), lambda s, t: (0, 0)), pl.BlockSpec((1, D_MODEL), lambda s, t: (0, 0))]
    args += [w_o.astype(BF16), ln_g.reshape(1, D_MODEL), ln_b.reshape(1, D_MODEL)]
    out_specs = [seq_spec]
    out_shape = [jax.ShapeDtypeStruct((n_seq * seq_len, D_MODEL), F32)]
    if emit_kv:
        out_specs += [pl.BlockSpec((seq_len, wk), lambda s, t: (s, 0))] * 2
        out_shape += [jax.ShapeDtypeStruct((n_seq * seq_len, wk), F32)] * 2
    kern = functools.partial(_attn_kernel, seq_len=seq_len, tq=tq, n_kv=n_kv, norm=norm_args is not None,
                             rope=rope_args is not None, n_ctx=n_ctx, na=bias is not None, emit_kv=emit_kv)
    return pl.pallas_call(
        kern,
        grid=(n_seq, n_qt),
        in_specs=in_specs,
        out_specs=out_specs,
        out_shape=out_shape,
        scratch_shapes=[pltpu.VMEM((seq_len, wq), BF16), pltpu.VMEM((n_ctx + seq_len, wk), BF16),
                        pltpu.VMEM((n_ctx + seq_len, wk), BF16), pltpu.VMEM((seq_len, wq), BF16)],
        compiler_params=_cparams(("arbitrary", "arbitrary")),
        name=name,
    )(*args)


def _rope_tables():
    t = np.arange(DEC_SEQ)
    pos = np.stack([t // GRID_W, t % GRID_W], axis=1).astype(np.float64)
    inv = (ROPE_THETA ** (-np.arange(ROPE_PAIRS, dtype=np.float32) / ROPE_PAIRS)).astype(np.float64)
    ang = pos[:, :, None] * inv[None, None, :]
    cos, sin = np.cos(ang), np.sin(ang)
    zero = np.zeros_like(sin)
    cos_t = np.concatenate([cos, cos], axis=2).reshape(DEC_SEQ, HEAD_DIM)
    s1_t = np.concatenate([-sin, zero], axis=2).reshape(DEC_SEQ, HEAD_DIM)
    s2_t = np.concatenate([zero, sin], axis=2).reshape(DEC_SEQ, HEAD_DIM)
    return tuple(jnp.asarray(np.tile(a, (1, LANES // HEAD_DIM)), F32) for a in (cos_t, s1_t, s2_t))


def _block_diag_mean():
    a = np.kron(np.eye(N_HEADS), np.full((HEAD_DIM, HEAD_DIM), 1.0 / HEAD_DIM))
    return jnp.asarray(a, BF16)


def _bias_kernel(rpb_ref, sel_ref, inside_ref, o_ref):
    v = jnp.dot(rpb_ref[...], sel_ref[...], precision=HIGHEST, preferred_element_type=F32)
    o_ref[...] = jnp.where(inside_ref[...] > 0.0, v, -1e30)


def _na_bias_table(rpb):
    n_rows, n_off = N_HEADS * (2 * WIN_R - 1), 2 * WIN_C - 1
    cols = np.arange(GRID_W)
    col_start = np.clip(cols - WIN_C // 2, 0, GRID_W - WIN_C)
    kc = np.arange(GRID_W)
    inside = (kc[None, :] >= col_start[:, None]) & (kc[None, :] < col_start[:, None] + WIN_C)
    off = np.clip(kc[None, :] - cols[:, None] + (WIN_C - 1), 0, n_off - 1)
    sel = (np.arange(LANES)[:, None] == off.reshape(1, -1)).astype(np.float32)
    rpb2 = jnp.pad(rpb.reshape(n_rows, n_off), ((0, 0), (0, LANES - n_off)))
    blocks = pl.pallas_call(
        _bias_kernel,
        out_shape=jax.ShapeDtypeStruct((n_rows, GRID_W * GRID_W), F32),
        name="na_bias",
    )(rpb2, jnp.asarray(sel), jnp.asarray(inside.reshape(1, -1).astype(np.float32)))
    blocks = blocks.reshape(N_HEADS, 2 * WIN_R - 1, GRID_W, GRID_W)
    return jnp.concatenate([blocks[:, :-1], blocks[:, 1:]], axis=-1)


MOE_TM = 256
N_TILES = N_TOK // MOE_TM
CHUNK_SHIFT, BIG_SHIFT, PART_SHIFT, SUB_SHIFT = 3, 5, 7, 9
CHUNK = 1 << CHUNK_SHIFT
BIG = 1 << BIG_SHIFT
TILE_ROWS = 1280
SUB = 1 << SUB_SHIFT
PART = 1 << PART_SHIFT
ITEM_ROWS = 2048
MAX_ITEMS = 48
ITEM_TABLE = 64
MAX_ROWS = N_ASSIGN + N_TILES * N_EXPERTS * (CHUNK - 1)
BIG_LIST = 1024
SMALL_LIST = 2560
N_SEG =N_TILES * N_EXPERTS

assert TILE_ROWS >= MOE_TM * TOP_K + N_EXPERTS * (CHUNK - 1) and TILE_ROWS % MOE_TM == 0 and ITEM_ROWS % SUB == 0
assert MAX_ITEMS >= N_EXPERTS + (N_ASSIGN + N_SEG * (CHUNK - 1) - 1) // (ITEM_ROWS - MOE_TM)
assert BIG_LIST >= MAX_ROWS // BIG + MOE_TM // BIG and SMALL_LIST >= (N_SEG + 1) * (BIG // CHUNK - 1)
assert ITEM_ROWS <= 1 << 11


def _dispatch_kernel(*refs, n_x):
    mod_ref, rwt_ref, rbc_ref, upper_ref, lower_ref, xsb_ref, route_ref, tab_ref = refs[n_x:]
    h2 = _stream_tile(refs[:n_x], MOE_TM) * (1.0 + mod_ref[4:5, :]) + mod_ref[3:4, :]
    def split(v):
        hi = v.astype(BF16)
        return hi, (v - hi.astype(F32)).astype(BF16)

    def nt_dot(a, b):
        return lax.dot_general(a, b, (((1,), (1,)), ((), ())), preferred_element_type=F32)

    (w_hi, w_lo), (h_hi, h_lo) = split(rwt_ref[...]), split(h2)
    logits = (nt_dot(w_hi, h_hi) + nt_dot(w_hi, h_lo) + nt_dot(w_lo, h_hi)) + rbc_ref[...]
    sub = lax.broadcasted_iota(I32, (N_EXPERTS, MOE_TM), 0)
    vals, hots = [], []
    cur = logits
    for _ in range(TOP_K):
        m = jnp.max(cur, axis=0, keepdims=True)
        idx = jnp.min(jnp.where(cur == m, sub, N_EXPERTS), axis=0, keepdims=True)
        hot = sub == idx
        vals.append(m)
        hots.append(hot)
        cur = jnp.where(hot, -jnp.inf, cur)
    exps = [jnp.exp(v - vals[0]) for v in vals]
    den = (exps[0] + exps[1]) + (exps[2] + exps[3])
    mask = jnp.zeros((N_EXPERTS, MOE_TM), F32)
    for hot in hots:
        mask = mask + jnp.where(hot, 1.0, 0.0)
    before = jnp.dot(mask.astype(BF16), upper_ref[...], preferred_element_type=F32)
    n = jnp.sum(mask, axis=1, keepdims=True)
    p = (((n.astype(I32) + (CHUNK - 1)) >> CHUNK_SHIFT) << CHUNK_SHIFT).astype(F32)
    start = jnp.dot(lower_ref[...], jnp.broadcast_to(p, (N_EXPERTS, LANES)), precision=HIGHEST,
                    preferred_element_type=F32)[:, :1]
    base = start + before
    lps = [jnp.sum(jnp.where(hot, base, 0.0), axis=0, keepdims=True) for hot in hots]
    route_ref[...] = jnp.concatenate(lps + [e / den for e in exps], axis=0)
    lane = lax.broadcasted_iota(I32, (N_EXPERTS, LANES), 1)
    tab_ref[...] = jnp.where(lane == 0, n, jnp.where(lane == 1, p, jnp.where(lane == 2, start, 0.0)))
    jrow = lax.broadcasted_iota(I32, (TILE_ROWS, MOE_TM), 0)
    onehot = jnp.zeros((TILE_ROWS, MOE_TM), F32)
    for lp in lps:
        onehot = jnp.where(jrow == lp.astype(I32), 1.0, onehot)
    xs = jnp.dot(onehot.astype(BF16), h_hi, preferred_element_type=F32)
    xsb_ref[...] = pltpu.pack_elementwise([xs[:, :PACKED_W], xs[:, PACKED_W:]], packed_dtype=BF16)


def _dispatch(x1, mod, l, rw, rb):
    upper = jnp.asarray(np.triu(np.ones((MOE_TM, MOE_TM)), 1), BF16)
    lower = jnp.asarray(np.tril(np.ones((N_EXPERTS, N_EXPERTS)), -1), F32)
    x_specs, x_args = _stream_specs(x1, MOE_TM)
    return pl.pallas_call(
        functools.partial(_dispatch_kernel, n_x=len(x_args)),
        grid=(N_TILES,),
        in_specs=x_specs + [_mod_spec(l, MOE_TM), _full((N_EXPERTS, D_MODEL)), _full((N_EXPERTS, 1)),
                            _full((MOE_TM, MOE_TM)), _full((N_EXPERTS, N_EXPERTS))],
        out_specs=[pl.BlockSpec((None, TILE_ROWS, PACKED_W), lambda i: (i, 0, 0)),
                   pl.BlockSpec((None, 2 * TOP_K, MOE_TM), lambda i: (i, 0, 0)),
                   pl.BlockSpec((None, N_EXPERTS, LANES), lambda i: (i, 0, 0))],
        out_shape=[jax.ShapeDtypeStruct((N_TILES, TILE_ROWS, PACKED_W), I32),
                   jax.ShapeDtypeStruct((N_TILES, 2 * TOP_K, MOE_TM), F32),
                   jax.ShapeDtypeStruct((N_TILES, N_EXPERTS, LANES), F32)],
        compiler_params=_cparams(("arbitrary",)),
        name="moe_dispatch",
    )(*x_args, mod, rw.T, rb.reshape(N_EXPERTS, 1), upper, lower)


def _plan_kernel(p_ref, s_ref, zbig_hbm, zsmall_hbm, iexp_ref, ib0_ref, ib1_ref, irows_ref, ibig_ref, ismall_ref,
                 big_ref, small_ref, cnt_ref, sem):
    clears = [pltpu.make_async_copy(zbig_hbm, big_ref, sem), pltpu.make_async_copy(zsmall_hbm, small_ref, sem)]
    for cp in clears:
        cp.start()
    for cp in clears:
        cp.wait()
    cnt_ref[0] = 0

    def emit(e, b0, b1, rows):
        it = cnt_ref[0]
        iexp_ref[it] = e
        ib0_ref[it] = b0
        ib1_ref[it] = b1
        irows_ref[it] = rows
        cnt_ref[0] = it + 1

    def per_expert(e, last_e):
        def per_tile(b, carry):
            b0, rows = carry
            pb = p_ref[b * N_EXPERTS + e]
            full = rows + pb > ITEM_ROWS

            @pl.when(full)
            def _():
                emit(e, b0, b, rows)

            return jnp.where(full, b, b0), jnp.where(full, pb, rows + pb)

        b0, rows = lax.fori_loop(0, N_TILES, per_tile, (jnp.int32(0), jnp.int32(0)))

        @pl.when(rows > 0)
        def _():
            emit(e, b0, jnp.int32(N_TILES), rows)

        return jnp.where(rows > 0, e, last_e)

    last_e = lax.fori_loop(0, N_EXPERTS, per_expert, jnp.int32(0))

    def idle(it, c):
        iexp_ref[it] = last_e
        ib0_ref[it] = 0
        ib1_ref[it] = 0
        irows_ref[it] = 0
        return c

    lax.fori_loop(cnt_ref[0], ITEM_TABLE, idle, 0)

    def per_item(it, carry):
        n_big, n_small = carry
        ibig_ref[it] = n_big
        ismall_ref[it] = n_small
        e = iexp_ref[it]

        def per_tile(b, c):
            dst, n_big, n_small = c
            pb = p_ref[b * N_EXPERTS + e]
            word = (b << 22) | (s_ref[b * N_EXPERTS + e] << 11) | dst
            nb = pb >> BIG_SHIFT
            ns = (pb - nb * BIG) >> CHUNK_SHIFT
            for k in range(MOE_TM // BIG):
                big_ref[n_big + k] = word + k * (BIG << 11 | BIG)
            for k in range(BIG // CHUNK - 1):
                small_ref[n_small + k] = word + nb * (BIG << 11 | BIG) + k * (CHUNK << 11 | CHUNK)
            return dst + pb, n_big + nb, n_small + ns

        _, n_big, n_small = lax.fori_loop(ib0_ref[it], ib1_ref[it], per_tile, (jnp.int32(0), n_big, n_small))
        return n_big, n_small

    lax.fori_loop(0, ITEM_TABLE, per_item, (jnp.int32(0), jnp.int32(0)))


def _plan(ptab, stab):
    smem = pl.BlockSpec(memory_space=pltpu.SMEM)
    table = jax.ShapeDtypeStruct((ITEM_TABLE,), I32)
    return pl.pallas_call(
        _plan_kernel,
        in_specs=[smem, smem, pl.BlockSpec(memory_space=pl.ANY), pl.BlockSpec(memory_space=pl.ANY)],
        out_specs=[smem] * 8,
        out_shape=[table] * 6 + [jax.ShapeDtypeStruct((BIG_LIST,), I32), jax.ShapeDtypeStruct((SMALL_LIST,), I32)],
        scratch_shapes=[pltpu.SMEM((1,), I32), pltpu.SemaphoreType.DMA(())],
        name="moe_plan",
    )(ptab, stab, jnp.zeros((BIG_LIST,), I32), jnp.zeros((SMALL_LIST,), I32))


def _unpack_halves(p):
    lo = pltpu.unpack_elementwise(p, index=0, packed_dtype=BF16, unpacked_dtype=F32)
    hi = pltpu.unpack_elementwise(p, index=1, packed_dtype=BF16, unpacked_dtype=F32)
    return lo.astype(BF16), hi.astype(BF16)


def _unpack_rows(p):
    return jnp.concatenate(_unpack_halves(p), axis=1)


def _pack_rows(y):
    return pltpu.pack_elementwise([y[:, :PACKED_W], y[:, PACKED_W:]], packed_dtype=BF16)


def _expert_kernel(iexp_ref, irows_ref, ibig_ref, ismall_ref, big_ref, small_ref, xsb_hbm, w1_hbm, w2_hbm,
                   b1_ref, b2_ref, ysb_hbm, xg_ref, g_ref, a_ref, yb_ref, w_ref, cnt_ref,
                   in_big, in_small, out_big, out_small, w_sem, *, layer):
    it = pl.program_id(0)
    rows = irows_ref[it]
    slot = it & 1

    def weight_copies(item, phase):
        ee = iexp_ref[item]
        copies = []
        for k in range(2):
            half = pl.ds(k * (D_MODEL // 2), D_MODEL // 2)
            if phase < 2:
                src = w1_hbm.at[layer, ee, half, pl.ds(phase * D_FF, D_FF)]
            else:
                src = w2_hbm.at[layer, ee, half, :]
            copies.append(pltpu.make_async_copy(src, w_ref.at[phase, half, :], w_sem.at[phase]))
        return copies

    def start_weights(item, phase):
        @pl.when(irows_ref[item] > 0)
        def _():
            for cp in weight_copies(item, phase):
                cp.start()

    def wait_weights(phase):
        for cp in weight_copies(it, phase):
            cp.wait()

    def aligned(v):
        return v if isinstance(v, int) else pl.multiple_of(v, CHUNK)

    def copy(to_vmem, buf, b, src, dst, size):
        hbm = (xsb_hbm if to_vmem else ysb_hbm).at[b, pl.ds(aligned(src), size), :]
        if to_vmem:
            sem = in_big if size == BIG else in_small
            return pltpu.make_async_copy(hbm, xg_ref.at[buf, pl.ds(aligned(dst), size), :], sem)
        sem = out_big if size == BIG else out_small
        return pltpu.make_async_copy(yb_ref.at[pl.ds(aligned(dst), size), :], hbm, sem)

    def start_segments(item, to_vmem, buf):
        def start_list(lst_ref, lo, hi, size):
            def body(i, z):
                word = lst_ref[i]
                copy(to_vmem, buf, word >> 22, (word >> 11) & 2047, word & 2047, size).start()
                return z

            lax.fori_loop(lo, hi, body, 0)
            return hi - lo

        return (start_list(big_ref, ibig_ref[item], ibig_ref[item + 1], BIG),
                start_list(small_ref, ismall_ref[item], ismall_ref[item + 1], CHUNK))

    def wait_segments(to_vmem, n_big, n_small):
        def wait_big(c, z):
            copy(to_vmem, 0, 0, 0, 0, BIG).wait()
            return z

        def wait_small(c, z):
            copy(to_vmem, 0, 0, 0, 0, CHUNK).wait()
            return z

        lax.fori_loop(0, n_big, wait_big, 0)
        lax.fori_loop(0, n_small, wait_small, 0)

    @pl.when(it == 0)
    def _init():
        xg_ref[...] = jnp.zeros_like(xg_ref)
        start_weights(0, 0)
        start_weights(0, 1)
        cnt_ref[0], cnt_ref[1] = start_segments(0, True, 0)
        cnt_ref[2] = 0
        cnt_ref[3] = 0

    def for_sub_tiles(fn):
        n_full = rows >> SUB_SHIFT
        rem_parts = (rows - (n_full << SUB_SHIFT) + (PART - 1)) >> PART_SHIFT

        def body(i, c):
            fn(pl.ds(pl.multiple_of(i * SUB, SUB), SUB))
            return c

        lax.fori_loop(0, n_full, body, 0)
        r0 = pl.multiple_of(n_full * SUB, SUB)
        for k in range(1, SUB // PART + 1):
            @pl.when(rem_parts == k)
            def _(k=k):
                fn(pl.ds(r0, k * PART))

    def matmul(lo, hi, phase):
        half = D_MODEL // 2
        return (jnp.dot(lo, w_ref[phase, :half, :].astype(BF16), preferred_element_type=F32)
                + jnp.dot(hi, w_ref[phase, half:, :].astype(BF16), preferred_element_type=F32))

    @pl.when(rows > 0)
    def _item():
        start_weights(it, 2)
        wait_segments(True, cnt_ref[0], cnt_ref[1])
        wait_weights(0)

        def gate(r):
            lo, hi = _unpack_halves(xg_ref[slot, r, :])
            gate = jnp.minimum(matmul(lo, hi, 0) + b1_ref[:, :D_FF], SWIGLU_LIMIT)
            g_ref[r, :] = gate * (0.5 + 0.5 * jnp.tanh((0.5 * SWIGLU_ALPHA) * gate))

        for_sub_tiles(gate)

        start_weights(it + 1, 0)
        cnt_ref[0], cnt_ref[1] = start_segments(it + 1, True, 1 - slot)
        wait_weights(1)

        def up_act(r):
            lo, hi = _unpack_halves(xg_ref[slot, r, :])
            up = matmul(lo, hi, 1) + b1_ref[:, D_FF:]
            up = jnp.clip(up, -SWIGLU_LIMIT, SWIGLU_LIMIT)
            a_ref[r, :] = (g_ref[r, :] * (up + 1.0)).astype(BF16)

        for_sub_tiles(up_act)

        start_weights(it + 1, 1)
        wait_weights(2)
        wait_segments(False, cnt_ref[2], cnt_ref[3])

        def down(r):
            y = matmul(a_ref[r, :D_FF // 2], a_ref[r, D_FF // 2:], 2) + b2_ref[...]
            yb_ref[r, :] = _pack_rows(y)

        for_sub_tiles(down)
        cnt_ref[2], cnt_ref[3] = start_segments(it, False, 0)

    @pl.when(it == MAX_ITEMS - 1)
    def _drain():
        wait_segments(False, cnt_ref[2], cnt_ref[3])


def _experts(iexp, irows, ibig, ismall, big, small, xsb, l, w1, b1, w2, b2):
    n_prefetch = 6
    any_spec = pl.BlockSpec(memory_space=pl.ANY)
    return pl.pallas_call(
        functools.partial(_expert_kernel, layer=l),
        grid_spec=pltpu.PrefetchScalarGridSpec(
            num_scalar_prefetch=n_prefetch,
            grid=(MAX_ITEMS,),
            in_specs=[any_spec, any_spec, any_spec,
                      pl.BlockSpec((None, None, 1, 2 * D_FF), lambda it, ie, *_: (l, ie[it], 0, 0)),
                      pl.BlockSpec((None, None, 1, D_MODEL), lambda it, ie, *_: (l, ie[it], 0, 0))],
            out_specs=pl.BlockSpec(memory_space=pl.ANY),
            scratch_shapes=[pltpu.VMEM((2, ITEM_ROWS, PACKED_W), I32),
                            pltpu.VMEM((ITEM_ROWS, D_FF), F32),
                            pltpu.VMEM((ITEM_ROWS, D_FF), BF16),
                            pltpu.VMEM((ITEM_ROWS, PACKED_W), I32),
                            pltpu.VMEM((3, D_MODEL, D_FF), F32),
                            pltpu.SMEM((4,), I32),
                            pltpu.SemaphoreType.DMA(()),
                            pltpu.SemaphoreType.DMA(()),
                            pltpu.SemaphoreType.DMA(()),
                            pltpu.SemaphoreType.DMA(()),
                            pltpu.SemaphoreType.DMA((3,))],
        ),
        out_shape=jax.ShapeDtypeStruct((N_TILES, TILE_ROWS, PACKED_W), I32),
        input_output_aliases={n_prefetch: 0},
        compiler_params=_cparams(("arbitrary",)),
        name="moe_experts",
    )(iexp, irows, ibig, ismall, big, small, xsb, w1, w2,
      b1.reshape(DEPTH, N_EXPERTS, 1, 2 * D_FF), b2.reshape(DEPTH, N_EXPERTS, 1, D_MODEL))


def _combine_kernel(*refs, n_x, n_out):
    ysb_ref, route_ref, eye_ref, mod_ref, g_ref, b_ref = refs[n_x:len(refs) - n_out]
    o_refs = refs[len(refs) - n_out:]
    rt = lax.dot_general(eye_ref[...], route_ref[...], (((1,), (1,)), ((), ())), precision=HIGHEST,
                         preferred_element_type=F32)
    lane = lax.broadcasted_iota(I32, (MOE_TM, TILE_ROWS), 1)
    c = jnp.zeros((MOE_TM, TILE_ROWS), F32)
    for k in range(TOP_K):
        c = jnp.where(lane == rt[:, k:k + 1].astype(I32), rt[:, TOP_K + k:TOP_K + k + 1], c)
    f = jnp.dot(c.astype(BF16), _unpack_rows(ysb_ref[...]), preferred_element_type=F32)
    x = _stream_tile(refs[:n_x], MOE_TM)
    out = _layer_norm(DEEPNORM_ALPHA * x + mod_ref[5:6, :] * f, g_ref[...], b_ref[...])
    if n_out == 1:
        o_refs[0][...] = out
    else:
        is_ctx = pl.program_id(0) < N_PROMPT // MOE_TM

        @pl.when(is_ctx)
        def _():
            o_refs[0][...] = out

        @pl.when(jnp.logical_not(is_ctx))
        def _():
            o_refs[1][...] = out


def _combine(x1, ysb, route, mod, l, g, b, split_out):
    x_specs, x_args = _stream_specs(x1, MOE_TM)
    if split_out:
        out_specs, _ = _stream_specs((None, None), MOE_TM)
        out_shape = [jax.ShapeDtypeStruct((N_PROMPT, D_MODEL), F32), jax.ShapeDtypeStruct((N_SAMPLE, D_MODEL), F32)]
    else:
        out_specs, _ = _stream_specs(None, MOE_TM)
        out_shape = [jax.ShapeDtypeStruct((N_TOK, D_MODEL), F32)]
    return pl.pallas_call(
        functools.partial(_combine_kernel, n_x=len(x_args), n_out=len(out_shape)),
        grid=(N_TILES,),
        in_specs=x_specs + [pl.BlockSpec((None, TILE_ROWS, PACKED_W), lambda i: (i, 0, 0)),
                            pl.BlockSpec((None, 2 * TOP_K, MOE_TM), lambda i: (i, 0, 0)),
                            _full((MOE_TM, MOE_TM)), _mod_spec(l, MOE_TM), _full((1, D_MODEL)),
                            _full((1, D_MODEL))],
        out_specs=out_specs,
        out_shape=out_shape,
        compiler_params=_cparams(("arbitrary",)),
        name="moe_combine",
    )(*x_args, ysb, route, jnp.eye(MOE_TM, dtype=F32), mod, g.reshape(1, D_MODEL), b.reshape(1, D_MODEL))


def _moe_layer(x1, mod, l, rw, rb, w1, b1, w2, b2, ln_g, ln_b, split_out=False):
    xsb, route, tab = _dispatch(x1, mod, l, rw, rb)
    ptab = tab[:, :, 1].astype(I32).reshape(N_SEG)
    stab = tab[:, :, 2].astype(I32).reshape(N_SEG)
    iexp, _, _, irows, ibig, ismall, big, small = _plan(ptab, stab)
    ysb = _experts(iexp, irows, ibig, ismall, big, small, xsb, l, w1, b1, w2, b2)
    return _combine(x1, ysb, route, mod, l, ln_g, ln_b, split_out)


def kernel(x_prompt, x_sample, cache_k_attn, cache_v_attn, cache_k_na, cache_v_na, c, c_ctx, w_mod, b_mod, ln1_g, ln1_b, ln2_g, ln2_b, conv_w_in, conv_w, conv_b, conv_w_out, attn_w_qkv, attn_q_norm, attn_k_norm, attn_w_o, na_w_qkv, na_rpb, na_w_o, router_w, router_b, moe_w1, moe_b1, moe_w2, moe_b2):
    x = (x_prompt.reshape(N_PROMPT, D_MODEL), x_sample.reshape(N_SAMPLE, D_MODEL))
    cond8 = jnp.concatenate([c_ctx[None, :], c, jnp.zeros((8 - 1 - DEC_BATCH, D_MODEL), F32)], axis=0)
    mod = _adaln_all(cond8, w_mod, b_mod)
    new_kv = {}
    for l in range(DEPTH):
        kind, j = l % 3, l // 3
        if kind == 0:
            x1 = _conv_layer(x, mod, l, conv_w_in[j], conv_w[j], conv_b[j], conv_w_out[j], ln1_g[l], ln1_b[l])
        elif kind == 1:
            gq = jnp.tile(attn_q_norm[j], N_HEADS).reshape(1, N_HEADS * HEAD_DIM)
            gk = jnp.tile(attn_k_norm[j], N_KV_HEADS).reshape(1, N_KV_HEADS * HEAD_DIM)
            norm_args = (gq, gk, _block_diag_mean())
            wk = N_KV_HEADS * HEAD_DIM
            xp1, nk, nv = _attn_call(x, mod, l, False, attn_w_qkv[j], attn_w_o[j], ln1_g[l], ln1_b[l],
                                     n_seq=BATCH, seq_len=SEQ, tq=SEQ, n_kv=N_KV_HEADS, norm_args=norm_args,
                                     emit_kv=True, name="gqa_prompt")
            new_kv["k_attn"] = nk.reshape(BATCH, 1, SEQ, N_KV_HEADS, HEAD_DIM)
            new_kv["v_attn"] = nv.reshape(BATCH, 1, SEQ, N_KV_HEADS, HEAD_DIM)
            (xs1,) = _attn_call(x, mod, l, True, attn_w_qkv[j], attn_w_o[j], ln1_g[l], ln1_b[l],
                                n_seq=DEC_BATCH, seq_len=DEC_SEQ, tq=256, n_kv=N_KV_HEADS, norm_args=norm_args,
                                rope_args=_rope_tables(),
                                ctx_args=(cache_k_attn[:, j].reshape(DEC_BATCH, PAST_LEN, wk),
                                          cache_v_attn[:, j].reshape(DEC_BATCH, PAST_LEN, wk)),
                                name="gqa_sample")
            x1 = (xp1, xs1)
        else:
            wk = N_HEADS * HEAD_DIM
            xp1, nk, nv = _attn_call(x, mod, l, False, na_w_qkv[j], na_w_o[j], ln1_g[l], ln1_b[l],
                                     n_seq=BATCH, seq_len=SEQ, tq=SEQ, n_kv=N_HEADS, emit_kv=True,
                                     name="mha_prompt")
            new_kv["k_na"] = nk.reshape(BATCH, 1, SEQ, N_HEADS, HEAD_DIM)
            new_kv["v_na"] = nv.reshape(BATCH, 1, SEQ, N_HEADS, HEAD_DIM)
            (xs1,) = _attn_call(x, mod, l, True, na_w_qkv[j], na_w_o[j], ln1_g[l], ln1_b[l],
                                n_seq=DEC_BATCH, seq_len=DEC_SEQ, tq=GRID_W, n_kv=N_HEADS,
                                ctx_args=(cache_k_na[:, j].reshape(DEC_BATCH, PAST_LEN, wk),
                                          cache_v_na[:, j].reshape(DEC_BATCH, PAST_LEN, wk)),
                                bias=_na_bias_table(na_rpb[j]), name="na_sample")
            x1 = (xp1, xs1)
        out = _moe_layer(x1, mod, l, router_w[l], router_b[l], moe_w1, moe_b1, moe_w2, moe_b2,
                         ln2_g[l], ln2_b[l], split_out=l == DEPTH - 1)
        x = out[0]
    y_prompt = out[0].reshape(BATCH, SEQ, D_MODEL)
    y_sample = out[1].reshape(DEC_BATCH, DEC_SEQ, D_MODEL)
    return (y_prompt, y_sample, new_kv["k_attn"], new_kv["v_attn"], new_kv["k_na"], new_kv["v_na"])
```

```python
import functools

import numpy as np
import jax
import jax.numpy as jnp
from jax import lax
from jax.experimental import pallas as pl
from jax.experimental.pallas import tpu as pltpu

D_MODEL = 1024
BATCH = 16
SEQ = 256
DEPTH = 4
DEC_BATCH = 2
DEC_SEQ = 1024
PAST_LEN = 256
GRID_W = 64
HEAD_DIM = 64
N_HEADS = 16
N_KV_HEADS = 4
ROPE_THETA = 10000.0
ROPE_PAIRS = HEAD_DIM // 4
WIN_R = 8
WIN_C = 16
N_EXPERTS = 32
TOP_K = 4
D_FF = D_MODEL
SWIGLU_LIMIT = 7.0
SWIGLU_ALPHA = 1.702
DEEPNORM_ALPHA = (2 * DEPTH) ** 0.25
LN_EPS = 1e-5
RMS_EPS = 1e-6

F32 = jnp.float32
BF16 = jnp.bfloat16
I32 = jnp.int32
HIGHEST = lax.Precision.HIGHEST

N_PROMPT = BATCH * SEQ
N_SAMPLE = DEC_BATCH * DEC_SEQ
N_TOK = N_PROMPT + N_SAMPLE
N_ASSIGN = N_TOK * TOP_K

LANES = 128
SUBLANES = 8
PACKED_W = D_MODEL // 2

VMEM_LIMIT = 58 * 1024 * 1024


def _cparams(sem):
    return pltpu.CompilerParams(dimension_semantics=sem, vmem_limit_bytes=VMEM_LIMIT)


def _layer_norm(x, g, b):
    mu = jnp.mean(x, -1, keepdims=True)
    xc = x - mu
    var = jnp.mean(xc * xc, -1, keepdims=True)
    return xc * lax.rsqrt(var + LN_EPS) * g + b


def _mod_row(i, tile):
    n_prompt_tiles = N_PROMPT // tile
    return jnp.where(i < n_prompt_tiles, 0, 1 + (i - n_prompt_tiles) // (DEC_SEQ // tile))


def _mod_spec(l, tile):
    return pl.BlockSpec((None, None, 6, D_MODEL), lambda i, *_: (l, _mod_row(i, tile), 0, 0))


def _full(shape):
    nd = len(shape)
    return pl.BlockSpec(shape, lambda *_: (0,) * nd)


def _stream_specs(x, tile):
    if not isinstance(x, tuple):
        return [pl.BlockSpec((tile, D_MODEL), lambda i, *_: (i, 0))], [x]
    n_a = N_PROMPT // tile
    return ([pl.BlockSpec((tile, D_MODEL), lambda i, *_: (jnp.minimum(i, n_a - 1), 0)),
             pl.BlockSpec((tile, D_MODEL), lambda i, *_: (jnp.maximum(i - n_a, 0), 0))], list(x))


def _stream_tile(x_refs, tile):
    if len(x_refs) == 1:
        return x_refs[0][...]
    return jnp.where(pl.program_id(0) < N_PROMPT // tile, x_refs[0][...], x_refs[1][...])


ADALN_TN = 1536


N_COND = 1 + DEC_BATCH
ADALN_UNROLL = 4


def _adaln_kernel(ct_ref, w_ref, b_ref, o_ref, s_ref):
    @pl.when((pl.program_id(0) == 0) & (pl.program_id(1) == 0))
    def _silu():
        c = ct_ref[...]
        s_ref[...] = c * jax.nn.sigmoid(c)

    def body(j, accs):
        accs = list(accs)
        for u in range(ADALN_UNROLL):
            rows = pl.ds(pl.multiple_of((j * ADALN_UNROLL + u) * SUBLANES, SUBLANES), SUBLANES)
            w = w_ref[rows, :]
            for r in range(N_COND):
                accs[r] = accs[r] + w * jnp.concatenate([s_ref[r, rows, :]] * (ADALN_TN // LANES), axis=1)
        return tuple(accs)

    zero = jnp.zeros((SUBLANES, ADALN_TN), F32)
    accs = lax.fori_loop(0, D_MODEL // (SUBLANES * ADALN_UNROLL), body, (zero,) * N_COND)
    rows = [jnp.sum(a, axis=0, keepdims=True) for a in accs]
    o_ref[...] = jnp.concatenate(rows + [jnp.zeros((8 - N_COND, ADALN_TN), F32)], axis=0) + b_ref[...]


def _adaln_all(cond, w_mod, b_mod):
    n = 6 * D_MODEL
    cond_t = jnp.broadcast_to(cond[:, :, None], (N_COND, D_MODEL, LANES))
    out = pl.pallas_call(
        _adaln_kernel,
        grid=(DEPTH, n // ADALN_TN),
        in_specs=[
            pl.BlockSpec((N_COND, D_MODEL, LANES), lambda l, j: (0, 0, 0)),
            pl.BlockSpec((None, D_MODEL, ADALN_TN), lambda l, j: (l, 0, j)),
            pl.BlockSpec((None, 1, ADALN_TN), lambda l, j: (l, 0, j)),
        ],
        out_specs=pl.BlockSpec((None, 8, ADALN_TN), lambda l, j: (l, 0, j)),
        out_shape=jax.ShapeDtypeStruct((DEPTH, 8, n), F32),
        scratch_shapes=[pltpu.VMEM((N_COND, D_MODEL, LANES), F32)],
        compiler_params=_cparams(("arbitrary", "arbitrary")),
        name="adaln",
    )(cond_t, w_mod, b_mod.reshape(DEPTH, 1, n))
    return out.reshape(DEPTH, 8, 6, D_MODEL)


CONV_TM = 1024


def _conv_kernel(*refs, n_x):
    mod_ref, win_ref, cw_ref, cb_ref, wout_ref, g_ref, b_ref, o_ref = refs[n_x:]
    i = pl.program_id(0)
    x = _stream_tile(refs[:n_x], CONV_TM)
    h = (x * (1.0 + mod_ref[1:2, :]) + mod_ref[0:1, :]).astype(BF16)
    gc = jnp.dot(h, win_ref[:, D_MODEL:2 * D_MODEL], preferred_element_type=F32)
    xv = jnp.dot(h, win_ref[:, 2 * D_MODEL:], preferred_element_type=F32)
    u = gc * xv
    seq_len = jnp.where(i < N_PROMPT // CONV_TM, SEQ, DEC_SEQ)
    t = lax.broadcasted_iota(I32, (CONV_TM, 1), 0) & (seq_len - 1)
    u_prev = jnp.where(t == 0, 0.0, pltpu.roll(u, 1, axis=0))
    u_next = jnp.where(t == seq_len - 1, 0.0, pltpu.roll(u, CONV_TM - 1, axis=0))
    y = u_prev * cw_ref[0:1, :] + u * cw_ref[1:2, :] + u_next * cw_ref[2:3, :] + cb_ref[...]
    gb = jnp.dot(h, win_ref[:, :D_MODEL], preferred_element_type=F32)
    v = (gb * y).astype(BF16)
    o = jnp.dot(v, wout_ref[...], preferred_element_type=F32)
    o_ref[...] = _layer_norm(DEEPNORM_ALPHA * x + mod_ref[2:3, :] * o, g_ref[...], b_ref[...])


def _conv_layer(x, mod, l, w_in, cw, cb, w_out, ln_g, ln_b):
    x_specs, x_args = _stream_specs(x, CONV_TM)
    return pl.pallas_call(
        functools.partial(_conv_kernel, n_x=len(x_args)),
        grid=(N_TOK // CONV_TM,),
        in_specs=x_specs + [_mod_spec(l, CONV_TM), _full((D_MODEL, 3 * D_MODEL)), _full((3, D_MODEL)),
                            _full((1, D_MODEL)), _full((D_MODEL, D_MODEL)), _full((1, D_MODEL)),
                            _full((1, D_MODEL))],
        out_specs=pl.BlockSpec((CONV_TM, D_MODEL), lambda i: (i, 0)),
        out_shape=jax.ShapeDtypeStruct((N_TOK, D_MODEL), F32),
        compiler_params=_cparams(("arbitrary",)),
        name="conv_mixer",
    )(*x_args, mod, w_in.astype(BF16), cw, cb.reshape(1, D_MODEL), w_out.astype(BF16),
      ln_g.reshape(1, D_MODEL), ln_b.reshape(1, D_MODEL))


ATTN_CHUNK = 256
NA_SLOTS = WIN_R + 2


def _attn_kernel(*refs, seq_len, tq, n_kv, norm, rope, n_ctx, na, emit_kv):
    refs = list(refs)
    x_ref, mod_ref, wqkv_ref = refs[:3]
    pos = 3
    if norm:
        gq_ref, gk_ref, bd_ref = refs[pos:pos + 3]
        pos += 3
    if rope:
        cos_ref, s1_ref, s2_ref = refs[pos:pos + 3]
        pos += 3
    if n_ctx:
        ck_ref, cv_ref = refs[pos:pos + 2]
        pos += 2
    if na:
        bias_ref = refs[pos]
        pos += 1
    wo_ref, lng_ref, lnb_ref = refs[pos:pos + 3]
    pos += 3
    o_ref = refs[pos]
    pos += 1
    if emit_kv:
        nk_ref, nv_ref = refs[pos:pos + 2]
        pos += 2
    q_scr, k_scr, v_scr, o_scr = refs[pos:pos + 4]

    qt = pl.program_id(1)
    n_qt = seq_len // tq
    wq = N_HEADS * HEAD_DIM
    wk = n_kv * HEAD_DIM
    rep = N_HEADS // n_kv

    def rms(v, g_ref, width):
        ms = jnp.dot((v * v).astype(BF16), bd_ref[:width, :width], preferred_element_type=F32)
        return v * lax.rsqrt(ms + RMS_EPS) * g_ref[...]

    def rot(v, rows, width):
        def tab(ref):
            t = ref[rows, :]
            return jnp.concatenate([t] * (width // LANES), axis=1)

        return (v * tab(cos_ref) + pltpu.roll(v, width - ROPE_PAIRS, axis=1) * tab(s1_ref)
                + pltpu.roll(v, ROPE_PAIRS, axis=1) * tab(s2_ref))

    chunk = min(seq_len, ATTN_CHUNK)

    @pl.when(qt == 0)
    def _project():
        def body(ci, carry):
            r0 = pl.multiple_of(ci * chunk, chunk)
            rows = pl.ds(r0, chunk)
            h = (x_ref[rows, :] * (1.0 + mod_ref[1:2, :]) + mod_ref[0:1, :]).astype(BF16)
            q = jnp.dot(h, wqkv_ref[:, :wq], preferred_element_type=F32)
            k = jnp.dot(h, wqkv_ref[:, wq:wq + wk], preferred_element_type=F32)
            v = jnp.dot(h, wqkv_ref[:, wq + wk:], preferred_element_type=F32)
            if norm:
                q = rms(q, gq_ref, wq)
                k = rms(k, gk_ref, wk)
            if emit_kv:
                nk_ref[rows, :] = k
                nv_ref[rows, :] = v
            if rope:
                q = rot(q, rows, wq)
                k = rot(k, rows, wk)
            q_scr[rows, :] = (q * (HEAD_DIM ** -0.5)).astype(BF16)
            k_scr[pl.ds(n_ctx + r0, chunk), :] = k.astype(BF16)
            v_scr[pl.ds(n_ctx + r0, chunk), :] = v.astype(BF16)
            return carry

        lax.fori_loop(0, seq_len // chunk, body, 0)
        if n_ctx:
            k_scr[:n_ctx, :] = ck_ref[...].astype(BF16)
            v_scr[:n_ctx, :] = cv_ref[...].astype(BF16)
        if na:
            k_scr[n_ctx + seq_len:, :] = jnp.zeros((GRID_W, wk), BF16)
            v_scr[n_ctx + seq_len:, :] = jnp.zeros((GRID_W, wk), BF16)

    q0 = pl.multiple_of(qt * tq, tq)
    if na:
        n_rows = DEC_SEQ // GRID_W
        w = jnp.minimum(jnp.clip(2 * qt - WIN_R // 2, 0, n_rows - WIN_R), n_rows - WIN_R - 1)
        k0 = pl.multiple_of(n_ctx + w * GRID_W, GRID_W)
        slot_lane = lax.broadcasted_iota(I32, (GRID_W, NA_SLOTS * GRID_W), 1)
        pair_base, invalid = [], []
        for j in range(2):
            r = 2 * qt + j
            first = jnp.clip(r - WIN_R // 2, 0, n_rows - WIN_R) - w
            ok = (slot_lane >= first * GRID_W) & (slot_lane < (first + WIN_R) * GRID_W)
            invalid.append(jnp.where(ok, 0.0, -1e30))
            pair_base.append(w - r + WIN_R)
    for hd in range(N_HEADS):
        g = hd // rep
        hs = slice(hd * HEAD_DIM, (hd + 1) * HEAD_DIM)
        gs = slice(g * HEAD_DIM, (g + 1) * HEAD_DIM)
        qh = q_scr[pl.ds(q0, tq), hs]
        dn = (((1,), (1,)), ((), ()))
        if na:
            kc, vc = k_scr[:n_ctx, gs], v_scr[:n_ctx, gs]
            kl, vl = k_scr[pl.ds(k0, NA_SLOTS * GRID_W), gs], v_scr[pl.ds(k0, NA_SLOTS * GRID_W), gs]
            bias = jnp.concatenate(
                [jnp.concatenate([bias_ref[hd, pl.ds(pair_base[j] + 2 * i, 1)][0] for i in range(NA_SLOTS // 2)],
                                 axis=1) + invalid[j] for j in range(2)], axis=0)
            s = jnp.concatenate([lax.dot_general(qh, kc, dn, preferred_element_type=F32),
                                 lax.dot_general(qh, kl, dn, preferred_element_type=F32) + bias], axis=1)
        else:
            s = lax.dot_general(qh, k_scr[:, gs], dn, preferred_element_type=F32)
        e = jnp.exp(s - jnp.max(s, axis=1, keepdims=True))
        den = jnp.sum(e, axis=1, keepdims=True)
        eb = e.astype(BF16)
        if na:
            oh = (jnp.dot(eb[:, :n_ctx], vc, preferred_element_type=F32)
                  + jnp.dot(eb[:, n_ctx:], vl, preferred_element_type=F32))
        else:
            oh = jnp.dot(eb, v_scr[:, gs], preferred_element_type=F32)
        o_scr[pl.ds(q0, tq), hs] = (oh / den).astype(BF16)

    @pl.when(qt == n_qt - 1)
    def _finish():
        def body(ci, carry):
            rows = pl.ds(pl.multiple_of(ci * chunk, chunk), chunk)
            o = jnp.dot(o_scr[rows, :], wo_ref[...], preferred_element_type=F32)
            o_ref[rows, :] = _layer_norm(DEEPNORM_ALPHA * x_ref[rows, :] + mod_ref[2:3, :] * o,
                                         lng_ref[...], lnb_ref[...])
            return carry

        lax.fori_loop(0, seq_len // chunk, body, 0)


def _attn_call(x, mod, l, latent, w_qkv, w_o, ln_g, ln_b, *, n_seq, seq_len, tq, n_kv, norm_args=None,
               rope_args=None, ctx_args=None, bias=None, emit_kv=False, name="attn"):
    wq = N_HEADS * HEAD_DIM
    wk = n_kv * HEAD_DIM
    n_ctx = PAST_LEN if ctx_args is not None else 0
    n_qt = seq_len // tq
    seq_off = N_PROMPT // seq_len if latent else 0
    pad = GRID_W if bias is not None else 0
    seq_spec = pl.BlockSpec((seq_len, D_MODEL), lambda s, t: (s, 0))
    mod_spec = pl.BlockSpec((None, None, 6, D_MODEL), lambda s, t: (l, (1 + s) if latent else 0, 0, 0))
    in_specs = [pl.BlockSpec((seq_len, D_MODEL), lambda s, t: (s + seq_off, 0)), mod_spec,
                pl.BlockSpec((D_MODEL, wq + 2 * wk), lambda s, t: (0, 0))]
    args = [x, mod, w_qkv.astype(BF16)]
    if norm_args is not None:
        gq, gk, bd = norm_args
        in_specs += [pl.BlockSpec((1, wq), lambda s, t: (0, 0)), pl.BlockSpec((1, wk), lambda s, t: (0, 0)),
                     pl.BlockSpec((wq, wq), lambda s, t: (0, 0))]
        args += [gq, gk, bd]
    if rope_args is not None:
        in_specs += [pl.BlockSpec((seq_len, LANES), lambda s, t: (0, 0))] * 3
        args += list(rope_args)
    if ctx_args is not None:
        in_specs += [pl.BlockSpec((None, n_ctx, wk), lambda s, t: (s, 0, 0))] * 2
        args += list(ctx_args)
    if bias is not None:
        in_specs += [pl.BlockSpec(bias.shape, lambda s, t: (0, 0, 0, 0))]
        args += [bias]
    in_specs += [pl.BlockSpec((D_MODEL, D_MODEL), lambda s, t: (0, 0)),
                 pl.BlockSpec((1, D_MODEL), lambda s, t: (0, 0)), pl.BlockSpec((1, D_MODEL), lambda s, t: (0, 0))]
    args += [w_o.astype(BF16), ln_g.reshape(1, D_MODEL), ln_b.reshape(1, D_MODEL)]
    out_specs = [seq_spec]
    out_shape = [jax.ShapeDtypeStruct((n_seq * seq_len, D_MODEL), F32)]
    if emit_kv:
        out_specs += [pl.BlockSpec((seq_len, wk), lambda s, t: (s, 0))] * 2
        out_shape += [jax.ShapeDtypeStruct((n_seq * seq_len, wk), F32)] * 2
    kern = functools.partial(_attn_kernel, seq_len=seq_len, tq=tq, n_kv=n_kv, norm=norm_args is not None,
                             rope=rope_args is not None, n_ctx=n_ctx, na=bias is not None, emit_kv=emit_kv)
    return pl.pallas_call(
        kern,
        grid=(n_seq, n_qt),
        in_specs=in_specs,
        out_specs=out_specs,
        out_shape=out_shape,
        scratch_shapes=[pltpu.VMEM((seq_len, wq), BF16), pltpu.VMEM((n_ctx + seq_len + pad, wk), BF16),
                        pltpu.VMEM((n_ctx + seq_len + pad, wk), BF16), pltpu.VMEM((seq_len, wq), BF16)],
        compiler_params=_cparams(("arbitrary", "arbitrary")),
        name=name,
    )(*args)


def _rope_tables():
    t = np.arange(DEC_SEQ)
    pos = np.stack([t // GRID_W, t % GRID_W], axis=1).astype(np.float64)
    inv = (ROPE_THETA ** (-np.arange(ROPE_PAIRS, dtype=np.float32) / ROPE_PAIRS)).astype(np.float64)
    ang = pos[:, :, None] * inv[None, None, :]
    cos, sin = np.cos(ang), np.sin(ang)
    zero = np.zeros_like(sin)
    cos_t = np.concatenate([cos, cos], axis=2).reshape(DEC_SEQ, HEAD_DIM)
    s1_t = np.concatenate([-sin, zero], axis=2).reshape(DEC_SEQ, HEAD_DIM)
    s2_t = np.concatenate([zero, sin], axis=2).reshape(DEC_SEQ, HEAD_DIM)
    return tuple(jnp.asarray(np.tile(a, (1, LANES // HEAD_DIM)), F32) for a in (cos_t, s1_t, s2_t))


def _block_diag_mean():
    a = np.kron(np.eye(N_HEADS), np.full((HEAD_DIM, HEAD_DIM), 1.0 / HEAD_DIM))
    return jnp.asarray(a, BF16)


def _bias_kernel(rpb_ref, sel_ref, inside_ref, o_ref):
    v = jnp.dot(rpb_ref[...], sel_ref[...], precision=HIGHEST, preferred_element_type=F32)
    o_ref[...] = jnp.where(inside_ref[...] > 0.0, v, -1e30)


def _na_bias_table(rpb):
    n_rows, n_off = N_HEADS * (2 * WIN_R - 1), 2 * WIN_C - 1
    cols = np.arange(GRID_W)
    col_start = np.clip(cols - WIN_C // 2, 0, GRID_W - WIN_C)
    kc = np.arange(GRID_W)
    inside = (kc[None, :] >= col_start[:, None]) & (kc[None, :] < col_start[:, None] + WIN_C)
    off = np.clip(kc[None, :] - cols[:, None] + (WIN_C - 1), 0, n_off - 1)
    sel = (np.arange(LANES)[:, None] == off.reshape(1, -1)).astype(np.float32)
    rpb2 = jnp.pad(rpb.reshape(n_rows, n_off), ((0, 0), (0, LANES - n_off)))
    blocks = pl.pallas_call(
        _bias_kernel,
        out_shape=jax.ShapeDtypeStruct((n_rows, GRID_W * GRID_W), F32),
        name="na_bias",
    )(rpb2, jnp.asarray(sel), jnp.asarray(inside.reshape(1, -1).astype(np.float32)))
    blocks = blocks.reshape(N_HEADS, 2 * WIN_R - 1, GRID_W, GRID_W)
    blocks = jnp.pad(blocks, ((0, 0), (1, 2), (0, 0), (0, 0)), constant_values=-1e30)
    return jnp.concatenate([blocks[:, :-1], blocks[:, 1:]], axis=-1)


MOE_TM = 256
N_TILES = N_TOK // MOE_TM
CHUNK_SHIFT, BIG_SHIFT, PART_SHIFT, SUB_SHIFT = 3, 5, 7, 9
CHUNK = 1 << CHUNK_SHIFT
BIG = 1 << BIG_SHIFT
TILE_ROWS = 1280
SUB = 1 << SUB_SHIFT
PART = 1 << PART_SHIFT
ITEM_ROWS = 2048
MAX_ITEMS = 48
ITEM_TABLE = 64
MAX_ROWS = N_ASSIGN + N_TILES * N_EXPERTS * (CHUNK - 1)
BIG_LIST = 1024
SMALL_LIST = 2560
N_SEG =N_TILES * N_EXPERTS

assert TILE_ROWS >= MOE_TM * TOP_K + N_EXPERTS * (CHUNK - 1) and TILE_ROWS % MOE_TM == 0 and ITEM_ROWS % SUB == 0
assert MAX_ITEMS >= N_EXPERTS + (N_ASSIGN + N_SEG * (CHUNK - 1) - 1) // (ITEM_ROWS - MOE_TM)
assert BIG_LIST >= MAX_ROWS // BIG + MOE_TM // BIG and SMALL_LIST >= (N_SEG + 1) * (BIG // CHUNK - 1)
assert ITEM_ROWS <= 1 << 11


def _dispatch_kernel(*refs, n_x):
    mod_ref, rwt_ref, rbc_ref, upper_ref, lower_ref, xsb_ref, route_ref, tab_ref = refs[n_x:]
    h2 = _stream_tile(refs[:n_x], MOE_TM) * (1.0 + mod_ref[4:5, :]) + mod_ref[3:4, :]
    def split(v):
        hi = v.astype(BF16)
        return hi, (v - hi.astype(F32)).astype(BF16)

    def nt_dot(a, b):
        return lax.dot_general(a, b, (((1,), (1,)), ((), ())), preferred_element_type=F32)

    (w_hi, w_lo), (h_hi, h_lo) = split(rwt_ref[...]), split(h2)
    logits = (nt_dot(w_hi, h_hi) + nt_dot(w_hi, h_lo) + nt_dot(w_lo, h_hi)) + rbc_ref[...]
    sub = lax.broadcasted_iota(I32, (N_EXPERTS, MOE_TM), 0)
    vals, hots = [], []
    cur = logits
    for _ in range(TOP_K):
        m = jnp.max(cur, axis=0, keepdims=True)
        idx = jnp.min(jnp.where(cur == m, sub, N_EXPERTS), axis=0, keepdims=True)
        hot = sub == idx
        vals.append(m)
        hots.append(hot)
        cur = jnp.where(hot, -jnp.inf, cur)
    exps = [jnp.exp(v - vals[0]) for v in vals]
    den = (exps[0] + exps[1]) + (exps[2] + exps[3])
    mask = jnp.zeros((N_EXPERTS, MOE_TM), F32)
    for hot in hots:
        mask = mask + jnp.where(hot, 1.0, 0.0)
    before = jnp.dot(mask.astype(BF16), upper_ref[...], preferred_element_type=F32)
    n = jnp.sum(mask, axis=1, keepdims=True)
    p = (((n.astype(I32) + (CHUNK - 1)) >> CHUNK_SHIFT) << CHUNK_SHIFT).astype(F32)
    start = jnp.dot(lower_ref[...], jnp.broadcast_to(p, (N_EXPERTS, LANES)), precision=HIGHEST,
                    preferred_element_type=F32)[:, :1]
    base = start + before
    lps = [jnp.sum(jnp.where(hot, base, 0.0), axis=0, keepdims=True) for hot in hots]
    route_ref[...] = jnp.concatenate(lps + [e / den for e in exps], axis=0)
    lane = lax.broadcasted_iota(I32, (N_EXPERTS, LANES), 1)
    tab_ref[...] = jnp.where(lane == 0, n, jnp.where(lane == 1, p, jnp.where(lane == 2, start, 0.0)))
    jrow = lax.broadcasted_iota(I32, (TILE_ROWS, MOE_TM), 0)
    onehot = jnp.zeros((TILE_ROWS, MOE_TM), F32)
    for lp in lps:
        onehot = jnp.where(jrow == lp.astype(I32), 1.0, onehot)
    xs = jnp.dot(onehot.astype(BF16), h_hi, preferred_element_type=F32)
    xsb_ref[...] = pltpu.pack_elementwise([xs[:, :PACKED_W], xs[:, PACKED_W:]], packed_dtype=BF16)


def _dispatch(x1, mod, l, rw, rb):
    upper = jnp.asarray(np.triu(np.ones((MOE_TM, MOE_TM)), 1), BF16)
    lower = jnp.asarray(np.tril(np.ones((N_EXPERTS, N_EXPERTS)), -1), F32)
    x_specs, x_args = _stream_specs(x1, MOE_TM)
    return pl.pallas_call(
        functools.partial(_dispatch_kernel, n_x=len(x_args)),
        grid=(N_TILES,),
        in_specs=x_specs + [_mod_spec(l, MOE_TM), _full((N_EXPERTS, D_MODEL)), _full((N_EXPERTS, 1)),
                            _full((MOE_TM, MOE_TM)), _full((N_EXPERTS, N_EXPERTS))],
        out_specs=[pl.BlockSpec((None, TILE_ROWS, PACKED_W), lambda i: (i, 0, 0)),
                   pl.BlockSpec((None, 2 * TOP_K, MOE_TM), lambda i: (i, 0, 0)),
                   pl.BlockSpec((None, N_EXPERTS, LANES), lambda i: (i, 0, 0))],
        out_shape=[jax.ShapeDtypeStruct((N_TILES, TILE_ROWS, PACKED_W), I32),
                   jax.ShapeDtypeStruct((N_TILES, 2 * TOP_K, MOE_TM), F32),
                   jax.ShapeDtypeStruct((N_TILES, N_EXPERTS, LANES), F32)],
        compiler_params=_cparams(("arbitrary",)),
        name="moe_dispatch",
    )(*x_args, mod, rw.T, rb.reshape(N_EXPERTS, 1), upper, lower)


def _plan_kernel(p_ref, s_ref, zbig_hbm, zsmall_hbm, iexp_ref, ib0_ref, ib1_ref, irows_ref, ibig_ref, ismall_ref,
                 big_ref, small_ref, cnt_ref, sem):
    clears = [pltpu.make_async_copy(zbig_hbm, big_ref, sem), pltpu.make_async_copy(zsmall_hbm, small_ref, sem)]
    for cp in clears:
        cp.start()
    for cp in clears:
        cp.wait()
    cnt_ref[0] = 0

    def emit(e, b0, b1, rows):
        it = cnt_ref[0]
        iexp_ref[it] = e
        ib0_ref[it] = b0
        ib1_ref[it] = b1
        irows_ref[it] = rows
        cnt_ref[0] = it + 1

    def per_expert(e, last_e):
        def per_tile(b, carry):
            b0, rows = carry
            pb = p_ref[b * N_EXPERTS + e]
            full = rows + pb > ITEM_ROWS

            @pl.when(full)
            def _():
                emit(e, b0, b, rows)

            return jnp.where(full, b, b0), jnp.where(full, pb, rows + pb)

        b0, rows = lax.fori_loop(0, N_TILES, per_tile, (jnp.int32(0), jnp.int32(0)))

        @pl.when(rows > 0)
        def _():
            emit(e, b0, jnp.int32(N_TILES), rows)

        return jnp.where(rows > 0, e, last_e)

    last_e = lax.fori_loop(0, N_EXPERTS, per_expert, jnp.int32(0))

    def idle(it, c):
        iexp_ref[it] = last_e
        ib0_ref[it] = 0
        ib1_ref[it] = 0
        irows_ref[it] = 0
        return c

    lax.fori_loop(cnt_ref[0], ITEM_TABLE, idle, 0)

    def per_item(it, carry):
        n_big, n_small = carry
        ibig_ref[it] = n_big
        ismall_ref[it] = n_small
        e = iexp_ref[it]

        def per_tile(b, c):
            dst, n_big, n_small = c
            pb = p_ref[b * N_EXPERTS + e]
            word = (b << 22) | (s_ref[b * N_EXPERTS + e] << 11) | dst
            nb = pb >> BIG_SHIFT
            ns = (pb - nb * BIG) >> CHUNK_SHIFT
            for k in range(MOE_TM // BIG):
                big_ref[n_big + k] = word + k * (BIG << 11 | BIG)
            for k in range(BIG // CHUNK - 1):
                small_ref[n_small + k] = word + nb * (BIG << 11 | BIG) + k * (CHUNK << 11 | CHUNK)
            return dst + pb, n_big + nb, n_small + ns

        _, n_big, n_small = lax.fori_loop(ib0_ref[it], ib1_ref[it], per_tile, (jnp.int32(0), n_big, n_small))
        return n_big, n_small

    lax.fori_loop(0, ITEM_TABLE, per_item, (jnp.int32(0), jnp.int32(0)))


def _plan(ptab, stab):
    smem = pl.BlockSpec(memory_space=pltpu.SMEM)
    table = jax.ShapeDtypeStruct((ITEM_TABLE,), I32)
    return pl.pallas_call(
        _plan_kernel,
        in_specs=[smem, smem, pl.BlockSpec(memory_space=pl.ANY), pl.BlockSpec(memory_space=pl.ANY)],
        out_specs=[smem] * 8,
        out_shape=[table] * 6 + [jax.ShapeDtypeStruct((BIG_LIST,), I32), jax.ShapeDtypeStruct((SMALL_LIST,), I32)],
        scratch_shapes=[pltpu.SMEM((1,), I32), pltpu.SemaphoreType.DMA(())],
        name="moe_plan",
    )(ptab, stab, jnp.zeros((BIG_LIST,), I32), jnp.zeros((SMALL_LIST,), I32))


def _unpack_halves(p):
    lo = pltpu.unpack_elementwise(p, index=0, packed_dtype=BF16, unpacked_dtype=F32)
    hi = pltpu.unpack_elementwise(p, index=1, packed_dtype=BF16, unpacked_dtype=F32)
    return lo.astype(BF16), hi.astype(BF16)


def _unpack_rows(p):
    return jnp.concatenate(_unpack_halves(p), axis=1)


def _pack_rows(y):
    return pltpu.pack_elementwise([y[:, :PACKED_W], y[:, PACKED_W:]], packed_dtype=BF16)


def _expert_kernel(iexp_ref, irows_ref, ibig_ref, ismall_ref, big_ref, small_ref, xsb_hbm, w1_hbm, w2_hbm,
                   b1_ref, b2_ref, ysb_hbm, xg_ref, g_ref, a_ref, yb_ref, w_ref, cnt_ref,
                   in_big, in_small, out_big, out_small, w_sem, *, layer):
    it = pl.program_id(0)
    rows = irows_ref[it]
    slot = it & 1

    def weight_copies(item, phase):
        ee = iexp_ref[item]
        copies = []
        for k in range(2):
            half = pl.ds(k * (D_MODEL // 2), D_MODEL // 2)
            if phase < 2:
                src = w1_hbm.at[layer, ee, half, pl.ds(phase * D_FF, D_FF)]
            else:
                src = w2_hbm.at[layer, ee, half, :]
            copies.append(pltpu.make_async_copy(src, w_ref.at[phase, half, :], w_sem.at[phase]))
        return copies

    def start_weights(item, phase):
        @pl.when(irows_ref[item] > 0)
        def _():
            for cp in weight_copies(item, phase):
                cp.start()

    def wait_weights(phase):
        for cp in weight_copies(it, phase):
            cp.wait()

    def aligned(v):
        return v if isinstance(v, int) else pl.multiple_of(v, CHUNK)

    def copy(to_vmem, buf, b, src, dst, size):
        hbm = (xsb_hbm if to_vmem else ysb_hbm).at[b, pl.ds(aligned(src), size), :]
        if to_vmem:
            sem = in_big if size == BIG else in_small
            return pltpu.make_async_copy(hbm, xg_ref.at[buf, pl.ds(aligned(dst), size), :], sem)
        sem = out_big if size == BIG else out_small
        return pltpu.make_async_copy(yb_ref.at[pl.ds(aligned(dst), size), :], hbm, sem)

    def start_segments(item, to_vmem, buf):
        def start_list(lst_ref, lo, hi, size):
            def body(i, z):
                word = lst_ref[i]
                copy(to_vmem, buf, word >> 22, (word >> 11) & 2047, word & 2047, size).start()
                return z

            lax.fori_loop(lo, hi, body, 0)
            return hi - lo

        return (start_list(big_ref, ibig_ref[item], ibig_ref[item + 1], BIG),
                start_list(small_ref, ismall_ref[item], ismall_ref[item + 1], CHUNK))

    def wait_segments(to_vmem, n_big, n_small):
        def wait_big(c, z):
            copy(to_vmem, 0, 0, 0, 0, BIG).wait()
            return z

        def wait_small(c, z):
            copy(to_vmem, 0, 0, 0, 0, CHUNK).wait()
            return z

        lax.fori_loop(0, n_big, wait_big, 0)
        lax.fori_loop(0, n_small, wait_small, 0)

    @pl.when(it == 0)
    def _init():
        xg_ref[...] = jnp.zeros_like(xg_ref)
        start_weights(0, 0)
        start_weights(0, 1)
        cnt_ref[0], cnt_ref[1] = start_segments(0, True, 0)
        cnt_ref[2] = 0
        cnt_ref[3] = 0

    def for_sub_tiles(fn):
        n_full = rows >> SUB_SHIFT
        rem_parts = (rows - (n_full << SUB_SHIFT) + (PART - 1)) >> PART_SHIFT

        def body(i, c):
            fn(pl.ds(pl.multiple_of(i * SUB, SUB), SUB))
            return c

        lax.fori_loop(0, n_full, body, 0)
        r0 = pl.multiple_of(n_full * SUB, SUB)
        for k in range(1, SUB // PART + 1):
            @pl.when(rem_parts == k)
            def _(k=k):
                fn(pl.ds(r0, k * PART))

    def matmul(lo, hi, phase):
        half = D_MODEL // 2
        return (jnp.dot(lo, w_ref[phase, :half, :].astype(BF16), preferred_element_type=F32)
                + jnp.dot(hi, w_ref[phase, half:, :].astype(BF16), preferred_element_type=F32))

    @pl.when(rows > 0)
    def _item():
        start_weights(it, 2)
        wait_segments(True, cnt_ref[0], cnt_ref[1])
        wait_weights(0)

        def gate(r):
            lo, hi = _unpack_halves(xg_ref[slot, r, :])
            gate = jnp.minimum(matmul(lo, hi, 0) + b1_ref[:, :D_FF], SWIGLU_LIMIT)
            g_ref[r, :] = gate * (0.5 + 0.5 * jnp.tanh((0.5 * SWIGLU_ALPHA) * gate))

        for_sub_tiles(gate)

        start_weights(it + 1, 0)
        cnt_ref[0], cnt_ref[1] = start_segments(it + 1, True, 1 - slot)
        wait_weights(1)

        def up_act(r):
            lo, hi = _unpack_halves(xg_ref[slot, r, :])
            up = matmul(lo, hi, 1) + b1_ref[:, D_FF:]
            up = jnp.clip(up, -SWIGLU_LIMIT, SWIGLU_LIMIT)
            a_ref[r, :] = (g_ref[r, :] * (up + 1.0)).astype(BF16)

        for_sub_tiles(up_act)

        start_weights(it + 1, 1)
        wait_weights(2)
        wait_segments(False, cnt_ref[2], cnt_ref[3])

        def down(r):
            y = matmul(a_ref[r, :D_FF // 2], a_ref[r, D_FF // 2:], 2) + b2_ref[...]
            yb_ref[r, :] = _pack_rows(y)

        for_sub_tiles(down)
        cnt_ref[2], cnt_ref[3] = start_segments(it, False, 0)

    @pl.when(it == MAX_ITEMS - 1)
    def _drain():
        wait_segments(False, cnt_ref[2], cnt_ref[3])


def _experts(iexp, irows, ibig, ismall, big, small, xsb, l, w1, b1, w2, b2):
    n_prefetch = 6
    any_spec = pl.BlockSpec(memory_space=pl.ANY)
    return pl.pallas_call(
        functools.partial(_expert_kernel, layer=l),
        grid_spec=pltpu.PrefetchScalarGridSpec(
            num_scalar_prefetch=n_prefetch,
            grid=(MAX_ITEMS,),
            in_specs=[any_spec, any_spec, any_spec,
                      pl.BlockSpec((None, None, 1, 2 * D_FF), lambda it, ie, *_: (l, ie[it], 0, 0)),
                      pl.BlockSpec((None, None, 1, D_MODEL), lambda it, ie, *_: (l, ie[it], 0, 0))],
            out_specs=pl.BlockSpec(memory_space=pl.ANY),
            scratch_shapes=[pltpu.VMEM((2, ITEM_ROWS, PACKED_W), I32),
                            pltpu.VMEM((ITEM_ROWS, D_FF), F32),
                            pltpu.VMEM((ITEM_ROWS, D_FF), BF16),
                            pltpu.VMEM((ITEM_ROWS, PACKED_W), I32),
                            pltpu.VMEM((3, D_MODEL, D_FF), F32),
                            pltpu.SMEM((4,), I32),
                            pltpu.SemaphoreType.DMA(()),
                            pltpu.SemaphoreType.DMA(()),
                            pltpu.SemaphoreType.DMA(()),
                            pltpu.SemaphoreType.DMA(()),
                            pltpu.SemaphoreType.DMA((3,))],
        ),
        out_shape=jax.ShapeDtypeStruct((N_TILES, TILE_ROWS, PACKED_W), I32),
        input_output_aliases={n_prefetch: 0},
        compiler_params=_cparams(("arbitrary",)),
        name="moe_experts",
    )(iexp, irows, ibig, ismall, big, small, xsb, w1, w2,
      b1.reshape(DEPTH, N_EXPERTS, 1, 2 * D_FF), b2.reshape(DEPTH, N_EXPERTS, 1, D_MODEL))


def _combine_kernel(*refs, n_x, n_out):
    ysb_ref, route_ref, eye_ref, mod_ref, g_ref, b_ref = refs[n_x:len(refs) - n_out]
    o_refs = refs[len(refs) - n_out:]
    rt = lax.dot_general(eye_ref[...], route_ref[...], (((1,), (1,)), ((), ())), precision=HIGHEST,
                         preferred_element_type=F32)
    lane = lax.broadcasted_iota(I32, (MOE_TM, TILE_ROWS), 1)
    c = jnp.zeros((MOE_TM, TILE_ROWS), F32)
    for k in range(TOP_K):
        c = jnp.where(lane == rt[:, k:k + 1].astype(I32), rt[:, TOP_K + k:TOP_K + k + 1], c)
    f = jnp.dot(c.astype(BF16), _unpack_rows(ysb_ref[...]), preferred_element_type=F32)
    x = _stream_tile(refs[:n_x], MOE_TM)
    out = _layer_norm(DEEPNORM_ALPHA * x + mod_ref[5:6, :] * f, g_ref[...], b_ref[...])
    if n_out == 1:
        o_refs[0][...] = out
    else:
        is_ctx = pl.program_id(0) < N_PROMPT // MOE_TM

        @pl.when(is_ctx)
        def _():
            o_refs[0][...] = out

        @pl.when(jnp.logical_not(is_ctx))
        def _():
            o_refs[1][...] = out


def _combine(x1, ysb, route, mod, l, g, b, split_out):
    x_specs, x_args = _stream_specs(x1, MOE_TM)
    if split_out:
        out_specs, _ = _stream_specs((None, None), MOE_TM)
        out_shape = [jax.ShapeDtypeStruct((N_PROMPT, D_MODEL), F32), jax.ShapeDtypeStruct((N_SAMPLE, D_MODEL), F32)]
    else:
        out_specs, _ = _stream_specs(None, MOE_TM)
        out_shape = [jax.ShapeDtypeStruct((N_TOK, D_MODEL), F32)]
    return pl.pallas_call(
        functools.partial(_combine_kernel, n_x=len(x_args), n_out=len(out_shape)),
        grid=(N_TILES,),
        in_specs=x_specs + [pl.BlockSpec((None, TILE_ROWS, PACKED_W), lambda i: (i, 0, 0)),
                            pl.BlockSpec((None, 2 * TOP_K, MOE_TM), lambda i: (i, 0, 0)),
                            _full((MOE_TM, MOE_TM)), _mod_spec(l, MOE_TM), _full((1, D_MODEL)),
                            _full((1, D_MODEL))],
        out_specs=out_specs,
        out_shape=out_shape,
        compiler_params=_cparams(("arbitrary",)),
        name="moe_combine",
    )(*x_args, ysb, route, jnp.eye(MOE_TM, dtype=F32), mod, g.reshape(1, D_MODEL), b.reshape(1, D_MODEL))


def _moe_layer(x1, mod, l, rw, rb, w1, b1, w2, b2, ln_g, ln_b, split_out=False):
    xsb, route, tab = _dispatch(x1, mod, l, rw, rb)
    ptab = tab[:, :, 1].astype(I32).reshape(N_SEG)
    stab = tab[:, :, 2].astype(I32).reshape(N_SEG)
    iexp, _, _, irows, ibig, ismall, big, small = _plan(ptab, stab)
    ysb = _experts(iexp, irows, ibig, ismall, big, small, xsb, l, w1, b1, w2, b2)
    return _combine(x1, ysb, route, mod, l, ln_g, ln_b, split_out)


def kernel(x_prompt, x_sample, cache_k_attn, cache_v_attn, cache_k_na, cache_v_na, c, c_ctx, w_mod, b_mod, ln1_g, ln1_b, ln2_g, ln2_b, conv_w_in, conv_w, conv_b, conv_w_out, attn_w_qkv, attn_q_norm, attn_k_norm, attn_w_o, na_w_qkv, na_rpb, na_w_o, router_w, router_b, moe_w1, moe_b1, moe_w2, moe_b2):
    x = (x_prompt.reshape(N_PROMPT, D_MODEL), x_sample.reshape(N_SAMPLE, D_MODEL))
    mod = _adaln_all(jnp.concatenate([c_ctx[None, :], c], axis=0), w_mod, b_mod)
    new_kv = {}
    for l in range(DEPTH):
        kind, j = l % 3, l // 3
        if kind == 0:
            x1 = _conv_layer(x, mod, l, conv_w_in[j], conv_w[j], conv_b[j], conv_w_out[j], ln1_g[l], ln1_b[l])
        elif kind == 1:
            gq = jnp.tile(attn_q_norm[j], N_HEADS).reshape(1, N_HEADS * HEAD_DIM)
            gk = jnp.tile(attn_k_norm[j], N_KV_HEADS).reshape(1, N_KV_HEADS * HEAD_DIM)
            norm_args = (gq, gk, _block_diag_mean())
            wk = N_KV_HEADS * HEAD_DIM
            xp1, nk, nv = _attn_call(x, mod, l, False, attn_w_qkv[j], attn_w_o[j], ln1_g[l], ln1_b[l],
                                     n_seq=BATCH, seq_len=SEQ, tq=SEQ, n_kv=N_KV_HEADS, norm_args=norm_args,
                                     emit_kv=True, name="gqa_prompt")
            new_kv["k_attn"] = nk.reshape(BATCH, 1, SEQ, N_KV_HEADS, HEAD_DIM)
            new_kv["v_attn"] = nv.reshape(BATCH, 1, SEQ, N_KV_HEADS, HEAD_DIM)
            (xs1,) = _attn_call(x, mod, l, True, attn_w_qkv[j], attn_w_o[j], ln1_g[l], ln1_b[l],
                                n_seq=DEC_BATCH, seq_len=DEC_SEQ, tq=256, n_kv=N_KV_HEADS, norm_args=norm_args,
                                rope_args=_rope_tables(),
                                ctx_args=(cache_k_attn[:, j].reshape(DEC_BATCH, PAST_LEN, wk),
                                          cache_v_attn[:, j].reshape(DEC_BATCH, PAST_LEN, wk)),
                                name="gqa_sample")
            x1 = (xp1, xs1)
        else:
            wk = N_HEADS * HEAD_DIM
            xp1, nk, nv = _attn_call(x, mod, l, False, na_w_qkv[j], na_w_o[j], ln1_g[l], ln1_b[l],
                                     n_seq=BATCH, seq_len=SEQ, tq=SEQ, n_kv=N_HEADS, emit_kv=True,
                                     name="mha_prompt")
            new_kv["k_na"] = nk.reshape(BATCH, 1, SEQ, N_HEADS, HEAD_DIM)
            new_kv["v_na"] = nv.reshape(BATCH, 1, SEQ, N_HEADS, HEAD_DIM)
            (xs1,) = _attn_call(x, mod, l, True, na_w_qkv[j], na_w_o[j], ln1_g[l], ln1_b[l],
                                n_seq=DEC_BATCH, seq_len=DEC_SEQ, tq=2 * GRID_W, n_kv=N_HEADS,
                                ctx_args=(cache_k_na[:, j].reshape(DEC_BATCH, PAST_LEN, wk),
                                          cache_v_na[:, j].reshape(DEC_BATCH, PAST_LEN, wk)),
                                bias=_na_bias_table(na_rpb[j]), name="na_sample")
            x1 = (xp1, xs1)
        out = _moe_layer(x1, mod, l, router_w[l], router_b[l], moe_w1, moe_b1, moe_w2, moe_b2,
                         ln2_g[l], ln2_b[l], split_out=l == DEPTH - 1)
        x = out[0]
    y_prompt = out[0].reshape(BATCH, SEQ, D_MODEL)
    y_sample = out[1].reshape(DEC_BATCH, DEC_SEQ, D_MODEL)
    return (y_prompt, y_sample, new_kv["k_attn"], new_kv["v_attn"], new_kv["k_na"], new_kv["v_na"])
```

```python
import functools

import numpy as np
import jax
import jax.numpy as jnp
from jax import lax
from jax.experimental import pallas as pl
from jax.experimental.pallas import tpu as pltpu

D_MODEL = 1024
BATCH = 16
SEQ = 256
DEPTH = 4
DEC_BATCH = 2
DEC_SEQ = 1024
PAST_LEN = 256
GRID_W = 64
HEAD_DIM = 64
N_HEADS = 16
N_KV_HEADS = 4
ROPE_THETA = 10000.0
ROPE_PAIRS = HEAD_DIM // 4
WIN_R = 8
WIN_C = 16
N_EXPERTS = 32
TOP_K = 4
D_FF = D_MODEL
SWIGLU_LIMIT = 7.0
SWIGLU_ALPHA = 1.702
DEEPNORM_ALPHA = (2 * DEPTH) ** 0.25
LN_EPS = 1e-5
RMS_EPS = 1e-6

F32 = jnp.float32
BF16 = jnp.bfloat16
I32 = jnp.int32
HIGHEST = lax.Precision.HIGHEST

N_PROMPT = BATCH * SEQ
N_SAMPLE = DEC_BATCH * DEC_SEQ
N_TOK = N_PROMPT + N_SAMPLE
N_ASSIGN = N_TOK * TOP_K

LANES = 128
SUBLANES = 8
PACKED_W = D_MODEL // 2

VMEM_LIMIT = 58 * 1024 * 1024


def _cparams(sem):
    return pltpu.CompilerParams(dimension_semantics=sem, vmem_limit_bytes=VMEM_LIMIT)


def _layer_norm(x, g, b):
    mu = jnp.mean(x, -1, keepdims=True)
    xc = x - mu
    var = jnp.mean(xc * xc, -1, keepdims=True)
    return xc * lax.rsqrt(var + LN_EPS) * g + b


def _mod_row(i, tile):
    n_prompt_tiles = N_PROMPT // tile
    return jnp.where(i < n_prompt_tiles, 0, 1 + (i - n_prompt_tiles) // (DEC_SEQ // tile))


def _mod_spec(l, tile):
    return pl.BlockSpec((None, None, 6, D_MODEL), lambda i, *_: (l, _mod_row(i, tile), 0, 0))


def _full(shape):
    nd = len(shape)
    return pl.BlockSpec(shape, lambda *_: (0,) * nd)


def _stream_specs(x, tile):
    if not isinstance(x, tuple):
        return [pl.BlockSpec((tile, D_MODEL), lambda i, *_: (i, 0))], [x]
    n_a = N_PROMPT // tile
    return ([pl.BlockSpec((tile, D_MODEL), lambda i, *_: (jnp.minimum(i, n_a - 1), 0)),
             pl.BlockSpec((tile, D_MODEL), lambda i, *_: (jnp.maximum(i - n_a, 0), 0))], list(x))


def _stream_tile(x_refs, tile):
    if len(x_refs) == 1:
        return x_refs[0][...]
    return jnp.where(pl.program_id(0) < N_PROMPT // tile, x_refs[0][...], x_refs[1][...])


ADALN_TN = 1536


N_COND = 1 + DEC_BATCH
ADALN_UNROLL = 4


def _adaln_kernel(ct_ref, w_ref, b_ref, o_ref, s_ref):
    @pl.when((pl.program_id(0) == 0) & (pl.program_id(1) == 0))
    def _silu():
        c = ct_ref[...]
        s_ref[...] = c * jax.nn.sigmoid(c)

    def body(j, accs):
        accs = list(accs)
        for u in range(ADALN_UNROLL):
            rows = pl.ds(pl.multiple_of((j * ADALN_UNROLL + u) * SUBLANES, SUBLANES), SUBLANES)
            w = w_ref[rows, :]
            for r in range(N_COND):
                accs[r] = accs[r] + w * jnp.concatenate([s_ref[r, rows, :]] * (ADALN_TN // LANES), axis=1)
        return tuple(accs)

    zero = jnp.zeros((SUBLANES, ADALN_TN), F32)
    accs = lax.fori_loop(0, D_MODEL // (SUBLANES * ADALN_UNROLL), body, (zero,) * N_COND)
    rows = [jnp.sum(a, axis=0, keepdims=True) for a in accs]
    o_ref[...] = jnp.concatenate(rows + [jnp.zeros((8 - N_COND, ADALN_TN), F32)], axis=0) + b_ref[...]


def _adaln_all(cond, w_mod, b_mod):
    n = 6 * D_MODEL
    cond_t = jnp.broadcast_to(cond[:, :, None], (N_COND, D_MODEL, LANES))
    out = pl.pallas_call(
        _adaln_kernel,
        grid=(DEPTH, n // ADALN_TN),
        in_specs=[
            pl.BlockSpec((N_COND, D_MODEL, LANES), lambda l, j: (0, 0, 0)),
            pl.BlockSpec((None, D_MODEL, ADALN_TN), lambda l, j: (l, 0, j)),
            pl.BlockSpec((None, 1, ADALN_TN), lambda l, j: (l, 0, j)),
        ],
        out_specs=pl.BlockSpec((None, 8, ADALN_TN), lambda l, j: (l, 0, j)),
        out_shape=jax.ShapeDtypeStruct((DEPTH, 8, n), F32),
        scratch_shapes=[pltpu.VMEM((N_COND, D_MODEL, LANES), F32)],
        compiler_params=_cparams(("arbitrary", "arbitrary")),
        name="adaln",
    )(cond_t, w_mod, b_mod.reshape(DEPTH, 1, n))
    return out.reshape(DEPTH, 8, 6, D_MODEL)


CONV_TM = 1024


def _conv_kernel(*refs, n_x):
    mod_ref, win_ref, cw_ref, cb_ref, wout_ref, g_ref, b_ref, o_ref = refs[n_x:]
    i = pl.program_id(0)
    x = _stream_tile(refs[:n_x], CONV_TM)
    h = (x * (1.0 + mod_ref[1:2, :]) + mod_ref[0:1, :]).astype(BF16)
    gc = jnp.dot(h, win_ref[:, D_MODEL:2 * D_MODEL], preferred_element_type=F32)
    xv = jnp.dot(h, win_ref[:, 2 * D_MODEL:], preferred_element_type=F32)
    u = gc * xv
    seq_len = jnp.where(i < N_PROMPT // CONV_TM, SEQ, DEC_SEQ)
    t = lax.broadcasted_iota(I32, (CONV_TM, 1), 0) & (seq_len - 1)
    u_prev = jnp.where(t == 0, 0.0, pltpu.roll(u, 1, axis=0))
    u_next = jnp.where(t == seq_len - 1, 0.0, pltpu.roll(u, CONV_TM - 1, axis=0))
    y = u_prev * cw_ref[0:1, :] + u * cw_ref[1:2, :] + u_next * cw_ref[2:3, :] + cb_ref[...]
    gb = jnp.dot(h, win_ref[:, :D_MODEL], preferred_element_type=F32)
    v = (gb * y).astype(BF16)
    o = jnp.dot(v, wout_ref[...], preferred_element_type=F32)
    o_ref[...] = _layer_norm(DEEPNORM_ALPHA * x + mod_ref[2:3, :] * o, g_ref[...], b_ref[...])


def _conv_layer(x, mod, l, w_in, cw, cb, w_out, ln_g, ln_b):
    x_specs, x_args = _stream_specs(x, CONV_TM)
    return pl.pallas_call(
        functools.partial(_conv_kernel, n_x=len(x_args)),
        grid=(N_TOK // CONV_TM,),
        in_specs=x_specs + [_mod_spec(l, CONV_TM), _full((D_MODEL, 3 * D_MODEL)), _full((3, D_MODEL)),
                            _full((1, D_MODEL)), _full((D_MODEL, D_MODEL)), _full((1, D_MODEL)),
                            _full((1, D_MODEL))],
        out_specs=pl.BlockSpec((CONV_TM, D_MODEL), lambda i: (i, 0)),
        out_shape=jax.ShapeDtypeStruct((N_TOK, D_MODEL), F32),
        compiler_params=_cparams(("arbitrary",)),
        name="conv_mixer",
    )(*x_args, mod, w_in.astype(BF16), cw, cb.reshape(1, D_MODEL), w_out.astype(BF16),
      ln_g.reshape(1, D_MODEL), ln_b.reshape(1, D_MODEL))


ATTN_CHUNK = 256
NA_SLOTS = WIN_R + 2


def _attn_kernel(*refs, seq_len, tq, n_kv, norm, rope, n_ctx, na, emit_kv):
    refs = list(refs)
    x_ref, mod_ref, wqkv_ref = refs[:3]
    pos = 3
    if norm:
        gq_ref, gk_ref, bd_ref = refs[pos:pos + 3]
        pos += 3
    if rope:
        cos_ref, s1_ref, s2_ref = refs[pos:pos + 3]
        pos += 3
    if n_ctx:
        ck_ref, cv_ref = refs[pos:pos + 2]
        pos += 2
    if na:
        bias_ref = refs[pos]
        pos += 1
    wo_ref, lng_ref, lnb_ref = refs[pos:pos + 3]
    pos += 3
    o_ref = refs[pos]
    pos += 1
    if emit_kv:
        nk_ref, nv_ref = refs[pos:pos + 2]
        pos += 2
    q_scr, k_scr, v_scr, o_scr = refs[pos:pos + 4]

    qt = pl.program_id(1)
    n_qt = seq_len // tq
    wq = N_HEADS * HEAD_DIM
    wk = n_kv * HEAD_DIM
    rep = N_HEADS // n_kv

    def rms(v, g_ref, width):
        ms = jnp.dot((v * v).astype(BF16), bd_ref[:width, :width], preferred_element_type=F32)
        return v * lax.rsqrt(ms + RMS_EPS) * g_ref[...]

    def rot(v, rows, width):
        def tab(ref):
            t = ref[rows, :]
            return jnp.concatenate([t] * (width // LANES), axis=1)

        return (v * tab(cos_ref) + pltpu.roll(v, width - ROPE_PAIRS, axis=1) * tab(s1_ref)
                + pltpu.roll(v, ROPE_PAIRS, axis=1) * tab(s2_ref))

    chunk = min(seq_len, ATTN_CHUNK)

    @pl.when(qt == 0)
    def _project():
        def body(ci, carry):
            r0 = pl.multiple_of(ci * chunk, chunk)
            rows = pl.ds(r0, chunk)
            h = (x_ref[rows, :] * (1.0 + mod_ref[1:2, :]) + mod_ref[0:1, :]).astype(BF16)
            q = jnp.dot(h, wqkv_ref[:, :wq], preferred_element_type=F32)
            k = jnp.dot(h, wqkv_ref[:, wq:wq + wk], preferred_element_type=F32)
            v = jnp.dot(h, wqkv_ref[:, wq + wk:], preferred_element_type=F32)
            if norm:
                q = rms(q, gq_ref, wq)
                k = rms(k, gk_ref, wk)
            if emit_kv:
                nk_ref[rows, :] = k
                nv_ref[rows, :] = v
            if rope:
                q = rot(q, rows, wq)
                k = rot(k, rows, wk)
            q_scr[rows, :] = (q * (HEAD_DIM ** -0.5)).astype(BF16)
            k_scr[pl.ds(n_ctx + r0, chunk), :] = k.astype(BF16)
            v_scr[pl.ds(n_ctx + r0, chunk), :] = v.astype(BF16)
            return carry

        lax.fori_loop(0, seq_len // chunk, body, 0)
        if n_ctx:
            k_scr[:n_ctx, :] = ck_ref[...].astype(BF16)
            v_scr[:n_ctx, :] = cv_ref[...].astype(BF16)
        if na:
            k_scr[n_ctx + seq_len:, :] = jnp.zeros((GRID_W, wk), BF16)
            v_scr[n_ctx + seq_len:, :] = jnp.zeros((GRID_W, wk), BF16)

    q0 = pl.multiple_of(qt * tq, tq)
    if na:
        n_rows = DEC_SEQ // GRID_W
        w = jnp.minimum(jnp.clip(2 * qt - WIN_R // 2, 0, n_rows - WIN_R), n_rows - WIN_R - 1)
        k0 = pl.multiple_of(n_ctx + w * GRID_W, GRID_W)
        slot_lane = lax.broadcasted_iota(I32, (GRID_W, NA_SLOTS * GRID_W), 1)
        pair_base, invalid = [], []
        for j in range(2):
            r = 2 * qt + j
            first = jnp.clip(r - WIN_R // 2, 0, n_rows - WIN_R) - w
            ok = (slot_lane >= first * GRID_W) & (slot_lane < (first + WIN_R) * GRID_W)
            invalid.append(jnp.where(ok, 0.0, -1e30))
            pair_base.append(w - r + WIN_R)
    for g in range(n_kv):
        heads = [g * rep + i for i in range(rep)]
        hd = heads[0]
        head_lanes = [slice(h * HEAD_DIM, (h + 1) * HEAD_DIM) for h in heads]
        gs = slice(g * HEAD_DIM, (g + 1) * HEAD_DIM)
        qh = jnp.concatenate([q_scr[pl.ds(q0, tq), hs] for hs in head_lanes], axis=0)
        dn = (((1,), (1,)), ((), ()))
        if na:
            kc, vc = k_scr[:n_ctx, gs], v_scr[:n_ctx, gs]
            kl, vl = k_scr[pl.ds(k0, NA_SLOTS * GRID_W), gs], v_scr[pl.ds(k0, NA_SLOTS * GRID_W), gs]
            bias = jnp.concatenate(
                [jnp.concatenate([bias_ref[hd, pl.ds(pair_base[j] + 2 * i, 1)][0] for i in range(NA_SLOTS // 2)],
                                 axis=1) + invalid[j] for j in range(2)], axis=0)
            s = jnp.concatenate([lax.dot_general(qh, kc, dn, preferred_element_type=F32),
                                 lax.dot_general(qh, kl, dn, preferred_element_type=F32) + bias], axis=1)
        else:
            s = lax.dot_general(qh, k_scr[:, gs], dn, preferred_element_type=F32)
        e = jnp.exp(s - jnp.max(s, axis=1, keepdims=True))
        den = jnp.sum(e, axis=1, keepdims=True)
        eb = e.astype(BF16)
        if na:
            oh = (jnp.dot(eb[:, :n_ctx], vc, preferred_element_type=F32)
                  + jnp.dot(eb[:, n_ctx:], vl, preferred_element_type=F32))
        else:
            oh = jnp.dot(eb, v_scr[:, gs], preferred_element_type=F32)
        oh = (oh / den).astype(BF16)
        for i, hs in enumerate(head_lanes):
            o_scr[pl.ds(q0, tq), hs] = oh[i * tq:(i + 1) * tq, :]

    @pl.when(qt == n_qt - 1)
    def _finish():
        def body(ci, carry):
            rows = pl.ds(pl.multiple_of(ci * chunk, chunk), chunk)
            o = jnp.dot(o_scr[rows, :], wo_ref[...], preferred_element_type=F32)
            o_ref[rows, :] = _layer_norm(DEEPNORM_ALPHA * x_ref[rows, :] + mod_ref[2:3, :] * o,
                                         lng_ref[...], lnb_ref[...])
            return carry

        lax.fori_loop(0, seq_len // chunk, body, 0)


def _attn_call(x, mod, l, latent, w_qkv, w_o, ln_g, ln_b, *, n_seq, seq_len, tq, n_kv, norm_args=None,
               rope_args=None, ctx_args=None, bias=None, emit_kv=False, name="attn"):
    wq = N_HEADS * HEAD_DIM
    wk = n_kv * HEAD_DIM
    n_ctx = PAST_LEN if ctx_args is not None else 0
    n_qt = seq_len // tq
    seq_off = N_PROMPT // seq_len if latent else 0
    pad = GRID_W if bias is not None else 0
    seq_spec = pl.BlockSpec((seq_len, D_MODEL), lambda s, t: (s, 0))
    mod_spec = pl.BlockSpec((None, None, 6, D_MODEL), lambda s, t: (l, (1 + s) if latent else 0, 0, 0))
    in_specs = [pl.BlockSpec((seq_len, D_MODEL), lambda s, t: (s + seq_off, 0)), mod_spec,
                pl.BlockSpec((D_MODEL, wq + 2 * wk), lambda s, t: (0, 0))]
    args = [x, mod, w_qkv.astype(BF16)]
    if norm_args is not None:
        gq, gk, bd = norm_args
        in_specs += [pl.BlockSpec((1, wq), lambda s, t: (0, 0)), pl.BlockSpec((1, wk), lambda s, t: (0, 0)),
                     pl.BlockSpec((wq, wq), lambda s, t: (0, 0))]
        args += [gq, gk, bd]
    if rope_args is not None:
        in_specs += [pl.BlockSpec((seq_len, LANES), lambda s, t: (0, 0))] * 3
        args += list(rope_args)
    if ctx_args is not None:
        in_specs += [pl.BlockSpec((None, n_ctx, wk), lambda s, t: (s, 0, 0))] * 2
        args += list(ctx_args)
    if bias is not None:
        in_specs += [pl.BlockSpec(bias.shape, lambda s, t: (0, 0, 0, 0))]
        args += [bias]
    in_specs += [pl.BlockSpec((D_MODEL, D_MODEL), lambda s, t: (0, 0)),
                 pl.BlockSpec((1, D_MODEL), lambda s, t: (0, 0)), pl.BlockSpec((1, D_MODEL), lambda s, t: (0, 0))]
    args += [w_o.astype(BF16), ln_g.reshape(1, D_MODEL), ln_b.reshape(1, D_MODEL)]
    out_specs = [seq_spec]
    out_shape = [jax.ShapeDtypeStruct((n_seq * seq_len, D_MODEL), F32)]
    if emit_kv:
        out_specs += [pl.BlockSpec((seq_len, wk), lambda s, t: (s, 0))] * 2
        out_shape += [jax.ShapeDtypeStruct((n_seq * seq_len, wk), F32)] * 2
    kern = functools.partial(_attn_kernel, seq_len=seq_len, tq=tq, n_kv=n_kv, norm=norm_args is not None,
                             rope=rope_args is not None, n_ctx=n_ctx, na=bias is not None, emit_kv=emit_kv)
    return pl.pallas_call(
        kern,
        grid=(n_seq, n_qt),
        in_specs=in_specs,
        out_specs=out_specs,
        out_shape=out_shape,
        scratch_shapes=[pltpu.VMEM((seq_len, wq), BF16), pltpu.VMEM((n_ctx + seq_len + pad, wk), BF16),
                        pltpu.VMEM((n_ctx + seq_len + pad, wk), BF16), pltpu.VMEM((seq_len, wq), BF16)],
        compiler_params=_cparams(("arbitrary", "arbitrary")),
        name=name,
    )(*args)


def _rope_tables():
    t = np.arange(DEC_SEQ)
    pos = np.stack([t // GRID_W, t % GRID_W], axis=1).astype(np.float64)
    inv = (ROPE_THETA ** (-np.arange(ROPE_PAIRS, dtype=np.float32) / ROPE_PAIRS)).astype(np.float64)
    ang = pos[:, :, None] * inv[None, None, :]
    cos, sin = np.cos(ang), np.sin(ang)
    zero = np.zeros_like(sin)
    cos_t = np.concatenate([cos, cos], axis=2).reshape(DEC_SEQ, HEAD_DIM)
    s1_t = np.concatenate([-sin, zero], axis=2).reshape(DEC_SEQ, HEAD_DIM)
    s2_t = np.concatenate([zero, sin], axis=2).reshape(DEC_SEQ, HEAD_DIM)
    return tuple(jnp.asarray(np.tile(a, (1, LANES // HEAD_DIM)), F32) for a in (cos_t, s1_t, s2_t))


def _block_diag_mean():
    a = np.kron(np.eye(N_HEADS), np.full((HEAD_DIM, HEAD_DIM), 1.0 / HEAD_DIM))
    return jnp.asarray(a, BF16)


def _bias_kernel(rpb_ref, sel_ref, inside_ref, o_ref):
    v = jnp.dot(rpb_ref[...], sel_ref[...], precision=HIGHEST, preferred_element_type=F32)
    o_ref[...] = jnp.where(inside_ref[...] > 0.0, v, -1e30)


def _na_bias_table(rpb):
    n_rows, n_off = N_HEADS * (2 * WIN_R - 1), 2 * WIN_C - 1
    cols = np.arange(GRID_W)
    col_start = np.clip(cols - WIN_C // 2, 0, GRID_W - WIN_C)
    kc = np.arange(GRID_W)
    inside = (kc[None, :] >= col_start[:, None]) & (kc[None, :] < col_start[:, None] + WIN_C)
    off = np.clip(kc[None, :] - cols[:, None] + (WIN_C - 1), 0, n_off - 1)
    sel = (np.arange(LANES)[:, None] == off.reshape(1, -1)).astype(np.float32)
    rpb2 = jnp.pad(rpb.reshape(n_rows, n_off), ((0, 0), (0, LANES - n_off)))
    blocks = pl.pallas_call(
        _bias_kernel,
        out_shape=jax.ShapeDtypeStruct((n_rows, GRID_W * GRID_W), F32),
        name="na_bias",
    )(rpb2, jnp.asarray(sel), jnp.asarray(inside.reshape(1, -1).astype(np.float32)))
    blocks = blocks.reshape(N_HEADS, 2 * WIN_R - 1, GRID_W, GRID_W)
    blocks = jnp.pad(blocks, ((0, 0), (1, 2), (0, 0), (0, 0)), constant_values=-1e30)
    return jnp.concatenate([blocks[:, :-1], blocks[:, 1:]], axis=-1)


MOE_TM = 256
N_TILES = N_TOK // MOE_TM
CHUNK_SHIFT, BIG_SHIFT, PART_SHIFT, SUB_SHIFT = 3, 5, 7, 9
CHUNK = 1 << CHUNK_SHIFT
BIG = 1 << BIG_SHIFT
TILE_ROWS = 1280
SUB = 1 << SUB_SHIFT
PART = 1 << PART_SHIFT
ITEM_ROWS = 2048
MAX_ITEMS = 48
ITEM_TABLE = 64
MAX_ROWS = N_ASSIGN + N_TILES * N_EXPERTS * (CHUNK - 1)
BIG_LIST = 1024
SMALL_LIST = 2560
N_SEG =N_TILES * N_EXPERTS

assert TILE_ROWS >= MOE_TM * TOP_K + N_EXPERTS * (CHUNK - 1) and TILE_ROWS % MOE_TM == 0 and ITEM_ROWS % SUB == 0
assert MAX_ITEMS >= N_EXPERTS + (N_ASSIGN + N_SEG * (CHUNK - 1) - 1) // (ITEM_ROWS - MOE_TM)
assert BIG_LIST >= MAX_ROWS // BIG + MOE_TM // BIG and SMALL_LIST >= (N_SEG + 1) * (BIG // CHUNK - 1)
assert ITEM_ROWS <= 1 << 11


def _dispatch_kernel(*refs, n_x):
    mod_ref, rwt_ref, rbc_ref, upper_ref, lower_ref, xsb_ref, route_ref, tab_ref = refs[n_x:]
    h2 = _stream_tile(refs[:n_x], MOE_TM) * (1.0 + mod_ref[4:5, :]) + mod_ref[3:4, :]
    def split(v):
        hi = v.astype(BF16)
        return hi, (v - hi.astype(F32)).astype(BF16)

    def nt_dot(a, b):
        return lax.dot_general(a, b, (((1,), (1,)), ((), ())), preferred_element_type=F32)

    (w_hi, w_lo), (h_hi, h_lo) = split(rwt_ref[...]), split(h2)
    logits = (nt_dot(w_hi, h_hi) + nt_dot(w_hi, h_lo) + nt_dot(w_lo, h_hi)) + rbc_ref[...]
    sub = lax.broadcasted_iota(I32, (N_EXPERTS, MOE_TM), 0)
    vals, hots = [], []
    cur = logits
    for _ in range(TOP_K):
        m = jnp.max(cur, axis=0, keepdims=True)
        idx = jnp.min(jnp.where(cur == m, sub, N_EXPERTS), axis=0, keepdims=True)
        hot = sub == idx
        vals.append(m)
        hots.append(hot)
        cur = jnp.where(hot, -jnp.inf, cur)
    exps = [jnp.exp(v - vals[0]) for v in vals]
    den = (exps[0] + exps[1]) + (exps[2] + exps[3])
    mask = jnp.zeros((N_EXPERTS, MOE_TM), F32)
    for hot in hots:
        mask = mask + jnp.where(hot, 1.0, 0.0)
    before = jnp.dot(mask.astype(BF16), upper_ref[...], preferred_element_type=F32)
    n = jnp.sum(mask, axis=1, keepdims=True)
    p = (((n.astype(I32) + (CHUNK - 1)) >> CHUNK_SHIFT) << CHUNK_SHIFT).astype(F32)
    start = jnp.dot(lower_ref[...], jnp.broadcast_to(p, (N_EXPERTS, LANES)), precision=HIGHEST,
                    preferred_element_type=F32)[:, :1]
    base = start + before
    lps = [jnp.sum(jnp.where(hot, base, 0.0), axis=0, keepdims=True) for hot in hots]
    route_ref[...] = jnp.concatenate(lps + [e / den for e in exps], axis=0)
    lane = lax.broadcasted_iota(I32, (N_EXPERTS, LANES), 1)
    tab_ref[...] = jnp.where(lane == 0, n, jnp.where(lane == 1, p, jnp.where(lane == 2, start, 0.0)))
    jrow = lax.broadcasted_iota(I32, (TILE_ROWS, MOE_TM), 0)
    onehot = jnp.zeros((TILE_ROWS, MOE_TM), F32)
    for lp in lps:
        onehot = jnp.where(jrow == lp.astype(I32), 1.0, onehot)
    xs = jnp.dot(onehot.astype(BF16), h_hi, preferred_element_type=F32)
    xsb_ref[...] = pltpu.pack_elementwise([xs[:, :PACKED_W], xs[:, PACKED_W:]], packed_dtype=BF16)


def _dispatch(x1, mod, l, rw, rb):
    upper = jnp.asarray(np.triu(np.ones((MOE_TM, MOE_TM)), 1), BF16)
    lower = jnp.asarray(np.tril(np.ones((N_EXPERTS, N_EXPERTS)), -1), F32)
    x_specs, x_args = _stream_specs(x1, MOE_TM)
    return pl.pallas_call(
        functools.partial(_dispatch_kernel, n_x=len(x_args)),
        grid=(N_TILES,),
        in_specs=x_specs + [_mod_spec(l, MOE_TM), _full((N_EXPERTS, D_MODEL)), _full((N_EXPERTS, 1)),
                            _full((MOE_TM, MOE_TM)), _full((N_EXPERTS, N_EXPERTS))],
        out_specs=[pl.BlockSpec((None, TILE_ROWS, PACKED_W), lambda i: (i, 0, 0)),
                   pl.BlockSpec((None, 2 * TOP_K, MOE_TM), lambda i: (i, 0, 0)),
                   pl.BlockSpec((None, N_EXPERTS, LANES), lambda i: (i, 0, 0))],
        out_shape=[jax.ShapeDtypeStruct((N_TILES, TILE_ROWS, PACKED_W), I32),
                   jax.ShapeDtypeStruct((N_TILES, 2 * TOP_K, MOE_TM), F32),
                   jax.ShapeDtypeStruct((N_TILES, N_EXPERTS, LANES), F32)],
        compiler_params=_cparams(("arbitrary",)),
        name="moe_dispatch",
    )(*x_args, mod, rw.T, rb.reshape(N_EXPERTS, 1), upper, lower)


def _plan_kernel(p_ref, s_ref, zbig_hbm, zsmall_hbm, iexp_ref, irows_ref, ibig_ref, ismall_ref, big_ref, small_ref,
                 sem):
    clears = [pltpu.make_async_copy(zbig_hbm, big_ref, sem), pltpu.make_async_copy(zsmall_hbm, small_ref, sem)]
    for cp in clears:
        cp.start()
    for cp in clears:
        cp.wait()

    def open_item(it, e, n_big, n_small):
        iexp_ref[it] = e
        ibig_ref[it] = n_big
        ismall_ref[it] = n_small

    def per_expert(e, carry):
        it, last_e, n_big, n_small = carry
        open_item(it, e, n_big, n_small)

        def per_tile(b, c):
            it, rows, n_big, n_small = c
            pb = p_ref[b * N_EXPERTS + e]
            full = rows + pb > ITEM_ROWS

            @pl.when(full)
            def _():
                irows_ref[it] = rows
                open_item(it + 1, e, n_big, n_small)

            it = jnp.where(full, it + 1, it)
            rows = jnp.where(full, 0, rows)
            word = (b << 22) | (s_ref[b * N_EXPERTS + e] << 11) | rows
            nb = pb >> BIG_SHIFT
            ns = (pb - nb * BIG) >> CHUNK_SHIFT
            for k in range(MOE_TM // BIG):
                big_ref[n_big + k] = word + k * (BIG << 11 | BIG)
            for k in range(BIG // CHUNK - 1):
                small_ref[n_small + k] = word + nb * (BIG << 11 | BIG) + k * (CHUNK << 11 | CHUNK)
            return it, rows + pb, n_big + nb, n_small + ns

        it, rows, n_big, n_small = lax.fori_loop(0, N_TILES, per_tile, (it, jnp.int32(0), n_big, n_small))
        irows_ref[it] = rows
        used = rows > 0
        return jnp.where(used, it + 1, it), jnp.where(used, e, last_e), n_big, n_small

    zero = jnp.int32(0)
    n_items, last_e, n_big, n_small = lax.fori_loop(0, N_EXPERTS, per_expert, (zero, zero, zero, zero))

    def idle(it, c):
        open_item(it, last_e, n_big, n_small)
        irows_ref[it] = 0
        return c

    lax.fori_loop(n_items, ITEM_TABLE, idle, 0)


def _plan(ptab, stab):
    smem = pl.BlockSpec(memory_space=pltpu.SMEM)
    table = jax.ShapeDtypeStruct((ITEM_TABLE,), I32)
    return pl.pallas_call(
        _plan_kernel,
        in_specs=[smem, smem, pl.BlockSpec(memory_space=pl.ANY), pl.BlockSpec(memory_space=pl.ANY)],
        out_specs=[smem] * 6,
        out_shape=[table] * 4 + [jax.ShapeDtypeStruct((BIG_LIST,), I32), jax.ShapeDtypeStruct((SMALL_LIST,), I32)],
        scratch_shapes=[pltpu.SemaphoreType.DMA(())],
        name="moe_plan",
    )(ptab, stab, jnp.zeros((BIG_LIST,), I32), jnp.zeros((SMALL_LIST,), I32))


def _unpack_halves(p):
    lo = pltpu.unpack_elementwise(p, index=0, packed_dtype=BF16, unpacked_dtype=F32)
    hi = pltpu.unpack_elementwise(p, index=1, packed_dtype=BF16, unpacked_dtype=F32)
    return lo.astype(BF16), hi.astype(BF16)


def _unpack_rows(p):
    return jnp.concatenate(_unpack_halves(p), axis=1)


def _pack_rows(y):
    return pltpu.pack_elementwise([y[:, :PACKED_W], y[:, PACKED_W:]], packed_dtype=BF16)


def _expert_kernel(iexp_ref, irows_ref, ibig_ref, ismall_ref, big_ref, small_ref, xsb_hbm, w1_hbm, w2_hbm,
                   b1_ref, b2_ref, ysb_hbm, xg_ref, g_ref, a_ref, yb_ref, w_ref, cnt_ref,
                   in_big, in_small, out_big, out_small, w_sem, *, layer):
    it = pl.program_id(0)
    rows = irows_ref[it]
    slot = it & 1

    def weight_copies(item, phase):
        ee = iexp_ref[item]
        copies = []
        for k in range(2):
            half = pl.ds(k * (D_MODEL // 2), D_MODEL // 2)
            if phase < 2:
                src = w1_hbm.at[layer, ee, half, pl.ds(phase * D_FF, D_FF)]
            else:
                src = w2_hbm.at[layer, ee, half, :]
            copies.append(pltpu.make_async_copy(src, w_ref.at[phase, half, :], w_sem.at[phase]))
        return copies

    def start_weights(item, phase):
        @pl.when(irows_ref[item] > 0)
        def _():
            for cp in weight_copies(item, phase):
                cp.start()

    def wait_weights(phase):
        for cp in weight_copies(it, phase):
            cp.wait()

    def aligned(v):
        return v if isinstance(v, int) else pl.multiple_of(v, CHUNK)

    def copy(to_vmem, buf, b, src, dst, size):
        hbm = (xsb_hbm if to_vmem else ysb_hbm).at[b, pl.ds(aligned(src), size), :]
        if to_vmem:
            sem = in_big if size == BIG else in_small
            return pltpu.make_async_copy(hbm, xg_ref.at[buf, pl.ds(aligned(dst), size), :], sem)
        sem = out_big if size == BIG else out_small
        return pltpu.make_async_copy(yb_ref.at[pl.ds(aligned(dst), size), :], hbm, sem)

    def start_segments(item, to_vmem, buf):
        def start_list(lst_ref, lo, hi, size):
            def body(i, z):
                word = lst_ref[i]
                copy(to_vmem, buf, word >> 22, (word >> 11) & 2047, word & 2047, size).start()
                return z

            lax.fori_loop(lo, hi, body, 0)
            return hi - lo

        return (start_list(big_ref, ibig_ref[item], ibig_ref[item + 1], BIG),
                start_list(small_ref, ismall_ref[item], ismall_ref[item + 1], CHUNK))

    def wait_segments(to_vmem, n_big, n_small):
        def wait_big(c, z):
            copy(to_vmem, 0, 0, 0, 0, BIG).wait()
            return z

        def wait_small(c, z):
            copy(to_vmem, 0, 0, 0, 0, CHUNK).wait()
            return z

        lax.fori_loop(0, n_big, wait_big, 0)
        lax.fori_loop(0, n_small, wait_small, 0)

    @pl.when(it == 0)
    def _init():
        xg_ref[...] = jnp.zeros_like(xg_ref)
        start_weights(0, 0)
        start_weights(0, 1)
        cnt_ref[0], cnt_ref[1] = start_segments(0, True, 0)
        cnt_ref[2] = 0
        cnt_ref[3] = 0

    def for_sub_tiles(fn):
        n_full = rows >> SUB_SHIFT
        rem_parts = (rows - (n_full << SUB_SHIFT) + (PART - 1)) >> PART_SHIFT

        def body(i, c):
            fn(pl.ds(pl.multiple_of(i * SUB, SUB), SUB))
            return c

        lax.fori_loop(0, n_full, body, 0)
        r0 = pl.multiple_of(n_full * SUB, SUB)
        for k in range(1, SUB // PART + 1):
            @pl.when(rem_parts == k)
            def _(k=k):
                fn(pl.ds(r0, k * PART))

    def matmul(lo, hi, phase):
        half = D_MODEL // 2
        return (jnp.dot(lo, w_ref[phase, :half, :].astype(BF16), preferred_element_type=F32)
                + jnp.dot(hi, w_ref[phase, half:, :].astype(BF16), preferred_element_type=F32))

    @pl.when(rows > 0)
    def _item():
        start_weights(it, 2)
        wait_segments(True, cnt_ref[0], cnt_ref[1])
        wait_weights(0)

        def gate(r):
            lo, hi = _unpack_halves(xg_ref[slot, r, :])
            gate = jnp.minimum(matmul(lo, hi, 0) + b1_ref[:, :D_FF], SWIGLU_LIMIT)
            g_ref[r, :] = gate * (0.5 + 0.5 * jnp.tanh((0.5 * SWIGLU_ALPHA) * gate))

        for_sub_tiles(gate)

        start_weights(it + 1, 0)
        cnt_ref[0], cnt_ref[1] = start_segments(it + 1, True, 1 - slot)
        wait_weights(1)

        def up_act(r):
            lo, hi = _unpack_halves(xg_ref[slot, r, :])
            up = matmul(lo, hi, 1) + b1_ref[:, D_FF:]
            up = jnp.clip(up, -SWIGLU_LIMIT, SWIGLU_LIMIT)
            a_ref[r, :] = (g_ref[r, :] * (up + 1.0)).astype(BF16)

        for_sub_tiles(up_act)

        start_weights(it + 1, 1)
        wait_weights(2)
        wait_segments(False, cnt_ref[2], cnt_ref[3])

        def down(r):
            y = matmul(a_ref[r, :D_FF // 2], a_ref[r, D_FF // 2:], 2) + b2_ref[...]
            yb_ref[r, :] = _pack_rows(y)

        for_sub_tiles(down)
        cnt_ref[2], cnt_ref[3] = start_segments(it, False, 0)

    @pl.when(it == MAX_ITEMS - 1)
    def _drain():
        wait_segments(False, cnt_ref[2], cnt_ref[3])


def _experts(iexp, irows, ibig, ismall, big, small, xsb, l, w1, b1, w2, b2):
    n_prefetch = 6
    any_spec = pl.BlockSpec(memory_space=pl.ANY)
    return pl.pallas_call(
        functools.partial(_expert_kernel, layer=l),
        grid_spec=pltpu.PrefetchScalarGridSpec(
            num_scalar_prefetch=n_prefetch,
            grid=(MAX_ITEMS,),
            in_specs=[any_spec, any_spec, any_spec,
                      pl.BlockSpec((None, None, 1, 2 * D_FF), lambda it, ie, *_: (l, ie[it], 0, 0)),
                      pl.BlockSpec((None, None, 1, D_MODEL), lambda it, ie, *_: (l, ie[it], 0, 0))],
            out_specs=pl.BlockSpec(memory_space=pl.ANY),
            scratch_shapes=[pltpu.VMEM((2, ITEM_ROWS, PACKED_W), I32),
                            pltpu.VMEM((ITEM_ROWS, D_FF), F32),
                            pltpu.VMEM((ITEM_ROWS, D_FF), BF16),
                            pltpu.VMEM((ITEM_ROWS, PACKED_W), I32),
                            pltpu.VMEM((3, D_MODEL, D_FF), F32),
                            pltpu.SMEM((4,), I32),
                            pltpu.SemaphoreType.DMA(()),
                            pltpu.SemaphoreType.DMA(()),
                            pltpu.SemaphoreType.DMA(()),
                            pltpu.SemaphoreType.DMA(()),
                            pltpu.SemaphoreType.DMA((3,))],
        ),
        out_shape=jax.ShapeDtypeStruct((N_TILES, TILE_ROWS, PACKED_W), I32),
        input_output_aliases={n_prefetch: 0},
        compiler_params=_cparams(("arbitrary",)),
        name="moe_experts",
    )(iexp, irows, ibig, ismall, big, small, xsb, w1, w2,
      b1.reshape(DEPTH, N_EXPERTS, 1, 2 * D_FF), b2.reshape(DEPTH, N_EXPERTS, 1, D_MODEL))


def _combine_kernel(*refs, n_x, n_out):
    ysb_ref, route_ref, eye_ref, mod_ref, g_ref, b_ref = refs[n_x:len(refs) - n_out]
    o_refs = refs[len(refs) - n_out:]
    rt = lax.dot_general(eye_ref[...], route_ref[...], (((1,), (1,)), ((), ())), precision=HIGHEST,
                         preferred_element_type=F32)
    lane = lax.broadcasted_iota(I32, (MOE_TM, TILE_ROWS), 1)
    c = jnp.zeros((MOE_TM, TILE_ROWS), F32)
    for k in range(TOP_K):
        c = jnp.where(lane == rt[:, k:k + 1].astype(I32), rt[:, TOP_K + k:TOP_K + k + 1], c)
    f = jnp.dot(c.astype(BF16), _unpack_rows(ysb_ref[...]), preferred_element_type=F32)
    x = _stream_tile(refs[:n_x], MOE_TM)
    out = _layer_norm(DEEPNORM_ALPHA * x + mod_ref[5:6, :] * f, g_ref[...], b_ref[...])
    if n_out == 1:
        o_refs[0][...] = out
    else:
        is_ctx = pl.program_id(0) < N_PROMPT // MOE_TM

        @pl.when(is_ctx)
        def _():
            o_refs[0][...] = out

        @pl.when(jnp.logical_not(is_ctx))
        def _():
            o_refs[1][...] = out


def _combine(x1, ysb, route, mod, l, g, b, split_out):
    x_specs, x_args = _stream_specs(x1, MOE_TM)
    if split_out:
        out_specs, _ = _stream_specs((None, None), MOE_TM)
        out_shape = [jax.ShapeDtypeStruct((N_PROMPT, D_MODEL), F32), jax.ShapeDtypeStruct((N_SAMPLE, D_MODEL), F32)]
    else:
        out_specs, _ = _stream_specs(None, MOE_TM)
        out_shape = [jax.ShapeDtypeStruct((N_TOK, D_MODEL), F32)]
    return pl.pallas_call(
        functools.partial(_combine_kernel, n_x=len(x_args), n_out=len(out_shape)),
        grid=(N_TILES,),
        in_specs=x_specs + [pl.BlockSpec((None, TILE_ROWS, PACKED_W), lambda i: (i, 0, 0)),
                            pl.BlockSpec((None, 2 * TOP_K, MOE_TM), lambda i: (i, 0, 0)),
                            _full((MOE_TM, MOE_TM)), _mod_spec(l, MOE_TM), _full((1, D_MODEL)),
                            _full((1, D_MODEL))],
        out_specs=out_specs,
        out_shape=out_shape,
        compiler_params=_cparams(("arbitrary",)),
        name="moe_combine",
    )(*x_args, ysb, route, jnp.eye(MOE_TM, dtype=F32), mod, g.reshape(1, D_MODEL), b.reshape(1, D_MODEL))


def _moe_layer(x1, mod, l, rw, rb, w1, b1, w2, b2, ln_g, ln_b, split_out=False):
    xsb, route, tab = _dispatch(x1, mod, l, rw, rb)
    ptab = tab[:, :, 1].astype(I32).reshape(N_SEG)
    stab = tab[:, :, 2].astype(I32).reshape(N_SEG)
    iexp, irows, ibig, ismall, big, small = _plan(ptab, stab)
    ysb = _experts(iexp, irows, ibig, ismall, big, small, xsb, l, w1, b1, w2, b2)
    return _combine(x1, ysb, route, mod, l, ln_g, ln_b, split_out)


def kernel(x_prompt, x_sample, cache_k_attn, cache_v_attn, cache_k_na, cache_v_na, c, c_ctx, w_mod, b_mod, ln1_g, ln1_b, ln2_g, ln2_b, conv_w_in, conv_w, conv_b, conv_w_out, attn_w_qkv, attn_q_norm, attn_k_norm, attn_w_o, na_w_qkv, na_rpb, na_w_o, router_w, router_b, moe_w1, moe_b1, moe_w2, moe_b2):
    x = (x_prompt.reshape(N_PROMPT, D_MODEL), x_sample.reshape(N_SAMPLE, D_MODEL))
    mod = _adaln_all(jnp.concatenate([c_ctx[None, :], c], axis=0), w_mod, b_mod)
    new_kv = {}
    for l in range(DEPTH):
        kind, j = l % 3, l // 3
        if kind == 0:
            x1 = _conv_layer(x, mod, l, conv_w_in[j], conv_w[j], conv_b[j], conv_w_out[j], ln1_g[l], ln1_b[l])
        elif kind == 1:
            gq = jnp.tile(attn_q_norm[j], N_HEADS).reshape(1, N_HEADS * HEAD_DIM)
            gk = jnp.tile(attn_k_norm[j], N_KV_HEADS).reshape(1, N_KV_HEADS * HEAD_DIM)
            norm_args = (gq, gk, _block_diag_mean())
            wk = N_KV_HEADS * HEAD_DIM
            xp1, nk, nv = _attn_call(x, mod, l, False, attn_w_qkv[j], attn_w_o[j], ln1_g[l], ln1_b[l],
                                     n_seq=BATCH, seq_len=SEQ, tq=SEQ, n_kv=N_KV_HEADS, norm_args=norm_args,
                                     emit_kv=True, name="gqa_prompt")
            new_kv["k_attn"] = nk.reshape(BATCH, 1, SEQ, N_KV_HEADS, HEAD_DIM)
            new_kv["v_attn"] = nv.reshape(BATCH, 1, SEQ, N_KV_HEADS, HEAD_DIM)
            (xs1,) = _attn_call(x, mod, l, True, attn_w_qkv[j], attn_w_o[j], ln1_g[l], ln1_b[l],
                                n_seq=DEC_BATCH, seq_len=DEC_SEQ, tq=128, n_kv=N_KV_HEADS, norm_args=norm_args,
                                rope_args=_rope_tables(),
                                ctx_args=(cache_k_attn[:, j].reshape(DEC_BATCH, PAST_LEN, wk),
                                          cache_v_attn[:, j].reshape(DEC_BATCH, PAST_LEN, wk)),
                                name="gqa_sample")
            x1 = (xp1, xs1)
        else:
            wk = N_HEADS * HEAD_DIM
            xp1, nk, nv = _attn_call(x, mod, l, False, na_w_qkv[j], na_w_o[j], ln1_g[l], ln1_b[l],
                                     n_seq=BATCH, seq_len=SEQ, tq=SEQ, n_kv=N_HEADS, emit_kv=True,
                                     name="mha_prompt")
            new_kv["k_na"] = nk.reshape(BATCH, 1, SEQ, N_HEADS, HEAD_DIM)
            new_kv["v_na"] = nv.reshape(BATCH, 1, SEQ, N_HEADS, HEAD_DIM)
            (xs1,) = _attn_call(x, mod, l, True, na_w_qkv[j], na_w_o[j], ln1_g[l], ln1_b[l],
                                n_seq=DEC_BATCH, seq_len=DEC_SEQ, tq=2 * GRID_W, n_kv=N_HEADS,
                                ctx_args=(cache_k_na[:, j].reshape(DEC_BATCH, PAST_LEN, wk),
                                          cache_v_na[:, j].reshape(DEC_BATCH, PAST_LEN, wk)),
                                bias=_na_bias_table(na_rpb[j]), name="na_sample")
            x1 = (xp1, xs1)
        out = _moe_layer(x1, mod, l, router_w[l], router_b[l], moe_w1, moe_b1, moe_w2, moe_b2,
                         ln2_g[l], ln2_b[l], split_out=l == DEPTH - 1)
        x = out[0]
    y_prompt = out[0].reshape(BATCH, SEQ, D_MODEL)
    y_sample = out[1].reshape(DEC_BATCH, DEC_SEQ, D_MODEL)
    return (y_prompt, y_sample, new_kv["k_attn"], new_kv["v_attn"], new_kv["k_na"], new_kv["v_na"])
```

```python
import functools

import numpy as np
import jax
import jax.numpy as jnp
from jax import lax
from jax.experimental import pallas as pl
from jax.experimental.pallas import tpu as pltpu

D_MODEL = 1024
BATCH = 16
SEQ = 256
DEPTH = 4
DEC_BATCH = 2
DEC_SEQ = 1024
PAST_LEN = 256
GRID_W = 64
HEAD_DIM = 64
N_HEADS = 16
N_KV_HEADS = 4
ROPE_THETA = 10000.0
ROPE_PAIRS = HEAD_DIM // 4
WIN_R = 8
WIN_C = 16
N_EXPERTS = 32
TOP_K = 4
D_FF = D_MODEL
SWIGLU_LIMIT = 7.0
SWIGLU_ALPHA = 1.702
DEEPNORM_ALPHA = (2 * DEPTH) ** 0.25
LN_EPS = 1e-5
RMS_EPS = 1e-6

F32 = jnp.float32
BF16 = jnp.bfloat16
I32 = jnp.int32
HIGHEST = lax.Precision.HIGHEST

N_PROMPT = BATCH * SEQ
N_SAMPLE = DEC_BATCH * DEC_SEQ
N_TOK = N_PROMPT + N_SAMPLE
N_ASSIGN = N_TOK * TOP_K

LANES = 128
SUBLANES = 8
PACKED_W = D_MODEL // 2

VMEM_LIMIT = 58 * 1024 * 1024


def _cparams(sem):
    return pltpu.CompilerParams(dimension_semantics=sem, vmem_limit_bytes=VMEM_LIMIT)


def _layer_norm(x, g, b):
    mu = jnp.mean(x, -1, keepdims=True)
    xc = x - mu
    var = jnp.mean(xc * xc, -1, keepdims=True)
    return xc * lax.rsqrt(var + LN_EPS) * g + b


def _mod_row(i, tile):
    n_prompt_tiles = N_PROMPT // tile
    return jnp.where(i < n_prompt_tiles, 0, 1 + (i - n_prompt_tiles) // (DEC_SEQ // tile))


def _mod_spec(l, tile):
    return pl.BlockSpec((None, None, 6, D_MODEL), lambda i, *_: (l, _mod_row(i, tile), 0, 0))


def _full(shape):
    nd = len(shape)
    return pl.BlockSpec(shape, lambda *_: (0,) * nd)


def _stream_specs(x, tile):
    if not isinstance(x, tuple):
        return [pl.BlockSpec((tile, D_MODEL), lambda i, *_: (i, 0))], [x]
    n_a = N_PROMPT // tile
    return ([pl.BlockSpec((tile, D_MODEL), lambda i, *_: (jnp.minimum(i, n_a - 1), 0)),
             pl.BlockSpec((tile, D_MODEL), lambda i, *_: (jnp.maximum(i - n_a, 0), 0))], list(x))


def _stream_tile(x_refs, tile):
    if len(x_refs) == 1:
        return x_refs[0][...]
    return jnp.where(pl.program_id(0) < N_PROMPT // tile, x_refs[0][...], x_refs[1][...])


ADALN_TN = 1536


N_COND = 1 + DEC_BATCH
ADALN_UNROLL = 4


def _adaln_kernel(ct_ref, w_ref, b_ref, o_ref, s_ref):
    @pl.when((pl.program_id(0) == 0) & (pl.program_id(1) == 0))
    def _silu():
        c = ct_ref[...]
        s_ref[...] = c * jax.nn.sigmoid(c)

    def body(j, accs):
        accs = list(accs)
        for u in range(ADALN_UNROLL):
            rows = pl.ds(pl.multiple_of((j * ADALN_UNROLL + u) * SUBLANES, SUBLANES), SUBLANES)
            w = w_ref[rows, :]
            for r in range(N_COND):
                accs[r] = accs[r] + w * jnp.concatenate([s_ref[r, rows, :]] * (ADALN_TN // LANES), axis=1)
        return tuple(accs)

    zero = jnp.zeros((SUBLANES, ADALN_TN), F32)
    accs = lax.fori_loop(0, D_MODEL // (SUBLANES * ADALN_UNROLL), body, (zero,) * N_COND)
    rows = [jnp.sum(a, axis=0, keepdims=True) for a in accs]
    o_ref[...] = jnp.concatenate(rows + [jnp.zeros((8 - N_COND, ADALN_TN), F32)], axis=0) + b_ref[...]


def _adaln_all(cond, w_mod, b_mod):
    n = 6 * D_MODEL
    cond_t = jnp.broadcast_to(cond[:, :, None], (N_COND, D_MODEL, LANES))
    out = pl.pallas_call(
        _adaln_kernel,
        grid=(DEPTH, n // ADALN_TN),
        in_specs=[
            pl.BlockSpec((N_COND, D_MODEL, LANES), lambda l, j: (0, 0, 0)),
            pl.BlockSpec((None, D_MODEL, ADALN_TN), lambda l, j: (l, 0, j)),
            pl.BlockSpec((None, 1, ADALN_TN), lambda l, j: (l, 0, j)),
        ],
        out_specs=pl.BlockSpec((None, 8, ADALN_TN), lambda l, j: (l, 0, j)),
        out_shape=jax.ShapeDtypeStruct((DEPTH, 8, n), F32),
        scratch_shapes=[pltpu.VMEM((N_COND, D_MODEL, LANES), F32)],
        compiler_params=_cparams(("arbitrary", "arbitrary")),
        name="adaln",
    )(cond_t, w_mod, b_mod.reshape(DEPTH, 1, n))
    return out.reshape(DEPTH, 8, 6, D_MODEL)


CONV_TM = 1024


def _conv_kernel(*refs, n_x):
    mod_ref, win_ref, cw_ref, cb_ref, wout_ref, g_ref, b_ref, o_ref = refs[n_x:]
    i = pl.program_id(0)
    x = _stream_tile(refs[:n_x], CONV_TM)
    h = (x * (1.0 + mod_ref[1:2, :]) + mod_ref[0:1, :]).astype(BF16)
    gc = jnp.dot(h, win_ref[:, D_MODEL:2 * D_MODEL], preferred_element_type=F32)
    xv = jnp.dot(h, win_ref[:, 2 * D_MODEL:], preferred_element_type=F32)
    u = gc * xv
    seq_len = jnp.where(i < N_PROMPT // CONV_TM, SEQ, DEC_SEQ)
    t = lax.broadcasted_iota(I32, (CONV_TM, 1), 0) & (seq_len - 1)
    u_prev = jnp.where(t == 0, 0.0, pltpu.roll(u, 1, axis=0))
    u_next = jnp.where(t == seq_len - 1, 0.0, pltpu.roll(u, CONV_TM - 1, axis=0))
    y = u_prev * cw_ref[0:1, :] + u * cw_ref[1:2, :] + u_next * cw_ref[2:3, :] + cb_ref[...]
    gb = jnp.dot(h, win_ref[:, :D_MODEL], preferred_element_type=F32)
    v = (gb * y).astype(BF16)
    o = jnp.dot(v, wout_ref[...], preferred_element_type=F32)
    o_ref[...] = _layer_norm(DEEPNORM_ALPHA * x + mod_ref[2:3, :] * o, g_ref[...], b_ref[...])


def _conv_layer(x, mod, l, w_in, cw, cb, w_out, ln_g, ln_b):
    x_specs, x_args = _stream_specs(x, CONV_TM)
    return pl.pallas_call(
        functools.partial(_conv_kernel, n_x=len(x_args)),
        grid=(N_TOK // CONV_TM,),
        in_specs=x_specs + [_mod_spec(l, CONV_TM), _full((D_MODEL, 3 * D_MODEL)), _full((3, D_MODEL)),
                            _full((1, D_MODEL)), _full((D_MODEL, D_MODEL)), _full((1, D_MODEL)),
                            _full((1, D_MODEL))],
        out_specs=pl.BlockSpec((CONV_TM, D_MODEL), lambda i: (i, 0)),
        out_shape=jax.ShapeDtypeStruct((N_TOK, D_MODEL), F32),
        compiler_params=_cparams(("arbitrary",)),
        name="conv_mixer",
    )(*x_args, mod, w_in.astype(BF16), cw, cb.reshape(1, D_MODEL), w_out.astype(BF16),
      ln_g.reshape(1, D_MODEL), ln_b.reshape(1, D_MODEL))


ATTN_CHUNK = 256
NA_SLOTS = WIN_R + 2


def _attn_kernel(*refs, seq_len, tq, n_kv, norm, rope, n_ctx, na, emit_kv, head_rows):
    refs = list(refs)
    x_ref, mod_ref, wqkv_ref = refs[:3]
    pos = 3
    if norm:
        gq_ref, gk_ref, bd_ref = refs[pos:pos + 3]
        pos += 3
    if rope:
        cos_ref, s1_ref, s2_ref = refs[pos:pos + 3]
        pos += 3
    if n_ctx:
        ck_ref, cv_ref = refs[pos:pos + 2]
        pos += 2
    if na:
        bias_ref = refs[pos]
        pos += 1
    wo_ref, lng_ref, lnb_ref = refs[pos:pos + 3]
    pos += 3
    o_ref = refs[pos]
    pos += 1
    if emit_kv:
        nk_ref, nv_ref = refs[pos:pos + 2]
        pos += 2
    q_scr, k_scr, v_scr, o_scr = refs[pos:pos + 4]

    qt = pl.program_id(1)
    n_qt = seq_len // tq
    wq = N_HEADS * HEAD_DIM
    wk = n_kv * HEAD_DIM
    rep = N_HEADS // n_kv

    def rms(v, g_ref, width):
        ms = jnp.dot((v * v).astype(BF16), bd_ref[:width, :width], preferred_element_type=F32)
        return v * lax.rsqrt(ms + RMS_EPS) * g_ref[...]

    def rot(v, rows, width):
        def tab(ref):
            t = ref[rows, :]
            return jnp.concatenate([t] * (width // LANES), axis=1)

        return (v * tab(cos_ref) + pltpu.roll(v, width - ROPE_PAIRS, axis=1) * tab(s1_ref)
                + pltpu.roll(v, ROPE_PAIRS, axis=1) * tab(s2_ref))

    chunk = min(seq_len, ATTN_CHUNK)

    @pl.when(qt == 0)
    def _project():
        def body(ci, carry):
            r0 = pl.multiple_of(ci * chunk, chunk)
            rows = pl.ds(r0, chunk)
            h = (x_ref[rows, :] * (1.0 + mod_ref[1:2, :]) + mod_ref[0:1, :]).astype(BF16)
            q = jnp.dot(h, wqkv_ref[:, :wq], preferred_element_type=F32)
            k = jnp.dot(h, wqkv_ref[:, wq:wq + wk], preferred_element_type=F32)
            v = jnp.dot(h, wqkv_ref[:, wq + wk:], preferred_element_type=F32)
            if norm:
                q = rms(q, gq_ref, wq)
                k = rms(k, gk_ref, wk)
            if emit_kv and head_rows:
                for hh in range(n_kv):
                    dst = pl.ds(r0 * n_kv + hh, chunk, stride=n_kv)
                    nk_ref[dst, :] = k[:, hh * HEAD_DIM:(hh + 1) * HEAD_DIM]
                    nv_ref[dst, :] = v[:, hh * HEAD_DIM:(hh + 1) * HEAD_DIM]
            elif emit_kv:
                nk_ref[rows, :] = k
                nv_ref[rows, :] = v
            if rope:
                q = rot(q, rows, wq)
                k = rot(k, rows, wk)
            q_scr[rows, :] = (q * (HEAD_DIM ** -0.5)).astype(BF16)
            k_scr[pl.ds(n_ctx + r0, chunk), :] = k.astype(BF16)
            v_scr[pl.ds(n_ctx + r0, chunk), :] = v.astype(BF16)
            return carry

        lax.fori_loop(0, seq_len // chunk, body, 0)
        if n_ctx:
            k_scr[:n_ctx, :] = ck_ref[...].astype(BF16)
            v_scr[:n_ctx, :] = cv_ref[...].astype(BF16)
        if na:
            k_scr[n_ctx + seq_len:, :] = jnp.zeros((GRID_W, wk), BF16)
            v_scr[n_ctx + seq_len:, :] = jnp.zeros((GRID_W, wk), BF16)

    q0 = pl.multiple_of(qt * tq, tq)
    if na:
        n_rows = DEC_SEQ // GRID_W
        w = jnp.minimum(jnp.clip(2 * qt - WIN_R // 2, 0, n_rows - WIN_R), n_rows - WIN_R - 1)
        k0 = pl.multiple_of(n_ctx + w * GRID_W, GRID_W)
        slot_lane = lax.broadcasted_iota(I32, (GRID_W, NA_SLOTS * GRID_W), 1)
        pair_base, invalid = [], []
        for j in range(2):
            r = 2 * qt + j
            first = jnp.clip(r - WIN_R // 2, 0, n_rows - WIN_R) - w
            ok = (slot_lane >= first * GRID_W) & (slot_lane < (first + WIN_R) * GRID_W)
            invalid.append(jnp.where(ok, 0.0, -1e30))
            pair_base.append(w - r + WIN_R)
    for g in range(n_kv):
        heads = [g * rep + i for i in range(rep)]
        hd = heads[0]
        head_lanes = [slice(h * HEAD_DIM, (h + 1) * HEAD_DIM) for h in heads]
        gs = slice(g * HEAD_DIM, (g + 1) * HEAD_DIM)
        qh = jnp.concatenate([q_scr[pl.ds(q0, tq), hs] for hs in head_lanes], axis=0)
        dn = (((1,), (1,)), ((), ()))
        if na:
            kc, vc = k_scr[:n_ctx, gs], v_scr[:n_ctx, gs]
            kl, vl = k_scr[pl.ds(k0, NA_SLOTS * GRID_W), gs], v_scr[pl.ds(k0, NA_SLOTS * GRID_W), gs]
            bias = jnp.concatenate(
                [jnp.concatenate([bias_ref[hd, pl.ds(pair_base[j] + 2 * i, 1)][0] for i in range(NA_SLOTS // 2)],
                                 axis=1) + invalid[j] for j in range(2)], axis=0)
            s = jnp.concatenate([lax.dot_general(qh, kc, dn, preferred_element_type=F32),
                                 lax.dot_general(qh, kl, dn, preferred_element_type=F32) + bias], axis=1)
        else:
            s = lax.dot_general(qh, k_scr[:, gs], dn, preferred_element_type=F32)
        e = jnp.exp(s - jnp.max(s, axis=1, keepdims=True))
        den = jnp.sum(e, axis=1, keepdims=True)
        eb = e.astype(BF16)
        if na:
            oh = (jnp.dot(eb[:, :n_ctx], vc, preferred_element_type=F32)
                  + jnp.dot(eb[:, n_ctx:], vl, preferred_element_type=F32))
        else:
            oh = jnp.dot(eb, v_scr[:, gs], preferred_element_type=F32)
        oh = (oh / den).astype(BF16)
        for i, hs in enumerate(head_lanes):
            o_scr[pl.ds(q0, tq), hs] = oh[i * tq:(i + 1) * tq, :]

    @pl.when(qt == n_qt - 1)
    def _finish():
        def body(ci, carry):
            rows = pl.ds(pl.multiple_of(ci * chunk, chunk), chunk)
            o = jnp.dot(o_scr[rows, :], wo_ref[...], preferred_element_type=F32)
            o_ref[rows, :] = _layer_norm(DEEPNORM_ALPHA * x_ref[rows, :] + mod_ref[2:3, :] * o,
                                         lng_ref[...], lnb_ref[...])
            return carry

        lax.fori_loop(0, seq_len // chunk, body, 0)


def _attn_call(x, mod, l, latent, w_qkv, w_o, ln_g, ln_b, *, n_seq, seq_len, tq, n_kv, norm_args=None,
               rope_args=None, ctx_args=None, bias=None, emit_kv=False, head_rows=False, name="attn"):
    wq = N_HEADS * HEAD_DIM
    wk = n_kv * HEAD_DIM
    n_ctx = PAST_LEN if ctx_args is not None else 0
    n_qt = seq_len // tq
    seq_off = N_PROMPT // seq_len if latent else 0
    pad = GRID_W if bias is not None else 0
    seq_spec = pl.BlockSpec((seq_len, D_MODEL), lambda s, t: (s, 0))
    mod_spec = pl.BlockSpec((None, None, 6, D_MODEL), lambda s, t: (l, (1 + s) if latent else 0, 0, 0))
    in_specs = [pl.BlockSpec((seq_len, D_MODEL), lambda s, t: (s + seq_off, 0)), mod_spec,
                pl.BlockSpec((D_MODEL, wq + 2 * wk), lambda s, t: (0, 0))]
    args = [x, mod, w_qkv.astype(BF16)]
    if norm_args is not None:
        gq, gk, bd = norm_args
        in_specs += [pl.BlockSpec((1, wq), lambda s, t: (0, 0)), pl.BlockSpec((1, wk), lambda s, t: (0, 0)),
                     pl.BlockSpec((wq, wq), lambda s, t: (0, 0))]
        args += [gq, gk, bd]
    if rope_args is not None:
        in_specs += [pl.BlockSpec((seq_len, LANES), lambda s, t: (0, 0))] * 3
        args += list(rope_args)
    if ctx_args is not None:
        in_specs += [pl.BlockSpec((None, n_ctx, wk), lambda s, t: (s, 0, 0))] * 2
        args += list(ctx_args)
    if bias is not None:
        in_specs += [pl.BlockSpec(bias.shape, lambda s, t: (0, 0, 0, 0))]
        args += [bias]
    in_specs += [pl.BlockSpec((D_MODEL, D_MODEL), lambda s, t: (0, 0)),
                 pl.BlockSpec((1, D_MODEL), lambda s, t: (0, 0)), pl.BlockSpec((1, D_MODEL), lambda s, t: (0, 0))]
    args += [w_o.astype(BF16), ln_g.reshape(1, D_MODEL), ln_b.reshape(1, D_MODEL)]
    out_specs = [seq_spec]
    out_shape = [jax.ShapeDtypeStruct((n_seq * seq_len, D_MODEL), F32)]
    if emit_kv:
        kv_shape = (seq_len * n_kv, HEAD_DIM) if head_rows else (seq_len, wk)
        out_specs += [pl.BlockSpec(kv_shape, lambda s, t: (s, 0))] * 2
        out_shape += [jax.ShapeDtypeStruct((n_seq * kv_shape[0], kv_shape[1]), F32)] * 2
    kern = functools.partial(_attn_kernel, seq_len=seq_len, tq=tq, n_kv=n_kv, norm=norm_args is not None,
                             rope=rope_args is not None, n_ctx=n_ctx, na=bias is not None, emit_kv=emit_kv,
                             head_rows=head_rows)
    return pl.pallas_call(
        kern,
        grid=(n_seq, n_qt),
        in_specs=in_specs,
        out_specs=out_specs,
        out_shape=out_shape,
        scratch_shapes=[pltpu.VMEM((seq_len, wq), BF16), pltpu.VMEM((n_ctx + seq_len + pad, wk), BF16),
                        pltpu.VMEM((n_ctx + seq_len + pad, wk), BF16), pltpu.VMEM((seq_len, wq), BF16)],
        compiler_params=_cparams(("arbitrary", "arbitrary")),
        name=name,
    )(*args)


def _rope_tables():
    t = np.arange(DEC_SEQ)
    pos = np.stack([t // GRID_W, t % GRID_W], axis=1).astype(np.float64)
    inv = (ROPE_THETA ** (-np.arange(ROPE_PAIRS, dtype=np.float32) / ROPE_PAIRS)).astype(np.float64)
    ang = pos[:, :, None] * inv[None, None, :]
    cos, sin = np.cos(ang), np.sin(ang)
    zero = np.zeros_like(sin)
    cos_t = np.concatenate([cos, cos], axis=2).reshape(DEC_SEQ, HEAD_DIM)
    s1_t = np.concatenate([-sin, zero], axis=2).reshape(DEC_SEQ, HEAD_DIM)
    s2_t = np.concatenate([zero, sin], axis=2).reshape(DEC_SEQ, HEAD_DIM)
    return tuple(jnp.asarray(np.tile(a, (1, LANES // HEAD_DIM)), F32) for a in (cos_t, s1_t, s2_t))


def _block_diag_mean():
    a = np.kron(np.eye(N_HEADS), np.full((HEAD_DIM, HEAD_DIM), 1.0 / HEAD_DIM))
    return jnp.asarray(a, BF16)


def _bias_kernel(rpb_ref, sel_ref, inside_ref, o_ref):
    v = jnp.dot(rpb_ref[...], sel_ref[...], precision=HIGHEST, preferred_element_type=F32)
    o_ref[...] = jnp.where(inside_ref[...] > 0.0, v, -1e30)


def _na_bias_table(rpb):
    n_rows, n_off = N_HEADS * (2 * WIN_R - 1), 2 * WIN_C - 1
    cols = np.arange(GRID_W)
    col_start = np.clip(cols - WIN_C // 2, 0, GRID_W - WIN_C)
    kc = np.arange(GRID_W)
    inside = (kc[None, :] >= col_start[:, None]) & (kc[None, :] < col_start[:, None] + WIN_C)
    off = np.clip(kc[None, :] - cols[:, None] + (WIN_C - 1), 0, n_off - 1)
    sel = (np.arange(LANES)[:, None] == off.reshape(1, -1)).astype(np.float32)
    rpb2 = jnp.pad(rpb.reshape(n_rows, n_off), ((0, 0), (0, LANES - n_off)))
    blocks = pl.pallas_call(
        _bias_kernel,
        out_shape=jax.ShapeDtypeStruct((n_rows, GRID_W * GRID_W), F32),
        name="na_bias",
    )(rpb2, jnp.asarray(sel), jnp.asarray(inside.reshape(1, -1).astype(np.float32)))
    blocks = blocks.reshape(N_HEADS, 2 * WIN_R - 1, GRID_W, GRID_W)
    blocks = jnp.pad(blocks, ((0, 0), (1, 2), (0, 0), (0, 0)), constant_values=-1e30)
    return jnp.concatenate([blocks[:, :-1], blocks[:, 1:]], axis=-1)


MOE_TM = 256
N_TILES = N_TOK // MOE_TM
CHUNK_SHIFT, BIG_SHIFT, PART_SHIFT, SUB_SHIFT = 3, 5, 7, 9
CHUNK = 1 << CHUNK_SHIFT
BIG = 1 << BIG_SHIFT
TILE_ROWS = 1280
SUB = 1 << SUB_SHIFT
PART = 1 << PART_SHIFT
ITEM_ROWS = 2048
MAX_ITEMS = 48
ITEM_TABLE = 64
MAX_ROWS = N_ASSIGN + N_TILES * N_EXPERTS * (CHUNK - 1)
BIG_LIST = 1024
SMALL_LIST = 2560
N_SEG =N_TILES * N_EXPERTS

assert TILE_ROWS >= MOE_TM * TOP_K + N_EXPERTS * (CHUNK - 1) and TILE_ROWS % MOE_TM == 0 and ITEM_ROWS % SUB == 0
assert MAX_ITEMS >= N_EXPERTS + (N_ASSIGN + N_SEG * (CHUNK - 1) - 1) // (ITEM_ROWS - MOE_TM)
assert BIG_LIST >= MAX_ROWS // BIG + MOE_TM // BIG and SMALL_LIST >= (N_SEG + 1) * (BIG // CHUNK - 1)
assert ITEM_ROWS <= 1 << 11


def _dispatch_kernel(*refs, n_x):
    mod_ref, rwt_ref, rbc_ref, upper_ref, lower_ref, xsb_ref, route_ref, tab_ref = refs[n_x:]
    h2 = _stream_tile(refs[:n_x], MOE_TM) * (1.0 + mod_ref[4:5, :]) + mod_ref[3:4, :]
    def split(v):
        hi = v.astype(BF16)
        return hi, (v - hi.astype(F32)).astype(BF16)

    def nt_dot(a, b):
        return lax.dot_general(a, b, (((1,), (1,)), ((), ())), preferred_element_type=F32)

    (w_hi, w_lo), (h_hi, h_lo) = split(rwt_ref[...]), split(h2)
    logits = (nt_dot(w_hi, h_hi) + nt_dot(w_hi, h_lo) + nt_dot(w_lo, h_hi)) + rbc_ref[...]
    sub = lax.broadcasted_iota(I32, (N_EXPERTS, MOE_TM), 0)
    vals, hots = [], []
    cur = logits
    for _ in range(TOP_K):
        m = jnp.max(cur, axis=0, keepdims=True)
        idx = jnp.min(jnp.where(cur == m, sub, N_EXPERTS), axis=0, keepdims=True)
        hot = sub == idx
        vals.append(m)
        hots.append(hot)
        cur = jnp.where(hot, -jnp.inf, cur)
    exps = [jnp.exp(v - vals[0]) for v in vals]
    den = (exps[0] + exps[1]) + (exps[2] + exps[3])
    mask = jnp.zeros((N_EXPERTS, MOE_TM), F32)
    for hot in hots:
        mask = mask + jnp.where(hot, 1.0, 0.0)
    before = jnp.dot(mask.astype(BF16), upper_ref[...], preferred_element_type=F32)
    n = jnp.sum(mask, axis=1, keepdims=True)
    p = (((n.astype(I32) + (CHUNK - 1)) >> CHUNK_SHIFT) << CHUNK_SHIFT).astype(F32)
    start = jnp.dot(lower_ref[...], jnp.broadcast_to(p, (N_EXPERTS, LANES)), precision=HIGHEST,
                    preferred_element_type=F32)[:, :1]
    base = start + before
    lps = [jnp.sum(jnp.where(hot, base, 0.0), axis=0, keepdims=True) for hot in hots]
    route_ref[...] = jnp.concatenate(lps + [e / den for e in exps], axis=0)
    lane = lax.broadcasted_iota(I32, (N_EXPERTS, LANES), 1)
    tab_ref[...] = jnp.where(lane == 0, n, jnp.where(lane == 1, p, jnp.where(lane == 2, start, 0.0)))
    jrow = lax.broadcasted_iota(I32, (TILE_ROWS, MOE_TM), 0)
    onehot = jnp.zeros((TILE_ROWS, MOE_TM), F32)
    for lp in lps:
        onehot = jnp.where(jrow == lp.astype(I32), 1.0, onehot)
    xs = jnp.dot(onehot.astype(BF16), h_hi, preferred_element_type=F32)
    xsb_ref[...] = pltpu.pack_elementwise([xs[:, :PACKED_W], xs[:, PACKED_W:]], packed_dtype=BF16)


def _dispatch(x1, mod, l, rw, rb):
    upper = jnp.asarray(np.triu(np.ones((MOE_TM, MOE_TM)), 1), BF16)
    lower = jnp.asarray(np.tril(np.ones((N_EXPERTS, N_EXPERTS)), -1), F32)
    x_specs, x_args = _stream_specs(x1, MOE_TM)
    return pl.pallas_call(
        functools.partial(_dispatch_kernel, n_x=len(x_args)),
        grid=(N_TILES,),
        in_specs=x_specs + [_mod_spec(l, MOE_TM), _full((N_EXPERTS, D_MODEL)), _full((N_EXPERTS, 1)),
                            _full((MOE_TM, MOE_TM)), _full((N_EXPERTS, N_EXPERTS))],
        out_specs=[pl.BlockSpec((None, TILE_ROWS, PACKED_W), lambda i: (i, 0, 0)),
                   pl.BlockSpec((None, 2 * TOP_K, MOE_TM), lambda i: (i, 0, 0)),
                   pl.BlockSpec((None, N_EXPERTS, LANES), lambda i: (i, 0, 0))],
        out_shape=[jax.ShapeDtypeStruct((N_TILES, TILE_ROWS, PACKED_W), I32),
                   jax.ShapeDtypeStruct((N_TILES, 2 * TOP_K, MOE_TM), F32),
                   jax.ShapeDtypeStruct((N_TILES, N_EXPERTS, LANES), F32)],
        compiler_params=_cparams(("arbitrary",)),
        name="moe_dispatch",
    )(*x_args, mod, rw.T, rb.reshape(N_EXPERTS, 1), upper, lower)


def _plan_kernel(p_ref, s_ref, zbig_hbm, zsmall_hbm, iexp_ref, irows_ref, ibig_ref, ismall_ref, big_ref, small_ref,
                 sem):
    clears = [pltpu.make_async_copy(zbig_hbm, big_ref, sem), pltpu.make_async_copy(zsmall_hbm, small_ref, sem)]
    for cp in clears:
        cp.start()
    for cp in clears:
        cp.wait()

    def open_item(it, e, n_big, n_small):
        iexp_ref[it] = e
        ibig_ref[it] = n_big
        ismall_ref[it] = n_small

    def per_expert(e, carry):
        it, last_e, n_big, n_small = carry
        open_item(it, e, n_big, n_small)

        def per_tile(b, c):
            it, rows, n_big, n_small = c
            pb = p_ref[b * N_EXPERTS + e]
            full = rows + pb > ITEM_ROWS

            @pl.when(full)
            def _():
                irows_ref[it] = rows
                open_item(it + 1, e, n_big, n_small)

            it = jnp.where(full, it + 1, it)
            rows = jnp.where(full, 0, rows)
            word = (b << 22) | (s_ref[b * N_EXPERTS + e] << 11) | rows
            nb = pb >> BIG_SHIFT
            ns = (pb - nb * BIG) >> CHUNK_SHIFT
            for k in range(MOE_TM // BIG):
                big_ref[n_big + k] = word + k * (BIG << 11 | BIG)
            for k in range(BIG // CHUNK - 1):
                small_ref[n_small + k] = word + nb * (BIG << 11 | BIG) + k * (CHUNK << 11 | CHUNK)
            return it, rows + pb, n_big + nb, n_small + ns

        it, rows, n_big, n_small = lax.fori_loop(0, N_TILES, per_tile, (it, jnp.int32(0), n_big, n_small))
        irows_ref[it] = rows
        used = rows > 0
        return jnp.where(used, it + 1, it), jnp.where(used, e, last_e), n_big, n_small

    zero = jnp.int32(0)
    n_items, last_e, n_big, n_small = lax.fori_loop(0, N_EXPERTS, per_expert, (zero, zero, zero, zero))

    def idle(it, c):
        open_item(it, last_e, n_big, n_small)
        irows_ref[it] = 0
        return c

    lax.fori_loop(n_items, ITEM_TABLE, idle, 0)


def _plan(ptab, stab):
    smem = pl.BlockSpec(memory_space=pltpu.SMEM)
    table = jax.ShapeDtypeStruct((ITEM_TABLE,), I32)
    return pl.pallas_call(
        _plan_kernel,
        in_specs=[smem, smem, pl.BlockSpec(memory_space=pl.ANY), pl.BlockSpec(memory_space=pl.ANY)],
        out_specs=[smem] * 6,
        out_shape=[table] * 4 + [jax.ShapeDtypeStruct((BIG_LIST,), I32), jax.ShapeDtypeStruct((SMALL_LIST,), I32)],
        scratch_shapes=[pltpu.SemaphoreType.DMA(())],
        name="moe_plan",
    )(ptab, stab, jnp.zeros((BIG_LIST,), I32), jnp.zeros((SMALL_LIST,), I32))


def _unpack_halves(p):
    lo = pltpu.unpack_elementwise(p, index=0, packed_dtype=BF16, unpacked_dtype=F32)
    hi = pltpu.unpack_elementwise(p, index=1, packed_dtype=BF16, unpacked_dtype=F32)
    return lo.astype(BF16), hi.astype(BF16)


def _unpack_rows(p):
    return jnp.concatenate(_unpack_halves(p), axis=1)


def _pack_rows(y):
    return pltpu.pack_elementwise([y[:, :PACKED_W], y[:, PACKED_W:]], packed_dtype=BF16)


def _expert_kernel(iexp_ref, irows_ref, ibig_ref, ismall_ref, big_ref, small_ref, xsb_hbm, w1_hbm, w2_hbm,
                   b1_ref, b2_ref, ysb_hbm, xg_ref, g_ref, a_ref, yb_ref, w_ref, cnt_ref,
                   in_big, in_small, out_big, out_small, w_sem, *, layer):
    it = pl.program_id(0)
    rows = irows_ref[it]
    slot = it & 1

    def weight_copies(item, phase):
        ee = iexp_ref[item]
        copies = []
        for k in range(2):
            half = pl.ds(k * (D_MODEL // 2), D_MODEL // 2)
            if phase < 2:
                src = w1_hbm.at[layer, ee, half, pl.ds(phase * D_FF, D_FF)]
            else:
                src = w2_hbm.at[layer, ee, half, :]
            copies.append(pltpu.make_async_copy(src, w_ref.at[phase, half, :], w_sem.at[phase]))
        return copies

    def start_weights(item, phase):
        @pl.when(irows_ref[item] > 0)
        def _():
            for cp in weight_copies(item, phase):
                cp.start()

    def wait_weights(phase):
        for cp in weight_copies(it, phase):
            cp.wait()

    def aligned(v):
        return v if isinstance(v, int) else pl.multiple_of(v, CHUNK)

    def copy(to_vmem, buf, b, src, dst, size):
        hbm = (xsb_hbm if to_vmem else ysb_hbm).at[b, pl.ds(aligned(src), size), :]
        if to_vmem:
            sem = in_big if size == BIG else in_small
            return pltpu.make_async_copy(hbm, xg_ref.at[buf, pl.ds(aligned(dst), size), :], sem)
        sem = out_big if size == BIG else out_small
        return pltpu.make_async_copy(yb_ref.at[pl.ds(aligned(dst), size), :], hbm, sem)

    def start_segments(item, to_vmem, buf):
        def start_list(lst_ref, lo, hi, size):
            def body(i, z):
                word = lst_ref[i]
                copy(to_vmem, buf, word >> 22, (word >> 11) & 2047, word & 2047, size).start()
                return z

            lax.fori_loop(lo, hi, body, 0)
            return hi - lo

        return (start_list(big_ref, ibig_ref[item], ibig_ref[item + 1], BIG),
                start_list(small_ref, ismall_ref[item], ismall_ref[item + 1], CHUNK))

    def wait_segments(to_vmem, n_big, n_small):
        def wait_big(c, z):
            copy(to_vmem, 0, 0, 0, 0, BIG).wait()
            return z

        def wait_small(c, z):
            copy(to_vmem, 0, 0, 0, 0, CHUNK).wait()
            return z

        lax.fori_loop(0, n_big, wait_big, 0)
        lax.fori_loop(0, n_small, wait_small, 0)

    @pl.when(it == 0)
    def _init():
        xg_ref[...] = jnp.zeros_like(xg_ref)
        start_weights(0, 0)
        start_weights(0, 1)
        cnt_ref[0], cnt_ref[1] = start_segments(0, True, 0)
        cnt_ref[2] = 0
        cnt_ref[3] = 0

    def for_sub_tiles(fn):
        n_full = rows >> SUB_SHIFT
        rem_parts = (rows - (n_full << SUB_SHIFT) + (PART - 1)) >> PART_SHIFT

        def body(i, c):
            fn(pl.ds(pl.multiple_of(i * SUB, SUB), SUB))
            return c

        lax.fori_loop(0, n_full, body, 0)
        r0 = pl.multiple_of(n_full * SUB, SUB)
        for k in range(1, SUB // PART + 1):
            @pl.when(rem_parts == k)
            def _(k=k):
                fn(pl.ds(r0, k * PART))

    def matmul(lo, hi, phase):
        half = D_MODEL // 2
        return (jnp.dot(lo, w_ref[phase, :half, :].astype(BF16), preferred_element_type=F32)
                + jnp.dot(hi, w_ref[phase, half:, :].astype(BF16), preferred_element_type=F32))

    @pl.when(rows > 0)
    def _item():
        start_weights(it, 2)
        wait_segments(True, cnt_ref[0], cnt_ref[1])
        wait_weights(0)

        def gate(r):
            lo, hi = _unpack_halves(xg_ref[slot, r, :])
            gate = jnp.minimum(matmul(lo, hi, 0) + b1_ref[:, :D_FF], SWIGLU_LIMIT)
            g_ref[r, :] = gate * (0.5 + 0.5 * jnp.tanh((0.5 * SWIGLU_ALPHA) * gate))

        for_sub_tiles(gate)

        start_weights(it + 1, 0)
        cnt_ref[0], cnt_ref[1] = start_segments(it + 1, True, 1 - slot)
        wait_weights(1)

        def up_act(r):
            lo, hi = _unpack_halves(xg_ref[slot, r, :])
            up = matmul(lo, hi, 1) + b1_ref[:, D_FF:]
            up = jnp.clip(up, -SWIGLU_LIMIT, SWIGLU_LIMIT)
            a_ref[r, :] = (g_ref[r, :] * (up + 1.0)).astype(BF16)

        for_sub_tiles(up_act)

        start_weights(it + 1, 1)
        wait_weights(2)
        wait_segments(False, cnt_ref[2], cnt_ref[3])

        def down(r):
            y = matmul(a_ref[r, :D_FF // 2], a_ref[r, D_FF // 2:], 2) + b2_ref[...]
            yb_ref[r, :] = _pack_rows(y)

        for_sub_tiles(down)
        cnt_ref[2], cnt_ref[3] = start_segments(it, False, 0)

    @pl.when(it == MAX_ITEMS - 1)
    def _drain():
        wait_segments(False, cnt_ref[2], cnt_ref[3])


def _experts(iexp, irows, ibig, ismall, big, small, xsb, l, w1, b1, w2, b2):
    n_prefetch = 6
    any_spec = pl.BlockSpec(memory_space=pl.ANY)
    return pl.pallas_call(
        functools.partial(_expert_kernel, layer=l),
        grid_spec=pltpu.PrefetchScalarGridSpec(
            num_scalar_prefetch=n_prefetch,
            grid=(MAX_ITEMS,),
            in_specs=[any_spec, any_spec, any_spec,
                      pl.BlockSpec((None, None, 1, 2 * D_FF), lambda it, ie, *_: (l, ie[it], 0, 0)),
                      pl.BlockSpec((None, None, 1, D_MODEL), lambda it, ie, *_: (l, ie[it], 0, 0))],
            out_specs=pl.BlockSpec(memory_space=pl.ANY),
            scratch_shapes=[pltpu.VMEM((2, ITEM_ROWS, PACKED_W), I32),
                            pltpu.VMEM((ITEM_ROWS, D_FF), F32),
                            pltpu.VMEM((ITEM_ROWS, D_FF), BF16),
                            pltpu.VMEM((ITEM_ROWS, PACKED_W), I32),
                            pltpu.VMEM((3, D_MODEL, D_FF), F32),
                            pltpu.SMEM((4,), I32),
                            pltpu.SemaphoreType.DMA(()),
                            pltpu.SemaphoreType.DMA(()),
                            pltpu.SemaphoreType.DMA(()),
                            pltpu.SemaphoreType.DMA(()),
                            pltpu.SemaphoreType.DMA((3,))],
        ),
        out_shape=jax.ShapeDtypeStruct((N_TILES, TILE_ROWS, PACKED_W), I32),
        input_output_aliases={n_prefetch: 0},
        compiler_params=_cparams(("arbitrary",)),
        name="moe_experts",
    )(iexp, irows, ibig, ismall, big, small, xsb, w1, w2,
      b1.reshape(DEPTH, N_EXPERTS, 1, 2 * D_FF), b2.reshape(DEPTH, N_EXPERTS, 1, D_MODEL))


def _combine_kernel(*refs, n_x, n_out):
    ysb_ref, route_ref, eye_ref, mod_ref, g_ref, b_ref = refs[n_x:len(refs) - n_out]
    o_refs = refs[len(refs) - n_out:]
    rt = lax.dot_general(eye_ref[...], route_ref[...], (((1,), (1,)), ((), ())), precision=HIGHEST,
                         preferred_element_type=F32)
    lane = lax.broadcasted_iota(I32, (MOE_TM, TILE_ROWS), 1)
    c = jnp.zeros((MOE_TM, TILE_ROWS), F32)
    for k in range(TOP_K):
        c = jnp.where(lane == rt[:, k:k + 1].astype(I32), rt[:, TOP_K + k:TOP_K + k + 1], c)
    f = jnp.dot(c.astype(BF16), _unpack_rows(ysb_ref[...]), preferred_element_type=F32)
    x = _stream_tile(refs[:n_x], MOE_TM)
    out = _layer_norm(DEEPNORM_ALPHA * x + mod_ref[5:6, :] * f, g_ref[...], b_ref[...])
    if n_out == 1:
        o_refs[0][...] = out
    else:
        is_ctx = pl.program_id(0) < N_PROMPT // MOE_TM

        @pl.when(is_ctx)
        def _():
            o_refs[0][...] = out

        @pl.when(jnp.logical_not(is_ctx))
        def _():
            o_refs[1][...] = out


def _combine(x1, ysb, route, mod, l, g, b, split_out):
    x_specs, x_args = _stream_specs(x1, MOE_TM)
    if split_out:
        out_specs, _ = _stream_specs((None, None), MOE_TM)
        out_shape = [jax.ShapeDtypeStruct((N_PROMPT, D_MODEL), F32), jax.ShapeDtypeStruct((N_SAMPLE, D_MODEL), F32)]
    else:
        out_specs, _ = _stream_specs(None, MOE_TM)
        out_shape = [jax.ShapeDtypeStruct((N_TOK, D_MODEL), F32)]
    return pl.pallas_call(
        functools.partial(_combine_kernel, n_x=len(x_args), n_out=len(out_shape)),
        grid=(N_TILES,),
        in_specs=x_specs + [pl.BlockSpec((None, TILE_ROWS, PACKED_W), lambda i: (i, 0, 0)),
                            pl.BlockSpec((None, 2 * TOP_K, MOE_TM), lambda i: (i, 0, 0)),
                            _full((MOE_TM, MOE_TM)), _mod_spec(l, MOE_TM), _full((1, D_MODEL)),
                            _full((1, D_MODEL))],
        out_specs=out_specs,
        out_shape=out_shape,
        compiler_params=_cparams(("arbitrary",)),
        name="moe_combine",
    )(*x_args, ysb, route, jnp.eye(MOE_TM, dtype=F32), mod, g.reshape(1, D_MODEL), b.reshape(1, D_MODEL))


def _moe_layer(x1, mod, l, rw, rb, w1, b1, w2, b2, ln_g, ln_b, split_out=False):
    xsb, route, tab = _dispatch(x1, mod, l, rw, rb)
    ptab = tab[:, :, 1].astype(I32).reshape(N_SEG)
    stab = tab[:, :, 2].astype(I32).reshape(N_SEG)
    iexp, irows, ibig, ismall, big, small = _plan(ptab, stab)
    ysb = _experts(iexp, irows, ibig, ismall, big, small, xsb, l, w1, b1, w2, b2)
    return _combine(x1, ysb, route, mod, l, ln_g, ln_b, split_out)


def kernel(x_prompt, x_sample, cache_k_attn, cache_v_attn, cache_k_na, cache_v_na, c, c_ctx, w_mod, b_mod, ln1_g, ln1_b, ln2_g, ln2_b, conv_w_in, conv_w, conv_b, conv_w_out, attn_w_qkv, attn_q_norm, attn_k_norm, attn_w_o, na_w_qkv, na_rpb, na_w_o, router_w, router_b, moe_w1, moe_b1, moe_w2, moe_b2):
    x = (x_prompt.reshape(N_PROMPT, D_MODEL), x_sample.reshape(N_SAMPLE, D_MODEL))
    mod = _adaln_all(jnp.concatenate([c_ctx[None, :], c], axis=0), w_mod, b_mod)
    new_kv = {}
    for l in range(DEPTH):
        kind, j = l % 3, l // 3
        if kind == 0:
            x1 = _conv_layer(x, mod, l, conv_w_in[j], conv_w[j], conv_b[j], conv_w_out[j], ln1_g[l], ln1_b[l])
        elif kind == 1:
            gq = jnp.tile(attn_q_norm[j], N_HEADS).reshape(1, N_HEADS * HEAD_DIM)
            gk = jnp.tile(attn_k_norm[j], N_KV_HEADS).reshape(1, N_KV_HEADS * HEAD_DIM)
            norm_args = (gq, gk, _block_diag_mean())
            wk = N_KV_HEADS * HEAD_DIM
            xp1, nk, nv = _attn_call(x, mod, l, False, attn_w_qkv[j], attn_w_o[j], ln1_g[l], ln1_b[l],
                                     n_seq=BATCH, seq_len=SEQ, tq=SEQ, n_kv=N_KV_HEADS, norm_args=norm_args,
                                     emit_kv=True, name="gqa_prompt")
            new_kv["k_attn"] = nk.reshape(BATCH, 1, SEQ, N_KV_HEADS, HEAD_DIM)
            new_kv["v_attn"] = nv.reshape(BATCH, 1, SEQ, N_KV_HEADS, HEAD_DIM)
            (xs1,) = _attn_call(x, mod, l, True, attn_w_qkv[j], attn_w_o[j], ln1_g[l], ln1_b[l],
                                n_seq=DEC_BATCH, seq_len=DEC_SEQ, tq=128, n_kv=N_KV_HEADS, norm_args=norm_args,
                                rope_args=_rope_tables(),
                                ctx_args=(cache_k_attn[:, j].reshape(DEC_BATCH, PAST_LEN, wk),
                                          cache_v_attn[:, j].reshape(DEC_BATCH, PAST_LEN, wk)),
                                name="gqa_sample")
            x1 = (xp1, xs1)
        else:
            wk = N_HEADS * HEAD_DIM
            xp1, nk, nv = _attn_call(x, mod, l, False, na_w_qkv[j], na_w_o[j], ln1_g[l], ln1_b[l],
                                     n_seq=BATCH, seq_len=SEQ, tq=SEQ, n_kv=N_HEADS, emit_kv=True,
                                     head_rows=True, name="mha_prompt")
            new_kv["k_na"] = nk.reshape(BATCH, 1, SEQ, N_HEADS, HEAD_DIM)
            new_kv["v_na"] = nv.reshape(BATCH, 1, SEQ, N_HEADS, HEAD_DIM)
            (xs1,) = _attn_call(x, mod, l, True, na_w_qkv[j], na_w_o[j], ln1_g[l], ln1_b[l],
                                n_seq=DEC_BATCH, seq_len=DEC_SEQ, tq=2 * GRID_W, n_kv=N_HEADS,
                                ctx_args=(cache_k_na[:, j].reshape(DEC_BATCH, PAST_LEN, wk),
                                          cache_v_na[:, j].reshape(DEC_BATCH, PAST_LEN, wk)),
                                bias=_na_bias_table(na_rpb[j]), name="na_sample")
            x1 = (xp1, xs1)
        out = _moe_layer(x1, mod, l, router_w[l], router_b[l], moe_w1, moe_b1, moe_w2, moe_b2,
                         ln2_g[l], ln2_b[l], split_out=l == DEPTH - 1)
        x = out[0]
    y_prompt = out[0].reshape(BATCH, SEQ, D_MODEL)
    y_sample = out[1].reshape(DEC_BATCH, DEC_SEQ, D_MODEL)
    return (y_prompt, y_sample, new_kv["k_attn"], new_kv["v_attn"], new_kv["k_na"], new_kv["v_na"])
```

```python
import functools

import numpy as np
import jax
import jax.numpy as jnp
from jax import lax
from jax.experimental import pallas as pl
from jax.experimental.pallas import tpu as pltpu

D_MODEL = 1024
BATCH = 16
SEQ = 256
DEPTH = 4
DEC_BATCH = 2
DEC_SEQ = 1024
PAST_LEN = 256
GRID_W = 64
HEAD_DIM = 64
N_HEADS = 16
N_KV_HEADS = 4
ROPE_THETA = 10000.0
ROPE_PAIRS = HEAD_DIM // 4
WIN_R = 8
WIN_C = 16
N_EXPERTS = 32
TOP_K = 4
D_FF = D_MODEL
SWIGLU_LIMIT = 7.0
SWIGLU_ALPHA = 1.702
DEEPNORM_ALPHA = (2 * DEPTH) ** 0.25
LN_EPS = 1e-5
RMS_EPS = 1e-6

F32 = jnp.float32
BF16 = jnp.bfloat16
I32 = jnp.int32
HIGHEST = lax.Precision.HIGHEST

N_PROMPT = BATCH * SEQ
N_SAMPLE = DEC_BATCH * DEC_SEQ
N_TOK = N_PROMPT + N_SAMPLE
N_ASSIGN = N_TOK * TOP_K

LANES = 128
SUBLANES = 8
PACKED_W = D_MODEL // 2

VMEM_LIMIT = 58 * 1024 * 1024


def _cparams(sem):
    return pltpu.CompilerParams(dimension_semantics=sem, vmem_limit_bytes=VMEM_LIMIT)


def _layer_norm(x, g, b):
    mu = jnp.mean(x, -1, keepdims=True)
    xc = x - mu
    var = jnp.mean(xc * xc, -1, keepdims=True)
    return xc * lax.rsqrt(var + LN_EPS) * g + b


def _mod_row(i, tile):
    n_prompt_tiles = N_PROMPT // tile
    return jnp.where(i < n_prompt_tiles, 0, 1 + (i - n_prompt_tiles) // (DEC_SEQ // tile))


def _mod_spec(l, tile):
    return pl.BlockSpec((None, None, 6, D_MODEL), lambda i, *_: (l, _mod_row(i, tile), 0, 0))


def _full(shape):
    nd = len(shape)
    return pl.BlockSpec(shape, lambda *_: (0,) * nd)


def _stream_specs(x, tile):
    if not isinstance(x, tuple):
        return [pl.BlockSpec((tile, D_MODEL), lambda i, *_: (i, 0))], [x]
    n_a = N_PROMPT // tile
    return ([pl.BlockSpec((tile, D_MODEL), lambda i, *_: (jnp.minimum(i, n_a - 1), 0)),
             pl.BlockSpec((tile, D_MODEL), lambda i, *_: (jnp.maximum(i - n_a, 0), 0))], list(x))


def _stream_tile(x_refs, tile):
    if len(x_refs) == 1:
        return x_refs[0][...]
    return jnp.where(pl.program_id(0) < N_PROMPT // tile, x_refs[0][...], x_refs[1][...])


ADALN_TN = 1536


N_COND = 1 + DEC_BATCH
ADALN_UNROLL = 4


def _adaln_kernel(ct_ref, w_ref, b_ref, o_ref, s_ref):
    @pl.when((pl.program_id(0) == 0) & (pl.program_id(1) == 0))
    def _silu():
        c = ct_ref[...]
        s_ref[...] = c * jax.nn.sigmoid(c)

    def body(j, accs):
        accs = list(accs)
        for u in range(ADALN_UNROLL):
            rows = pl.ds(pl.multiple_of((j * ADALN_UNROLL + u) * SUBLANES, SUBLANES), SUBLANES)
            w = w_ref[rows, :]
            for r in range(N_COND):
                accs[r] = accs[r] + w * jnp.concatenate([s_ref[r, rows, :]] * (ADALN_TN // LANES), axis=1)
        return tuple(accs)

    zero = jnp.zeros((SUBLANES, ADALN_TN), F32)
    accs = lax.fori_loop(0, D_MODEL // (SUBLANES * ADALN_UNROLL), body, (zero,) * N_COND)
    rows = [jnp.sum(a, axis=0, keepdims=True) for a in accs]
    o_ref[...] = jnp.concatenate(rows + [jnp.zeros((8 - N_COND, ADALN_TN), F32)], axis=0) + b_ref[...]


def _adaln_all(cond, w_mod, b_mod):
    n = 6 * D_MODEL
    cond_t = jnp.broadcast_to(cond[:, :, None], (N_COND, D_MODEL, LANES))
    out = pl.pallas_call(
        _adaln_kernel,
        grid=(DEPTH, n // ADALN_TN),
        in_specs=[
            pl.BlockSpec((N_COND, D_MODEL, LANES), lambda l, j: (0, 0, 0)),
            pl.BlockSpec((None, D_MODEL, ADALN_TN), lambda l, j: (l, 0, j)),
            pl.BlockSpec((None, 1, ADALN_TN), lambda l, j: (l, 0, j)),
        ],
        out_specs=pl.BlockSpec((None, 8, ADALN_TN), lambda l, j: (l, 0, j)),
        out_shape=jax.ShapeDtypeStruct((DEPTH, 8, n), F32),
        scratch_shapes=[pltpu.VMEM((N_COND, D_MODEL, LANES), F32)],
        compiler_params=_cparams(("arbitrary", "arbitrary")),
        name="adaln",
    )(cond_t, w_mod, b_mod.reshape(DEPTH, 1, n))
    return out.reshape(DEPTH, 8, 6, D_MODEL)


CONV_TM = 1024


def _conv_kernel(*refs, n_x):
    mod_ref, win_ref, cw_ref, cb_ref, wout_ref, g_ref, b_ref, o_ref = refs[n_x:]
    i = pl.program_id(0)
    x = _stream_tile(refs[:n_x], CONV_TM)
    h = (x * (1.0 + mod_ref[1:2, :]) + mod_ref[0:1, :]).astype(BF16)
    gc = jnp.dot(h, win_ref[:, D_MODEL:2 * D_MODEL], preferred_element_type=F32)
    xv = jnp.dot(h, win_ref[:, 2 * D_MODEL:], preferred_element_type=F32)
    u = gc * xv
    seq_len = jnp.where(i < N_PROMPT // CONV_TM, SEQ, DEC_SEQ)
    t = lax.broadcasted_iota(I32, (CONV_TM, 1), 0) & (seq_len - 1)
    u_prev = jnp.where(t == 0, 0.0, pltpu.roll(u, 1, axis=0))
    u_next = jnp.where(t == seq_len - 1, 0.0, pltpu.roll(u, CONV_TM - 1, axis=0))
    y = u_prev * cw_ref[0:1, :] + u * cw_ref[1:2, :] + u_next * cw_ref[2:3, :] + cb_ref[...]
    gb = jnp.dot(h, win_ref[:, :D_MODEL], preferred_element_type=F32)
    v = (gb * y).astype(BF16)
    o = jnp.dot(v, wout_ref[...], preferred_element_type=F32)
    o_ref[...] = _layer_norm(DEEPNORM_ALPHA * x + mod_ref[2:3, :] * o, g_ref[...], b_ref[...])


def _conv_layer(x, mod, l, w_in, cw, cb, w_out, ln_g, ln_b):
    x_specs, x_args = _stream_specs(x, CONV_TM)
    return pl.pallas_call(
        functools.partial(_conv_kernel, n_x=len(x_args)),
        grid=(N_TOK // CONV_TM,),
        in_specs=x_specs + [_mod_spec(l, CONV_TM), _full((D_MODEL, 3 * D_MODEL)), _full((3, D_MODEL)),
                            _full((1, D_MODEL)), _full((D_MODEL, D_MODEL)), _full((1, D_MODEL)),
                            _full((1, D_MODEL))],
        out_specs=pl.BlockSpec((CONV_TM, D_MODEL), lambda i: (i, 0)),
        out_shape=jax.ShapeDtypeStruct((N_TOK, D_MODEL), F32),
        compiler_params=_cparams(("arbitrary",)),
        name="conv_mixer",
    )(*x_args, mod, w_in.astype(BF16), cw, cb.reshape(1, D_MODEL), w_out.astype(BF16),
      ln_g.reshape(1, D_MODEL), ln_b.reshape(1, D_MODEL))


ATTN_CHUNK = 256
NA_SLOTS = WIN_R + 2


def _attn_kernel(*refs, seq_len, tq, n_kv, norm, rope, n_ctx, na, emit_kv, head_rows):
    refs = list(refs)
    x_ref, mod_ref, wqkv_ref = refs[:3]
    pos = 3
    if norm:
        gq_ref, gk_ref, bd_ref = refs[pos:pos + 3]
        pos += 3
    if rope:
        cos_ref, s1_ref, s2_ref = refs[pos:pos + 3]
        pos += 3
    if n_ctx:
        ck_ref, cv_ref = refs[pos:pos + 2]
        pos += 2
    if na:
        bias_ref = refs[pos]
        pos += 1
    wo_ref, lng_ref, lnb_ref = refs[pos:pos + 3]
    pos += 3
    o_ref = refs[pos]
    pos += 1
    if emit_kv:
        nk_ref, nv_ref = refs[pos:pos + 2]
        pos += 2
    q_scr, k_scr, v_scr, o_scr = refs[pos:pos + 4]

    qt = pl.program_id(1)
    n_qt = seq_len // tq
    wq = N_HEADS * HEAD_DIM
    wk = n_kv * HEAD_DIM
    rep = N_HEADS // n_kv

    def rms(v, g_ref, width):
        ms = jnp.dot((v * v).astype(BF16), bd_ref[:width, :width], preferred_element_type=F32)
        return v * lax.rsqrt(ms + RMS_EPS) * g_ref[...]

    def rot(v, rows, width):
        def tab(ref):
            t = ref[rows, :]
            return jnp.concatenate([t] * (width // LANES), axis=1)

        return (v * tab(cos_ref) + pltpu.roll(v, width - ROPE_PAIRS, axis=1) * tab(s1_ref)
                + pltpu.roll(v, ROPE_PAIRS, axis=1) * tab(s2_ref))

    chunk = min(seq_len, ATTN_CHUNK)

    @pl.when(qt == 0)
    def _project():
        def body(ci, carry):
            r0 = pl.multiple_of(ci * chunk, chunk)
            rows = pl.ds(r0, chunk)
            h = (x_ref[rows, :] * (1.0 + mod_ref[1:2, :]) + mod_ref[0:1, :]).astype(BF16)
            q = jnp.dot(h, wqkv_ref[:, :wq], preferred_element_type=F32)
            k = jnp.dot(h, wqkv_ref[:, wq:wq + wk], preferred_element_type=F32)
            v = jnp.dot(h, wqkv_ref[:, wq + wk:], preferred_element_type=F32)
            if norm:
                q = rms(q, gq_ref, wq)
                k = rms(k, gk_ref, wk)
            if emit_kv and head_rows:
                for hh in range(n_kv):
                    dst = pl.ds(r0 * n_kv + hh, chunk, stride=n_kv)
                    nk_ref[dst, :] = k[:, hh * HEAD_DIM:(hh + 1) * HEAD_DIM]
                    nv_ref[dst, :] = v[:, hh * HEAD_DIM:(hh + 1) * HEAD_DIM]
            elif emit_kv:
                nk_ref[rows, :] = k
                nv_ref[rows, :] = v
            if rope:
                q = rot(q, rows, wq)
                k = rot(k, rows, wk)
            q_scr[rows, :] = (q * (HEAD_DIM ** -0.5)).astype(BF16)
            k_scr[pl.ds(n_ctx + r0, chunk), :] = k.astype(BF16)
            v_scr[pl.ds(n_ctx + r0, chunk), :] = v.astype(BF16)
            return carry

        lax.fori_loop(0, seq_len // chunk, body, 0)
        if n_ctx:
            k_scr[:n_ctx, :] = ck_ref[...].astype(BF16)
            v_scr[:n_ctx, :] = cv_ref[...].astype(BF16)
        if na:
            k_scr[n_ctx + seq_len:, :] = jnp.zeros((GRID_W, wk), BF16)
            v_scr[n_ctx + seq_len:, :] = jnp.zeros((GRID_W, wk), BF16)

    q0 = pl.multiple_of(qt * tq, tq)
    if na:
        n_rows = DEC_SEQ // GRID_W
        w = jnp.minimum(jnp.clip(2 * qt - WIN_R // 2, 0, n_rows - WIN_R), n_rows - WIN_R - 1)
        k0 = pl.multiple_of(n_ctx + w * GRID_W, GRID_W)
        slot_lane = lax.broadcasted_iota(I32, (GRID_W, NA_SLOTS * GRID_W), 1)
        pair_base, invalid = [], []
        for j in range(2):
            r = 2 * qt + j
            first = jnp.clip(r - WIN_R // 2, 0, n_rows - WIN_R) - w
            ok = (slot_lane >= first * GRID_W) & (slot_lane < (first + WIN_R) * GRID_W)
            invalid.append(jnp.where(ok, 0.0, -1e30))
            pair_base.append(w - r + WIN_R)
    for g in range(n_kv):
        heads = [g * rep + i for i in range(rep)]
        hd = heads[0]
        head_lanes = [slice(h * HEAD_DIM, (h + 1) * HEAD_DIM) for h in heads]
        gs = slice(g * HEAD_DIM, (g + 1) * HEAD_DIM)
        qh = jnp.concatenate([q_scr[pl.ds(q0, tq), hs] for hs in head_lanes], axis=0)
        dn = (((1,), (1,)), ((), ()))
        if na:
            kc, vc = k_scr[:n_ctx, gs], v_scr[:n_ctx, gs]
            kl, vl = k_scr[pl.ds(k0, NA_SLOTS * GRID_W), gs], v_scr[pl.ds(k0, NA_SLOTS * GRID_W), gs]
            bias = jnp.concatenate(
                [jnp.concatenate([bias_ref[hd, pl.ds(pair_base[j] + 2 * i, 1)][0] for i in range(NA_SLOTS // 2)],
                                 axis=1) + invalid[j] for j in range(2)], axis=0)
            s = jnp.concatenate([lax.dot_general(qh, kc, dn, preferred_element_type=F32),
                                 lax.dot_general(qh, kl, dn, preferred_element_type=F32) + bias], axis=1)
        else:
            s = lax.dot_general(qh, k_scr[:, gs], dn, preferred_element_type=F32)
        e = jnp.exp(s - jnp.max(s, axis=1, keepdims=True))
        den = jnp.sum(e, axis=1, keepdims=True)
        eb = e.astype(BF16)
        if na:
            oh = (jnp.dot(eb[:, :n_ctx], vc, preferred_element_type=F32)
                  + jnp.dot(eb[:, n_ctx:], vl, preferred_element_type=F32))
        else:
            oh = jnp.dot(eb, v_scr[:, gs], preferred_element_type=F32)
        oh = (oh / den).astype(BF16)
        for i, hs in enumerate(head_lanes):
            o_scr[pl.ds(q0, tq), hs] = oh[i * tq:(i + 1) * tq, :]

    @pl.when(qt == n_qt - 1)
    def _finish():
        def body(ci, carry):
            rows = pl.ds(pl.multiple_of(ci * chunk, chunk), chunk)
            o = jnp.dot(o_scr[rows, :], wo_ref[...], preferred_element_type=F32)
            o_ref[rows, :] = _layer_norm(DEEPNORM_ALPHA * x_ref[rows, :] + mod_ref[2:3, :] * o,
                                         lng_ref[...], lnb_ref[...])
            return carry

        lax.fori_loop(0, seq_len // chunk, body, 0)


def _attn_call(x, mod, l, latent, w_qkv, w_o, ln_g, ln_b, *, n_seq, seq_len, tq, n_kv, norm_args=None,
               rope_args=None, ctx_args=None, bias=None, emit_kv=False, head_rows=False, name="attn"):
    wq = N_HEADS * HEAD_DIM
    wk = n_kv * HEAD_DIM
    n_ctx = PAST_LEN if ctx_args is not None else 0
    n_qt = seq_len // tq
    seq_off = N_PROMPT // seq_len if latent else 0
    pad = GRID_W if bias is not None else 0
    seq_spec = pl.BlockSpec((seq_len, D_MODEL), lambda s, t: (s, 0))
    mod_spec = pl.BlockSpec((None, None, 6, D_MODEL), lambda s, t: (l, (1 + s) if latent else 0, 0, 0))
    in_specs = [pl.BlockSpec((seq_len, D_MODEL), lambda s, t: (s + seq_off, 0)), mod_spec,
                pl.BlockSpec((D_MODEL, wq + 2 * wk), lambda s, t: (0, 0))]
    args = [x, mod, w_qkv.astype(BF16)]
    if norm_args is not None:
        gq, gk, bd = norm_args
        in_specs += [pl.BlockSpec((1, wq), lambda s, t: (0, 0)), pl.BlockSpec((1, wk), lambda s, t: (0, 0)),
                     pl.BlockSpec((wq, wq), lambda s, t: (0, 0))]
        args += [gq, gk, bd]
    if rope_args is not None:
        in_specs += [pl.BlockSpec((seq_len, LANES), lambda s, t: (0, 0))] * 3
        args += list(rope_args)
    if ctx_args is not None:
        in_specs += [pl.BlockSpec((None, n_ctx, wk), lambda s, t: (s, 0, 0))] * 2
        args += list(ctx_args)
    if bias is not None:
        in_specs += [pl.BlockSpec(bias.shape, lambda s, t: (0, 0, 0, 0))]
        args += [bias]
    in_specs += [pl.BlockSpec((D_MODEL, D_MODEL), lambda s, t: (0, 0)),
                 pl.BlockSpec((1, D_MODEL), lambda s, t: (0, 0)), pl.BlockSpec((1, D_MODEL), lambda s, t: (0, 0))]
    args += [w_o.astype(BF16), ln_g.reshape(1, D_MODEL), ln_b.reshape(1, D_MODEL)]
    out_specs = [seq_spec]
    out_shape = [jax.ShapeDtypeStruct((n_seq * seq_len, D_MODEL), F32)]
    if emit_kv:
        kv_shape = (seq_len * n_kv, HEAD_DIM) if head_rows else (seq_len, wk)
        out_specs += [pl.BlockSpec(kv_shape, lambda s, t: (s, 0))] * 2
        out_shape += [jax.ShapeDtypeStruct((n_seq * kv_shape[0], kv_shape[1]), F32)] * 2
    kern = functools.partial(_attn_kernel, seq_len=seq_len, tq=tq, n_kv=n_kv, norm=norm_args is not None,
                             rope=rope_args is not None, n_ctx=n_ctx, na=bias is not None, emit_kv=emit_kv,
                             head_rows=head_rows)
    return pl.pallas_call(
        kern,
        grid=(n_seq, n_qt),
        in_specs=in_specs,
        out_specs=out_specs,
        out_shape=out_shape,
        scratch_shapes=[pltpu.VMEM((seq_len, wq), BF16), pltpu.VMEM((n_ctx + seq_len + pad, wk), BF16),
                        pltpu.VMEM((n_ctx + seq_len + pad, wk), BF16), pltpu.VMEM((seq_len, wq), BF16)],
        compiler_params=_cparams(("arbitrary", "arbitrary")),
        name=name,
    )(*args)


def _rope_tables():
    t = np.arange(DEC_SEQ)
    pos = np.stack([t // GRID_W, t % GRID_W], axis=1).astype(np.float64)
    inv = (ROPE_THETA ** (-np.arange(ROPE_PAIRS, dtype=np.float32) / ROPE_PAIRS)).astype(np.float64)
    ang = pos[:, :, None] * inv[None, None, :]
    cos, sin = np.cos(ang), np.sin(ang)
    zero = np.zeros_like(sin)
    cos_t = np.concatenate([cos, cos], axis=2).reshape(DEC_SEQ, HEAD_DIM)
    s1_t = np.concatenate([-sin, zero], axis=2).reshape(DEC_SEQ, HEAD_DIM)
    s2_t = np.concatenate([zero, sin], axis=2).reshape(DEC_SEQ, HEAD_DIM)
    return tuple(jnp.asarray(np.tile(a, (1, LANES // HEAD_DIM)), F32) for a in (cos_t, s1_t, s2_t))


def _block_diag_mean():
    a = np.kron(np.eye(N_HEADS), np.full((HEAD_DIM, HEAD_DIM), 1.0 / HEAD_DIM))
    return jnp.asarray(a, BF16)


def _bias_kernel(rows_ref, sel_ref, row_ok_ref, inside_ref, o_ref):
    v = jnp.dot(rows_ref[...], sel_ref[...], precision=HIGHEST, preferred_element_type=F32)
    lane = lax.broadcasted_iota(I32, v.shape, 1)
    row_ok = jnp.where((lane & (LANES - 1)) < GRID_W, row_ok_ref[:, 0:1], row_ok_ref[:, 1:2])
    o_ref[...] = jnp.where((inside_ref[...] > 0.0) & (row_ok > 0.0), v, -1e30)


def _na_bias_table(rpb):
    n_d, n_off, n_pair = 2 * WIN_R - 1, 2 * WIN_C - 1, 2 * WIN_R + 1
    cols = np.arange(GRID_W)
    col_start = np.clip(cols - WIN_C // 2, 0, GRID_W - WIN_C)
    kc = np.arange(GRID_W)
    inside = (kc[None, :] >= col_start[:, None]) & (kc[None, :] < col_start[:, None] + WIN_C)
    off = np.clip(kc[None, :] - cols[:, None] + (WIN_C - 1), 0, n_off - 1)
    sel = np.zeros((2, LANES, GRID_W, 2, GRID_W), np.float32)
    for side in range(2):
        sel[side, off, cols[:, None], side, kc[None, :]] = 1.0
    sel = sel.reshape(2 * LANES, GRID_W * LANES)
    inside2 = np.broadcast_to(inside[:, None, :], (GRID_W, 2, GRID_W)).reshape(1, GRID_W * LANES)
    d = np.arange(-1, n_pair - 1)
    row_ok = np.stack([(d >= 0) & (d < n_d), (d + 1 >= 0) & (d + 1 < n_d)], axis=1)
    row_ok = np.tile(np.pad(row_ok, ((0, 0), (0, LANES - 2))), (N_HEADS, 1)).astype(np.float32)
    padded = jnp.pad(rpb, ((0, 0), (1, 2), (0, LANES - n_off)))
    rows = jnp.concatenate([padded[:, :-1], padded[:, 1:]], axis=-1).reshape(N_HEADS * n_pair, 2 * LANES)
    table = pl.pallas_call(
        _bias_kernel,
        out_shape=jax.ShapeDtypeStruct((N_HEADS * n_pair, GRID_W * LANES), F32),
        compiler_params=pltpu.CompilerParams(vmem_limit_bytes=VMEM_LIMIT),
        name="na_bias",
    )(rows, jnp.asarray(sel), jnp.asarray(row_ok), jnp.asarray(inside2.astype(np.float32)))
    return table.reshape(N_HEADS, n_pair, GRID_W, LANES)


MOE_TM = 256
N_TILES = N_TOK // MOE_TM
CHUNK_SHIFT, BIG_SHIFT, PART_SHIFT, SUB_SHIFT = 3, 5, 7, 9
CHUNK = 1 << CHUNK_SHIFT
BIG = 1 << BIG_SHIFT
TILE_ROWS = 1280
SUB = 1 << SUB_SHIFT
PART = 1 << PART_SHIFT
ITEM_ROWS = 2048
MAX_ITEMS = 48
ITEM_TABLE = 64
MAX_ROWS = N_ASSIGN + N_TILES * N_EXPERTS * (CHUNK - 1)
BIG_LIST = 1024
SMALL_LIST = 2560
N_SEG =N_TILES * N_EXPERTS

assert TILE_ROWS >= MOE_TM * TOP_K + N_EXPERTS * (CHUNK - 1) and TILE_ROWS % MOE_TM == 0 and ITEM_ROWS % SUB == 0
assert MAX_ITEMS >= N_EXPERTS + (N_ASSIGN + N_SEG * (CHUNK - 1) - 1) // (ITEM_ROWS - MOE_TM)
assert BIG_LIST >= MAX_ROWS // BIG + MOE_TM // BIG and SMALL_LIST >= (N_SEG + 1) * (BIG // CHUNK - 1)
assert ITEM_ROWS <= 1 << 11


def _dispatch_kernel(*refs, n_x):
    mod_ref, rwt_ref, rbc_ref, upper_ref, lower_ref, xsb_ref, route_ref, tab_ref = refs[n_x:]
    h2 = _stream_tile(refs[:n_x], MOE_TM) * (1.0 + mod_ref[4:5, :]) + mod_ref[3:4, :]
    def split(v):
        hi = v.astype(BF16)
        return hi, (v - hi.astype(F32)).astype(BF16)

    def nt_dot(a, b):
        return lax.dot_general(a, b, (((1,), (1,)), ((), ())), preferred_element_type=F32)

    (w_hi, w_lo), (h_hi, h_lo) = split(rwt_ref[...]), split(h2)
    logits = (nt_dot(w_hi, h_hi) + nt_dot(w_hi, h_lo) + nt_dot(w_lo, h_hi)) + rbc_ref[...]
    sub = lax.broadcasted_iota(I32, (N_EXPERTS, MOE_TM), 0)
    vals, hots = [], []
    cur = logits
    for _ in range(TOP_K):
        m = jnp.max(cur, axis=0, keepdims=True)
        idx = jnp.min(jnp.where(cur == m, sub, N_EXPERTS), axis=0, keepdims=True)
        hot = sub == idx
        vals.append(m)
        hots.append(hot)
        cur = jnp.where(hot, -jnp.inf, cur)
    exps = [jnp.exp(v - vals[0]) for v in vals]
    den = (exps[0] + exps[1]) + (exps[2] + exps[3])
    mask = jnp.zeros((N_EXPERTS, MOE_TM), F32)
    for hot in hots:
        mask = mask + jnp.where(hot, 1.0, 0.0)
    before = jnp.dot(mask.astype(BF16), upper_ref[...], preferred_element_type=F32)
    n = jnp.sum(mask, axis=1, keepdims=True)
    p = (((n.astype(I32) + (CHUNK - 1)) >> CHUNK_SHIFT) << CHUNK_SHIFT).astype(F32)
    start = jnp.dot(lower_ref[...], jnp.broadcast_to(p, (N_EXPERTS, LANES)), precision=HIGHEST,
                    preferred_element_type=F32)[:, :1]
    base = start + before
    lps = [jnp.sum(jnp.where(hot, base, 0.0), axis=0, keepdims=True) for hot in hots]
    route_ref[...] = jnp.concatenate(lps + [e / den for e in exps], axis=0)
    lane = lax.broadcasted_iota(I32, (N_EXPERTS, LANES), 1)
    tab_ref[...] = jnp.where(lane == 0, n, jnp.where(lane == 1, p, jnp.where(lane == 2, start, 0.0)))
    jrow = lax.broadcasted_iota(I32, (TILE_ROWS, MOE_TM), 0)
    onehot = jnp.zeros((TILE_ROWS, MOE_TM), F32)
    for lp in lps:
        onehot = jnp.where(jrow == lp.astype(I32), 1.0, onehot)
    xs = jnp.dot(onehot.astype(BF16), h_hi, preferred_element_type=F32)
    xsb_ref[...] = pltpu.pack_elementwise([xs[:, :PACKED_W], xs[:, PACKED_W:]], packed_dtype=BF16)


def _dispatch(x1, mod, l, rw, rb):
    upper = jnp.asarray(np.triu(np.ones((MOE_TM, MOE_TM)), 1), BF16)
    lower = jnp.asarray(np.tril(np.ones((N_EXPERTS, N_EXPERTS)), -1), F32)
    x_specs, x_args = _stream_specs(x1, MOE_TM)
    return pl.pallas_call(
        functools.partial(_dispatch_kernel, n_x=len(x_args)),
        grid=(N_TILES,),
        in_specs=x_specs + [_mod_spec(l, MOE_TM), _full((N_EXPERTS, D_MODEL)), _full((N_EXPERTS, 1)),
                            _full((MOE_TM, MOE_TM)), _full((N_EXPERTS, N_EXPERTS))],
        out_specs=[pl.BlockSpec((None, TILE_ROWS, PACKED_W), lambda i: (i, 0, 0)),
                   pl.BlockSpec((None, 2 * TOP_K, MOE_TM), lambda i: (i, 0, 0)),
                   pl.BlockSpec((None, N_EXPERTS, LANES), lambda i: (i, 0, 0))],
        out_shape=[jax.ShapeDtypeStruct((N_TILES, TILE_ROWS, PACKED_W), I32),
                   jax.ShapeDtypeStruct((N_TILES, 2 * TOP_K, MOE_TM), F32),
                   jax.ShapeDtypeStruct((N_TILES, N_EXPERTS, LANES), F32)],
        compiler_params=_cparams(("arbitrary",)),
        name="moe_dispatch",
    )(*x_args, mod, rw.T, rb.reshape(N_EXPERTS, 1), upper, lower)


def _plan_kernel(p_ref, s_ref, zbig_hbm, zsmall_hbm, iexp_ref, irows_ref, ibig_ref, ismall_ref, big_ref, small_ref,
                 sem):
    clears = [pltpu.make_async_copy(zbig_hbm, big_ref, sem), pltpu.make_async_copy(zsmall_hbm, small_ref, sem)]
    for cp in clears:
        cp.start()
    for cp in clears:
        cp.wait()

    def open_item(it, e, n_big, n_small):
        iexp_ref[it] = e
        ibig_ref[it] = n_big
        ismall_ref[it] = n_small

    def per_expert(e, carry):
        it, last_e, n_big, n_small = carry
        open_item(it, e, n_big, n_small)

        def per_tile(b, c):
            it, rows, n_big, n_small = c
            pb = p_ref[b * N_EXPERTS + e]
            full = rows + pb > ITEM_ROWS

            @pl.when(full)
            def _():
                irows_ref[it] = rows
                open_item(it + 1, e, n_big, n_small)

            it = jnp.where(full, it + 1, it)
            rows = jnp.where(full, 0, rows)
            word = (b << 22) | (s_ref[b * N_EXPERTS + e] << 11) | rows
            nb = pb >> BIG_SHIFT
            ns = (pb - nb * BIG) >> CHUNK_SHIFT
            for k in range(MOE_TM // BIG):
                big_ref[n_big + k] = word + k * (BIG << 11 | BIG)
            for k in range(BIG // CHUNK - 1):
                small_ref[n_small + k] = word + nb * (BIG << 11 | BIG) + k * (CHUNK << 11 | CHUNK)
            return it, rows + pb, n_big + nb, n_small + ns

        it, rows, n_big, n_small = lax.fori_loop(0, N_TILES, per_tile, (it, jnp.int32(0), n_big, n_small))
        irows_ref[it] = rows
        used = rows > 0
        return jnp.where(used, it + 1, it), jnp.where(used, e, last_e), n_big, n_small

    zero = jnp.int32(0)
    n_items, last_e, n_big, n_small = lax.fori_loop(0, N_EXPERTS, per_expert, (zero, zero, zero, zero))

    def idle(it, c):
        open_item(it, last_e, n_big, n_small)
        irows_ref[it] = 0
        return c

    lax.fori_loop(n_items, ITEM_TABLE, idle, 0)


def _plan(ptab, stab):
    smem = pl.BlockSpec(memory_space=pltpu.SMEM)
    table = jax.ShapeDtypeStruct((ITEM_TABLE,), I32)
    return pl.pallas_call(
        _plan_kernel,
        in_specs=[smem, smem, pl.BlockSpec(memory_space=pl.ANY), pl.BlockSpec(memory_space=pl.ANY)],
        out_specs=[smem] * 6,
        out_shape=[table] * 4 + [jax.ShapeDtypeStruct((BIG_LIST,), I32), jax.ShapeDtypeStruct((SMALL_LIST,), I32)],
        scratch_shapes=[pltpu.SemaphoreType.DMA(())],
        name="moe_plan",
    )(ptab, stab, jnp.zeros((BIG_LIST,), I32), jnp.zeros((SMALL_LIST,), I32))


def _unpack_halves(p):
    lo = pltpu.unpack_elementwise(p, index=0, packed_dtype=BF16, unpacked_dtype=F32)
    hi = pltpu.unpack_elementwise(p, index=1, packed_dtype=BF16, unpacked_dtype=F32)
    return lo.astype(BF16), hi.astype(BF16)


def _unpack_rows(p):
    return jnp.concatenate(_unpack_halves(p), axis=1)


def _pack_rows(y):
    return pltpu.pack_elementwise([y[:, :PACKED_W], y[:, PACKED_W:]], packed_dtype=BF16)


def _expert_kernel(iexp_ref, irows_ref, ibig_ref, ismall_ref, big_ref, small_ref, xsb_hbm, w1_hbm, w2_hbm,
                   b1_ref, b2_ref, ysb_hbm, xg_ref, g_ref, a_ref, yb_ref, w_ref, cnt_ref,
                   in_big, in_small, out_big, out_small, w_sem, *, layer):
    it = pl.program_id(0)
    rows = irows_ref[it]
    slot = it & 1

    def weight_copies(item, phase):
        ee = iexp_ref[item]
        copies = []
        for k in range(2):
            half = pl.ds(k * (D_MODEL // 2), D_MODEL // 2)
            if phase < 2:
                src = w1_hbm.at[layer, ee, half, pl.ds(phase * D_FF, D_FF)]
            else:
                src = w2_hbm.at[layer, ee, half, :]
            copies.append(pltpu.make_async_copy(src, w_ref.at[phase, half, :], w_sem.at[phase]))
        return copies

    def start_weights(item, phase):
        @pl.when(irows_ref[item] > 0)
        def _():
            for cp in weight_copies(item, phase):
                cp.start()

    def wait_weights(phase):
        for cp in weight_copies(it, phase):
            cp.wait()

    def aligned(v):
        return v if isinstance(v, int) else pl.multiple_of(v, CHUNK)

    def copy(to_vmem, buf, b, src, dst, size):
        hbm = (xsb_hbm if to_vmem else ysb_hbm).at[b, pl.ds(aligned(src), size), :]
        if to_vmem:
            sem = in_big if size == BIG else in_small
            return pltpu.make_async_copy(hbm, xg_ref.at[buf, pl.ds(aligned(dst), size), :], sem)
        sem = out_big if size == BIG else out_small
        return pltpu.make_async_copy(yb_ref.at[pl.ds(aligned(dst), size), :], hbm, sem)

    def start_segments(item, to_vmem, buf):
        def start_list(lst_ref, lo, hi, size):
            def body(i, z):
                word = lst_ref[i]
                copy(to_vmem, buf, word >> 22, (word >> 11) & 2047, word & 2047, size).start()
                return z

            lax.fori_loop(lo, hi, body, 0)
            return hi - lo

        return (start_list(big_ref, ibig_ref[item], ibig_ref[item + 1], BIG),
                start_list(small_ref, ismall_ref[item], ismall_ref[item + 1], CHUNK))

    def wait_segments(to_vmem, n_big, n_small):
        def wait_big(c, z):
            copy(to_vmem, 0, 0, 0, 0, BIG).wait()
            return z

        def wait_small(c, z):
            copy(to_vmem, 0, 0, 0, 0, CHUNK).wait()
            return z

        lax.fori_loop(0, n_big, wait_big, 0)
        lax.fori_loop(0, n_small, wait_small, 0)

    @pl.when(it == 0)
    def _init():
        xg_ref[...] = jnp.zeros_like(xg_ref)
        start_weights(0, 0)
        start_weights(0, 1)
        cnt_ref[0], cnt_ref[1] = start_segments(0, True, 0)
        cnt_ref[2] = 0
        cnt_ref[3] = 0

    def for_sub_tiles(fn):
        n_full = rows >> SUB_SHIFT
        rem_parts = (rows - (n_full << SUB_SHIFT) + (PART - 1)) >> PART_SHIFT

        def body(i, c):
            fn(pl.ds(pl.multiple_of(i * SUB, SUB), SUB))
            return c

        lax.fori_loop(0, n_full, body, 0)
        r0 = pl.multiple_of(n_full * SUB, SUB)
        for k in range(1, SUB // PART + 1):
            @pl.when(rem_parts == k)
            def _(k=k):
                fn(pl.ds(r0, k * PART))

    def matmul(lo, hi, phase):
        half = D_MODEL // 2
        return (jnp.dot(lo, w_ref[phase, :half, :].astype(BF16), preferred_element_type=F32)
                + jnp.dot(hi, w_ref[phase, half:, :].astype(BF16), preferred_element_type=F32))

    @pl.when(rows > 0)
    def _item():
        start_weights(it, 2)
        wait_segments(True, cnt_ref[0], cnt_ref[1])
        wait_weights(0)

        def gate(r):
            lo, hi = _unpack_halves(xg_ref[slot, r, :])
            gate = jnp.minimum(matmul(lo, hi, 0) + b1_ref[:, :D_FF], SWIGLU_LIMIT)
            g_ref[r, :] = gate * (0.5 + 0.5 * jnp.tanh((0.5 * SWIGLU_ALPHA) * gate))

        for_sub_tiles(gate)

        start_weights(it + 1, 0)
        cnt_ref[0], cnt_ref[1] = start_segments(it + 1, True, 1 - slot)
        wait_weights(1)

        def up_act(r):
            lo, hi = _unpack_halves(xg_ref[slot, r, :])
            up = matmul(lo, hi, 1) + b1_ref[:, D_FF:]
            up = jnp.clip(up, -SWIGLU_LIMIT, SWIGLU_LIMIT)
            a_ref[r, :] = (g_ref[r, :] * (up + 1.0)).astype(BF16)

        for_sub_tiles(up_act)

        start_weights(it + 1, 1)
        wait_weights(2)
        wait_segments(False, cnt_ref[2], cnt_ref[3])

        def down(r):
            y = matmul(a_ref[r, :D_FF // 2], a_ref[r, D_FF // 2:], 2) + b2_ref[...]
            yb_ref[r, :] = _pack_rows(y)

        for_sub_tiles(down)
        cnt_ref[2], cnt_ref[3] = start_segments(it, False, 0)

    @pl.when(it == MAX_ITEMS - 1)
    def _drain():
        wait_segments(False, cnt_ref[2], cnt_ref[3])


def _experts(iexp, irows, ibig, ismall, big, small, xsb, l, w1, b1, w2, b2):
    n_prefetch = 6
    any_spec = pl.BlockSpec(memory_space=pl.ANY)
    return pl.pallas_call(
        functools.partial(_expert_kernel, layer=l),
        grid_spec=pltpu.PrefetchScalarGridSpec(
            num_scalar_prefetch=n_prefetch,
            grid=(MAX_ITEMS,),
            in_specs=[any_spec, any_spec, any_spec,
                      pl.BlockSpec((None, None, 1, 2 * D_FF), lambda it, ie, *_: (l, ie[it], 0, 0)),
                      pl.BlockSpec((None, None, 1, D_MODEL), lambda it, ie, *_: (l, ie[it], 0, 0))],
            out_specs=pl.BlockSpec(memory_space=pl.ANY),
            scratch_shapes=[pltpu.VMEM((2, ITEM_ROWS, PACKED_W), I32),
                            pltpu.VMEM((ITEM_ROWS, D_FF), F32),
                            pltpu.VMEM((ITEM_ROWS, D_FF), BF16),
                            pltpu.VMEM((ITEM_ROWS, PACKED_W), I32),
                            pltpu.VMEM((3, D_MODEL, D_FF), F32),
                            pltpu.SMEM((4,), I32),
                            pltpu.SemaphoreType.DMA(()),
                            pltpu.SemaphoreType.DMA(()),
                            pltpu.SemaphoreType.DMA(()),
                            pltpu.SemaphoreType.DMA(()),
                            pltpu.SemaphoreType.DMA((3,))],
        ),
        out_shape=jax.ShapeDtypeStruct((N_TILES, TILE_ROWS, PACKED_W), I32),
        input_output_aliases={n_prefetch: 0},
        compiler_params=_cparams(("arbitrary",)),
        name="moe_experts",
    )(iexp, irows, ibig, ismall, big, small, xsb, w1, w2,
      b1.reshape(DEPTH, N_EXPERTS, 1, 2 * D_FF), b2.reshape(DEPTH, N_EXPERTS, 1, D_MODEL))


def _combine_kernel(*refs, n_x, n_out):
    ysb_ref, route_ref, eye_ref, mod_ref, g_ref, b_ref = refs[n_x:len(refs) - n_out]
    o_refs = refs[len(refs) - n_out:]
    rt = lax.dot_general(eye_ref[...], route_ref[...], (((1,), (1,)), ((), ())), precision=HIGHEST,
                         preferred_element_type=F32)
    lane = lax.broadcasted_iota(I32, (MOE_TM, TILE_ROWS), 1)
    c = jnp.zeros((MOE_TM, TILE_ROWS), F32)
    for k in range(TOP_K):
        c = jnp.where(lane == rt[:, k:k + 1].astype(I32), rt[:, TOP_K + k:TOP_K + k + 1], c)
    f = jnp.dot(c.astype(BF16), _unpack_rows(ysb_ref[...]), preferred_element_type=F32)
    x = _stream_tile(refs[:n_x], MOE_TM)
    out = _layer_norm(DEEPNORM_ALPHA * x + mod_ref[5:6, :] * f, g_ref[...], b_ref[...])
    if n_out == 1:
        o_refs[0][...] = out
    else:
        is_ctx = pl.program_id(0) < N_PROMPT // MOE_TM

        @pl.when(is_ctx)
        def _():
            o_refs[0][...] = out

        @pl.when(jnp.logical_not(is_ctx))
        def _():
            o_refs[1][...] = out


def _combine(x1, ysb, route, mod, l, g, b, split_out):
    x_specs, x_args = _stream_specs(x1, MOE_TM)
    if split_out:
        out_specs, _ = _stream_specs((None, None), MOE_TM)
        out_shape = [jax.ShapeDtypeStruct((N_PROMPT, D_MODEL), F32), jax.ShapeDtypeStruct((N_SAMPLE, D_MODEL), F32)]
    else:
        out_specs, _ = _stream_specs(None, MOE_TM)
        out_shape = [jax.ShapeDtypeStruct((N_TOK, D_MODEL), F32)]
    return pl.pallas_call(
        functools.partial(_combine_kernel, n_x=len(x_args), n_out=len(out_shape)),
        grid=(N_TILES,),
        in_specs=x_specs + [pl.BlockSpec((None, TILE_ROWS, PACKED_W), lambda i: (i, 0, 0)),
                            pl.BlockSpec((None, 2 * TOP_K, MOE_TM), lambda i: (i, 0, 0)),
                            _full((MOE_TM, MOE_TM)), _mod_spec(l, MOE_TM), _full((1, D_MODEL)),
                            _full((1, D_MODEL))],
        out_specs=out_specs,
        out_shape=out_shape,
        compiler_params=_cparams(("arbitrary",)),
        name="moe_combine",
    )(*x_args, ysb, route, jnp.eye(MOE_TM, dtype=F32), mod, g.reshape(1, D_MODEL), b.reshape(1, D_MODEL))


def _moe_layer(x1, mod, l, rw, rb, w1, b1, w2, b2, ln_g, ln_b, split_out=False):
    xsb, route, tab = _dispatch(x1, mod, l, rw, rb)
    ptab = tab[:, :, 1].astype(I32).reshape(N_SEG)
    stab = tab[:, :, 2].astype(I32).reshape(N_SEG)
    iexp, irows, ibig, ismall, big, small = _plan(ptab, stab)
    ysb = _experts(iexp, irows, ibig, ismall, big, small, xsb, l, w1, b1, w2, b2)
    return _combine(x1, ysb, route, mod, l, ln_g, ln_b, split_out)


def kernel(x_prompt, x_sample, cache_k_attn, cache_v_attn, cache_k_na, cache_v_na, c, c_ctx, w_mod, b_mod, ln1_g, ln1_b, ln2_g, ln2_b, conv_w_in, conv_w, conv_b, conv_w_out, attn_w_qkv, attn_q_norm, attn_k_norm, attn_w_o, na_w_qkv, na_rpb, na_w_o, router_w, router_b, moe_w1, moe_b1, moe_w2, moe_b2):
    x = (x_prompt.reshape(N_PROMPT, D_MODEL), x_sample.reshape(N_SAMPLE, D_MODEL))
    mod = _adaln_all(jnp.concatenate([c_ctx[None, :], c], axis=0), w_mod, b_mod)
    new_kv = {}
    for l in range(DEPTH):
        kind, j = l % 3, l // 3
        if kind == 0:
            x1 = _conv_layer(x, mod, l, conv_w_in[j], conv_w[j], conv_b[j], conv_w_out[j], ln1_g[l], ln1_b[l])
        elif kind == 1:
            gq = jnp.tile(attn_q_norm[j], N_HEADS).reshape(1, N_HEADS * HEAD_DIM)
            gk = jnp.tile(attn_k_norm[j], N_KV_HEADS).reshape(1, N_KV_HEADS * HEAD_DIM)
            norm_args = (gq, gk, _block_diag_mean())
            wk = N_KV_HEADS * HEAD_DIM
            xp1, nk, nv = _attn_call(x, mod, l, False, attn_w_qkv[j], attn_w_o[j], ln1_g[l], ln1_b[l],
                                     n_seq=BATCH, seq_len=SEQ, tq=SEQ, n_kv=N_KV_HEADS, norm_args=norm_args,
                                     emit_kv=True, name="gqa_prompt")
            new_kv["k_attn"] = nk.reshape(BATCH, 1, SEQ, N_KV_HEADS, HEAD_DIM)
            new_kv["v_attn"] = nv.reshape(BATCH, 1, SEQ, N_KV_HEADS, HEAD_DIM)
            (xs1,) = _attn_call(x, mod, l, True, attn_w_qkv[j], attn_w_o[j], ln1_g[l], ln1_b[l],
                                n_seq=DEC_BATCH, seq_len=DEC_SEQ, tq=128, n_kv=N_KV_HEADS, norm_args=norm_args,
                                rope_args=_rope_tables(),
                                ctx_args=(cache_k_attn[:, j].reshape(DEC_BATCH, PAST_LEN, wk),
                                          cache_v_attn[:, j].reshape(DEC_BATCH, PAST_LEN, wk)),
                                name="gqa_sample")
            x1 = (xp1, xs1)
        else:
            wk = N_HEADS * HEAD_DIM
            xp1, nk, nv = _attn_call(x, mod, l, False, na_w_qkv[j], na_w_o[j], ln1_g[l], ln1_b[l],
                                     n_seq=BATCH, seq_len=SEQ, tq=SEQ, n_kv=N_HEADS, emit_kv=True,
                                     head_rows=True, name="mha_prompt")
            new_kv["k_na"] = nk.reshape(BATCH, 1, SEQ, N_HEADS, HEAD_DIM)
            new_kv["v_na"] = nv.reshape(BATCH, 1, SEQ, N_HEADS, HEAD_DIM)
            (xs1,) = _attn_call(x, mod, l, True, na_w_qkv[j], na_w_o[j], ln1_g[l], ln1_b[l],
                                n_seq=DEC_BATCH, seq_len=DEC_SEQ, tq=2 * GRID_W, n_kv=N_HEADS,
                                ctx_args=(cache_k_na[:, j].reshape(DEC_BATCH, PAST_LEN, wk),
                                          cache_v_na[:, j].reshape(DEC_BATCH, PAST_LEN, wk)),
                                bias=_na_bias_table(na_rpb[j]), name="na_sample")
            x1 = (xp1, xs1)
        out = _moe_layer(x1, mod, l, router_w[l], router_b[l], moe_w1, moe_b1, moe_w2, moe_b2,
                         ln2_g[l], ln2_b[l], split_out=l == DEPTH - 1)
        x = out[0]
    y_prompt = out[0].reshape(BATCH, SEQ, D_MODEL)
    y_sample = out[1].reshape(DEC_BATCH, DEC_SEQ, D_MODEL)
    return (y_prompt, y_sample, new_kv["k_attn"], new_kv["v_attn"], new_kv["k_na"], new_kv["v_na"])
```

```python
import functools

import numpy as np
import jax
import jax.numpy as jnp
from jax import lax
from jax.experimental import pallas as pl
from jax.experimental.pallas import tpu as pltpu

D_MODEL = 1024
BATCH = 16
SEQ = 256
DEPTH = 4
DEC_BATCH = 2
DEC_SEQ = 1024
PAST_LEN = 256
GRID_W = 64
HEAD_DIM = 64
N_HEADS = 16
N_KV_HEADS = 4
ROPE_THETA = 10000.0
ROPE_PAIRS = HEAD_DIM // 4
WIN_R = 8
WIN_C = 16
N_EXPERTS = 32
TOP_K = 4
D_FF = D_MODEL
SWIGLU_LIMIT = 7.0
SWIGLU_ALPHA = 1.702
DEEPNORM_ALPHA = (2 * DEPTH) ** 0.25
LN_EPS = 1e-5
RMS_EPS = 1e-6

F32 = jnp.float32
BF16 = jnp.bfloat16
I32 = jnp.int32
HIGHEST = lax.Precision.HIGHEST

N_PROMPT = BATCH * SEQ
N_SAMPLE = DEC_BATCH * DEC_SEQ
N_TOK = N_PROMPT + N_SAMPLE
N_ASSIGN = N_TOK * TOP_K

LANES = 128
SUBLANES = 8
PACKED_W = D_MODEL // 2

VMEM_LIMIT = 58 * 1024 * 1024


def _cparams(sem):
    return pltpu.CompilerParams(dimension_semantics=sem, vmem_limit_bytes=VMEM_LIMIT)


def _layer_norm(x, g, b):
    mu = jnp.mean(x, -1, keepdims=True)
    xc = x - mu
    var = jnp.mean(xc * xc, -1, keepdims=True)
    return xc * lax.rsqrt(var + LN_EPS) * g + b


def _mod_row(i, tile):
    n_prompt_tiles = N_PROMPT // tile
    return jnp.where(i < n_prompt_tiles, 0, 1 + (i - n_prompt_tiles) // (DEC_SEQ // tile))


def _mod_spec(l, tile):
    return pl.BlockSpec((None, None, 6, D_MODEL), lambda i, *_: (l, _mod_row(i, tile), 0, 0))


def _full(shape):
    nd = len(shape)
    return pl.BlockSpec(shape, lambda *_: (0,) * nd)


def _stream_specs(x, tile):
    if not isinstance(x, tuple):
        return [pl.BlockSpec((tile, D_MODEL), lambda i, *_: (i, 0))], [x]
    n_a = N_PROMPT // tile
    return ([pl.BlockSpec((tile, D_MODEL), lambda i, *_: (jnp.minimum(i, n_a - 1), 0)),
             pl.BlockSpec((tile, D_MODEL), lambda i, *_: (jnp.maximum(i - n_a, 0), 0))], list(x))


def _stream_tile(x_refs, tile):
    if len(x_refs) == 1:
        return x_refs[0][...]
    return jnp.where(pl.program_id(0) < N_PROMPT // tile, x_refs[0][...], x_refs[1][...])


ADALN_TN = 1536


N_COND = 1 + DEC_BATCH
ADALN_UNROLL = 4


def _adaln_kernel(ct_ref, w_ref, b_ref, o_ref, s_ref):
    @pl.when((pl.program_id(0) == 0) & (pl.program_id(1) == 0))
    def _silu():
        c = ct_ref[...]
        s_ref[...] = c * jax.nn.sigmoid(c)

    def body(j, accs):
        accs = list(accs)
        for u in range(ADALN_UNROLL):
            rows = pl.ds(pl.multiple_of((j * ADALN_UNROLL + u) * SUBLANES, SUBLANES), SUBLANES)
            w = w_ref[rows, :]
            for r in range(N_COND):
                accs[r] = accs[r] + w * jnp.concatenate([s_ref[r, rows, :]] * (ADALN_TN // LANES), axis=1)
        return tuple(accs)

    zero = jnp.zeros((SUBLANES, ADALN_TN), F32)
    accs = lax.fori_loop(0, D_MODEL // (SUBLANES * ADALN_UNROLL), body, (zero,) * N_COND)
    rows = [jnp.sum(a, axis=0, keepdims=True) for a in accs]
    o_ref[...] = jnp.concatenate(rows + [jnp.zeros((8 - N_COND, ADALN_TN), F32)], axis=0) + b_ref[...]


def _adaln_all(cond, w_mod, b_mod):
    n = 6 * D_MODEL
    cond_t = jnp.broadcast_to(cond[:, :, None], (N_COND, D_MODEL, LANES))
    out = pl.pallas_call(
        _adaln_kernel,
        grid=(DEPTH, n // ADALN_TN),
        in_specs=[
            pl.BlockSpec((N_COND, D_MODEL, LANES), lambda l, j: (0, 0, 0)),
            pl.BlockSpec((None, D_MODEL, ADALN_TN), lambda l, j: (l, 0, j)),
            pl.BlockSpec((None, 1, ADALN_TN), lambda l, j: (l, 0, j)),
        ],
        out_specs=pl.BlockSpec((None, 8, ADALN_TN), lambda l, j: (l, 0, j)),
        out_shape=jax.ShapeDtypeStruct((DEPTH, 8, n), F32),
        scratch_shapes=[pltpu.VMEM((N_COND, D_MODEL, LANES), F32)],
        compiler_params=_cparams(("arbitrary", "arbitrary")),
        name="adaln",
    )(cond_t, w_mod, b_mod.reshape(DEPTH, 1, n))
    return out.reshape(DEPTH, 8, 6, D_MODEL)


CONV_TM = 1024


def _conv_kernel(*refs, n_x):
    mod_ref, win_ref, cw_ref, cb_ref, wout_ref, g_ref, b_ref, o_ref = refs[n_x:]
    i = pl.program_id(0)
    x = _stream_tile(refs[:n_x], CONV_TM)
    h = (x * (1.0 + mod_ref[1:2, :]) + mod_ref[0:1, :]).astype(BF16)
    gc = jnp.dot(h, win_ref[:, D_MODEL:2 * D_MODEL], preferred_element_type=F32)
    xv = jnp.dot(h, win_ref[:, 2 * D_MODEL:], preferred_element_type=F32)
    u = gc * xv
    seq_len = jnp.where(i < N_PROMPT // CONV_TM, SEQ, DEC_SEQ)
    t = lax.broadcasted_iota(I32, (CONV_TM, 1), 0) & (seq_len - 1)
    u_prev = jnp.where(t == 0, 0.0, pltpu.roll(u, 1, axis=0))
    u_next = jnp.where(t == seq_len - 1, 0.0, pltpu.roll(u, CONV_TM - 1, axis=0))
    y = u_prev * cw_ref[0:1, :] + u * cw_ref[1:2, :] + u_next * cw_ref[2:3, :] + cb_ref[...]
    gb = jnp.dot(h, win_ref[:, :D_MODEL], preferred_element_type=F32)
    v = (gb * y).astype(BF16)
    o = jnp.dot(v, wout_ref[...], preferred_element_type=F32)
    o_ref[...] = _layer_norm(DEEPNORM_ALPHA * x + mod_ref[2:3, :] * o, g_ref[...], b_ref[...])


def _conv_layer(x, mod, l, w_in, cw, cb, w_out, ln_g, ln_b):
    x_specs, x_args = _stream_specs(x, CONV_TM)
    return pl.pallas_call(
        functools.partial(_conv_kernel, n_x=len(x_args)),
        grid=(N_TOK // CONV_TM,),
        in_specs=x_specs + [_mod_spec(l, CONV_TM), _full((D_MODEL, 3 * D_MODEL)), _full((3, D_MODEL)),
                            _full((1, D_MODEL)), _full((D_MODEL, D_MODEL)), _full((1, D_MODEL)),
                            _full((1, D_MODEL))],
        out_specs=pl.BlockSpec((CONV_TM, D_MODEL), lambda i: (i, 0)),
        out_shape=jax.ShapeDtypeStruct((N_TOK, D_MODEL), F32),
        compiler_params=_cparams(("arbitrary",)),
        name="conv_mixer",
    )(*x_args, mod, w_in.astype(BF16), cw, cb.reshape(1, D_MODEL), w_out.astype(BF16),
      ln_g.reshape(1, D_MODEL), ln_b.reshape(1, D_MODEL))


ATTN_CHUNK = 256
NA_SLOTS = WIN_R + 2


def _attn_kernel(*refs, seq_len, tq, n_kv, norm, rope, n_ctx, na, emit_kv, head_rows):
    refs = list(refs)
    x_ref, mod_ref, wqkv_ref = refs[:3]
    pos = 3
    if norm:
        gq_ref, gk_ref, bd_ref = refs[pos:pos + 3]
        pos += 3
    if rope:
        cos_ref, s1_ref, s2_ref = refs[pos:pos + 3]
        pos += 3
    if n_ctx:
        ck_ref, cv_ref = refs[pos:pos + 2]
        pos += 2
    if na:
        bias_ref = refs[pos]
        pos += 1
    wo_ref, lng_ref, lnb_ref = refs[pos:pos + 3]
    pos += 3
    o_ref = refs[pos]
    pos += 1
    if emit_kv:
        nk_ref, nv_ref = refs[pos:pos + 2]
        pos += 2
    q_scr, k_scr, v_scr, o_scr = refs[pos:pos + 4]

    qt = pl.program_id(1)
    n_qt = seq_len // tq
    wq = N_HEADS * HEAD_DIM
    wk = n_kv * HEAD_DIM
    rep = N_HEADS // n_kv

    def rms(v, g_ref, width):
        ms = jnp.dot((v * v).astype(BF16), bd_ref[:width, :width], preferred_element_type=F32)
        return v * lax.rsqrt(ms + RMS_EPS) * g_ref[...]

    def rot(v, rows, width):
        def tab(ref):
            t = ref[rows, :]
            return jnp.concatenate([t] * (width // LANES), axis=1)

        return (v * tab(cos_ref) + pltpu.roll(v, width - ROPE_PAIRS, axis=1) * tab(s1_ref)
                + pltpu.roll(v, ROPE_PAIRS, axis=1) * tab(s2_ref))

    chunk = min(seq_len, ATTN_CHUNK)

    @pl.when(qt == 0)
    def _project():
        def body(ci, carry):
            r0 = pl.multiple_of(ci * chunk, chunk)
            rows = pl.ds(r0, chunk)
            h = (x_ref[rows, :] * (1.0 + mod_ref[1:2, :]) + mod_ref[0:1, :]).astype(BF16)
            q = jnp.dot(h, wqkv_ref[:, :wq], preferred_element_type=F32)
            k = jnp.dot(h, wqkv_ref[:, wq:wq + wk], preferred_element_type=F32)
            v = jnp.dot(h, wqkv_ref[:, wq + wk:], preferred_element_type=F32)
            if norm:
                q = rms(q, gq_ref, wq)
                k = rms(k, gk_ref, wk)
            if emit_kv and head_rows:
                for hh in range(n_kv):
                    dst = pl.ds(r0 * n_kv + hh, chunk, stride=n_kv)
                    nk_ref[dst, :] = k[:, hh * HEAD_DIM:(hh + 1) * HEAD_DIM]
                    nv_ref[dst, :] = v[:, hh * HEAD_DIM:(hh + 1) * HEAD_DIM]
            elif emit_kv:
                nk_ref[rows, :] = k
                nv_ref[rows, :] = v
            if rope:
                q = rot(q, rows, wq)
                k = rot(k, rows, wk)
            q_scr[rows, :] = (q * (HEAD_DIM ** -0.5)).astype(BF16)
            k_scr[pl.ds(n_ctx + r0, chunk), :] = k.astype(BF16)
            v_scr[pl.ds(n_ctx + r0, chunk), :] = v.astype(BF16)
            return carry

        lax.fori_loop(0, seq_len // chunk, body, 0)
        if n_ctx:
            k_scr[:n_ctx, :] = ck_ref[...].astype(BF16)
            v_scr[:n_ctx, :] = cv_ref[...].astype(BF16)
        if na:
            k_scr[n_ctx + seq_len:, :] = jnp.zeros((GRID_W, wk), BF16)
            v_scr[n_ctx + seq_len:, :] = jnp.zeros((GRID_W, wk), BF16)

    q0 = pl.multiple_of(qt * tq, tq)
    if na:
        n_rows = DEC_SEQ // GRID_W
        w = jnp.minimum(jnp.clip(2 * qt - WIN_R // 2, 0, n_rows - WIN_R), n_rows - WIN_R - 1)
        k0 = pl.multiple_of(n_ctx + w * GRID_W, GRID_W)
        slot_lane = lax.broadcasted_iota(I32, (GRID_W, NA_SLOTS * GRID_W), 1)
        pair_base, invalid = [], []
        for j in range(2):
            r = 2 * qt + j
            first = jnp.clip(r - WIN_R // 2, 0, n_rows - WIN_R) - w
            ok = (slot_lane >= first * GRID_W) & (slot_lane < (first + WIN_R) * GRID_W)
            invalid.append(jnp.where(ok, 0.0, -1e30))
            pair_base.append(w - r + WIN_R)
    for g in range(n_kv):
        heads = [g * rep + i for i in range(rep)]
        hd = heads[0]
        head_lanes = [slice(h * HEAD_DIM, (h + 1) * HEAD_DIM) for h in heads]
        gs = slice(g * HEAD_DIM, (g + 1) * HEAD_DIM)
        qh = jnp.concatenate([q_scr[pl.ds(q0, tq), hs] for hs in head_lanes], axis=0)
        dn = (((1,), (1,)), ((), ()))
        if na:
            kc, vc = k_scr[:n_ctx, gs], v_scr[:n_ctx, gs]
            kl, vl = k_scr[pl.ds(k0, NA_SLOTS * GRID_W), gs], v_scr[pl.ds(k0, NA_SLOTS * GRID_W), gs]
            bias = jnp.concatenate(
                [jnp.concatenate([bias_ref[hd, pl.ds(pair_base[j] + 2 * i, 1)][0] for i in range(NA_SLOTS // 2)],
                                 axis=1) + invalid[j] for j in range(2)], axis=0)
            s = jnp.concatenate([lax.dot_general(qh, kc, dn, preferred_element_type=F32),
                                 lax.dot_general(qh, kl, dn, preferred_element_type=F32) + bias], axis=1)
        else:
            s = lax.dot_general(qh, k_scr[:, gs], dn, preferred_element_type=F32)
        e = jnp.exp(s - jnp.max(s, axis=1, keepdims=True))
        den = jnp.sum(e, axis=1, keepdims=True)
        eb = e.astype(BF16)
        if na:
            oh = (jnp.dot(eb[:, :n_ctx], vc, preferred_element_type=F32)
                  + jnp.dot(eb[:, n_ctx:], vl, preferred_element_type=F32))
        else:
            oh = jnp.dot(eb, v_scr[:, gs], preferred_element_type=F32)
        oh = (oh / den).astype(BF16)
        for i, hs in enumerate(head_lanes):
            o_scr[pl.ds(q0, tq), hs] = oh[i * tq:(i + 1) * tq, :]

    @pl.when(qt == n_qt - 1)
    def _finish():
        def body(ci, carry):
            rows = pl.ds(pl.multiple_of(ci * chunk, chunk), chunk)
            o = jnp.dot(o_scr[rows, :], wo_ref[...], preferred_element_type=F32)
            o_ref[rows, :] = _layer_norm(DEEPNORM_ALPHA * x_ref[rows, :] + mod_ref[2:3, :] * o,
                                         lng_ref[...], lnb_ref[...])
            return carry

        lax.fori_loop(0, seq_len // chunk, body, 0)


def _attn_call(x, mod, l, latent, w_qkv, w_o, ln_g, ln_b, *, n_seq, seq_len, tq, n_kv, norm_args=None,
               rope_args=None, ctx_args=None, bias=None, emit_kv=False, head_rows=False, name="attn"):
    wq = N_HEADS * HEAD_DIM
    wk = n_kv * HEAD_DIM
    n_ctx = PAST_LEN if ctx_args is not None else 0
    n_qt = seq_len // tq
    seq_off = N_PROMPT // seq_len if latent else 0
    pad = GRID_W if bias is not None else 0
    seq_spec = pl.BlockSpec((seq_len, D_MODEL), lambda s, t: (s, 0))
    mod_spec = pl.BlockSpec((None, None, 6, D_MODEL), lambda s, t: (l, (1 + s) if latent else 0, 0, 0))
    in_specs = [pl.BlockSpec((seq_len, D_MODEL), lambda s, t: (s + seq_off, 0)), mod_spec,
                pl.BlockSpec((D_MODEL, wq + 2 * wk), lambda s, t: (0, 0))]
    args = [x, mod, w_qkv.astype(BF16)]
    if norm_args is not None:
        gq, gk, bd = norm_args
        in_specs += [pl.BlockSpec((1, wq), lambda s, t: (0, 0)), pl.BlockSpec((1, wk), lambda s, t: (0, 0)),
                     pl.BlockSpec((wq, wq), lambda s, t: (0, 0))]
        args += [gq, gk, bd]
    if rope_args is not None:
        in_specs += [pl.BlockSpec((seq_len, LANES), lambda s, t: (0, 0))] * 3
        args += list(rope_args)
    if ctx_args is not None:
        in_specs += [pl.BlockSpec((None, n_ctx, wk), lambda s, t: (s, 0, 0))] * 2
        args += list(ctx_args)
    if bias is not None:
        in_specs += [pl.BlockSpec(bias.shape, lambda s, t: (0, 0, 0, 0))]
        args += [bias]
    in_specs += [pl.BlockSpec((D_MODEL, D_MODEL), lambda s, t: (0, 0)),
                 pl.BlockSpec((1, D_MODEL), lambda s, t: (0, 0)), pl.BlockSpec((1, D_MODEL), lambda s, t: (0, 0))]
    args += [w_o.astype(BF16), ln_g.reshape(1, D_MODEL), ln_b.reshape(1, D_MODEL)]
    out_specs = [seq_spec]
    out_shape = [jax.ShapeDtypeStruct((n_seq * seq_len, D_MODEL), F32)]
    if emit_kv:
        kv_shape = (seq_len * n_kv, HEAD_DIM) if head_rows else (seq_len, wk)
        out_specs += [pl.BlockSpec(kv_shape, lambda s, t: (s, 0))] * 2
        out_shape += [jax.ShapeDtypeStruct((n_seq * kv_shape[0], kv_shape[1]), F32)] * 2
    kern = functools.partial(_attn_kernel, seq_len=seq_len, tq=tq, n_kv=n_kv, norm=norm_args is not None,
                             rope=rope_args is not None, n_ctx=n_ctx, na=bias is not None, emit_kv=emit_kv,
                             head_rows=head_rows)
    return pl.pallas_call(
        kern,
        grid=(n_seq, n_qt),
        in_specs=in_specs,
        out_specs=out_specs,
        out_shape=out_shape,
        scratch_shapes=[pltpu.VMEM((seq_len, wq), BF16), pltpu.VMEM((n_ctx + seq_len + pad, wk), BF16),
                        pltpu.VMEM((n_ctx + seq_len + pad, wk), BF16), pltpu.VMEM((seq_len, wq), BF16)],
        compiler_params=_cparams(("arbitrary", "arbitrary")),
        name=name,
    )(*args)


def _rope_tables():
    t = np.arange(DEC_SEQ)
    pos = np.stack([t // GRID_W, t % GRID_W], axis=1).astype(np.float64)
    inv = (ROPE_THETA ** (-np.arange(ROPE_PAIRS, dtype=np.float32) / ROPE_PAIRS)).astype(np.float64)
    ang = pos[:, :, None] * inv[None, None, :]
    cos, sin = np.cos(ang), np.sin(ang)
    zero = np.zeros_like(sin)
    cos_t = np.concatenate([cos, cos], axis=2).reshape(DEC_SEQ, HEAD_DIM)
    s1_t = np.concatenate([-sin, zero], axis=2).reshape(DEC_SEQ, HEAD_DIM)
    s2_t = np.concatenate([zero, sin], axis=2).reshape(DEC_SEQ, HEAD_DIM)
    return tuple(jnp.asarray(np.tile(a, (1, LANES // HEAD_DIM)), F32) for a in (cos_t, s1_t, s2_t))


def _block_diag_mean():
    a = np.kron(np.eye(N_HEADS), np.full((HEAD_DIM, HEAD_DIM), 1.0 / HEAD_DIM))
    return jnp.asarray(a, BF16)


def _bias_kernel(rows_ref, sel_ref, row_ok_ref, inside_ref, o_ref):
    r = rows_ref[...]
    r1 = r.astype(BF16)
    r2 = (r - r1.astype(F32)).astype(BF16)
    r3 = (r - r1.astype(F32) - r2.astype(F32)).astype(BF16)
    sel = sel_ref[...]
    v = (jnp.dot(r1, sel, preferred_element_type=F32) + jnp.dot(r2, sel, preferred_element_type=F32)
         + jnp.dot(r3, sel, preferred_element_type=F32))
    lane = lax.broadcasted_iota(I32, v.shape, 1)
    row_ok = jnp.where((lane & (LANES - 1)) < GRID_W, row_ok_ref[:, 0:1], row_ok_ref[:, 1:2])
    o_ref[...] = jnp.where((inside_ref[...] > 0.0) & (row_ok > 0.0), v, -1e30)


def _na_bias_table(rpb):
    n_d, n_off, n_pair = 2 * WIN_R - 1, 2 * WIN_C - 1, 2 * WIN_R + 1
    cols = np.arange(GRID_W)
    col_start = np.clip(cols - WIN_C // 2, 0, GRID_W - WIN_C)
    kc = np.arange(GRID_W)
    inside = (kc[None, :] >= col_start[:, None]) & (kc[None, :] < col_start[:, None] + WIN_C)
    off = np.clip(kc[None, :] - cols[:, None] + (WIN_C - 1), 0, n_off - 1)
    sel = np.zeros((2, LANES, GRID_W, 2, GRID_W), np.float32)
    for side in range(2):
        sel[side, off, cols[:, None], side, kc[None, :]] = 1.0
    sel = sel.reshape(2 * LANES, GRID_W * LANES)
    inside2 = np.broadcast_to(inside[:, None, :], (GRID_W, 2, GRID_W)).reshape(1, GRID_W * LANES)
    d = np.arange(-1, n_pair - 1)
    row_ok = np.stack([(d >= 0) & (d < n_d), (d + 1 >= 0) & (d + 1 < n_d)], axis=1)
    row_ok = np.tile(np.pad(row_ok, ((0, 0), (0, LANES - 2))), (N_HEADS, 1)).astype(np.float32)
    padded = jnp.pad(rpb, ((0, 0), (1, 2), (0, LANES - n_off)))
    rows = jnp.concatenate([padded[:, :-1], padded[:, 1:]], axis=-1).reshape(N_HEADS * n_pair, 2 * LANES)
    table = pl.pallas_call(
        _bias_kernel,
        out_shape=jax.ShapeDtypeStruct((N_HEADS * n_pair, GRID_W * LANES), F32),
        compiler_params=pltpu.CompilerParams(vmem_limit_bytes=VMEM_LIMIT),
        name="na_bias",
    )(rows, jnp.asarray(sel, BF16), jnp.asarray(row_ok), jnp.asarray(inside2.astype(np.float32)))
    return table.reshape(N_HEADS, n_pair, GRID_W, LANES)


MOE_TM = 256
N_TILES = N_TOK // MOE_TM
CHUNK_SHIFT, BIG_SHIFT, PART_SHIFT, SUB_SHIFT = 3, 5, 7, 9
CHUNK = 1 << CHUNK_SHIFT
BIG = 1 << BIG_SHIFT
TILE_ROWS = 1280
SUB = 1 << SUB_SHIFT
PART = 1 << PART_SHIFT
ITEM_ROWS = 2048
MAX_ITEMS = 48
ITEM_TABLE = 64
MAX_ROWS = N_ASSIGN + N_TILES * N_EXPERTS * (CHUNK - 1)
BIG_LIST = 1024
SMALL_LIST = 2560
N_SEG =N_TILES * N_EXPERTS

assert TILE_ROWS >= MOE_TM * TOP_K + N_EXPERTS * (CHUNK - 1) and TILE_ROWS % MOE_TM == 0 and ITEM_ROWS % SUB == 0
assert MAX_ITEMS >= N_EXPERTS + (N_ASSIGN + N_SEG * (CHUNK - 1) - 1) // (ITEM_ROWS - MOE_TM)
assert BIG_LIST >= MAX_ROWS // BIG + MOE_TM // BIG and SMALL_LIST >= (N_SEG + 1) * (BIG // CHUNK - 1)
assert ITEM_ROWS <= 1 << 11


def _dispatch_kernel(*refs, n_x):
    mod_ref, rwt_ref, rbc_ref, upper_ref, lower_ref, xsb_ref, route_ref, tab_ref = refs[n_x:]
    h2 = _stream_tile(refs[:n_x], MOE_TM) * (1.0 + mod_ref[4:5, :]) + mod_ref[3:4, :]
    def split(v):
        hi = v.astype(BF16)
        return hi, (v - hi.astype(F32)).astype(BF16)

    def nt_dot(a, b):
        return lax.dot_general(a, b, (((1,), (1,)), ((), ())), preferred_element_type=F32)

    (w_hi, w_lo), (h_hi, h_lo) = split(rwt_ref[...]), split(h2)
    logits = (nt_dot(w_hi, h_hi) + nt_dot(w_hi, h_lo) + nt_dot(w_lo, h_hi)) + rbc_ref[...]
    sub = lax.broadcasted_iota(I32, (N_EXPERTS, MOE_TM), 0)
    vals, hots = [], []
    cur = logits
    for _ in range(TOP_K):
        m = jnp.max(cur, axis=0, keepdims=True)
        idx = jnp.min(jnp.where(cur == m, sub, N_EXPERTS), axis=0, keepdims=True)
        hot = sub == idx
        vals.append(m)
        hots.append(hot)
        cur = jnp.where(hot, -jnp.inf, cur)
    exps = [jnp.exp(v - vals[0]) for v in vals]
    den = (exps[0] + exps[1]) + (exps[2] + exps[3])
    mask = jnp.zeros((N_EXPERTS, MOE_TM), F32)
    for hot in hots:
        mask = mask + jnp.where(hot, 1.0, 0.0)
    before = jnp.dot(mask.astype(BF16), upper_ref[...], preferred_element_type=F32)
    n = jnp.sum(mask, axis=1, keepdims=True)
    p = (((n.astype(I32) + (CHUNK - 1)) >> CHUNK_SHIFT) << CHUNK_SHIFT).astype(F32)
    start = jnp.dot(lower_ref[...], jnp.broadcast_to(p, (N_EXPERTS, LANES)), precision=HIGHEST,
                    preferred_element_type=F32)[:, :1]
    base = start + before
    lps = [jnp.sum(jnp.where(hot, base, 0.0), axis=0, keepdims=True) for hot in hots]
    route_ref[...] = jnp.concatenate(lps + [e / den for e in exps], axis=0)
    lane = lax.broadcasted_iota(I32, (N_EXPERTS, LANES), 1)
    tab_ref[...] = jnp.where(lane == 0, n, jnp.where(lane == 1, p, jnp.where(lane == 2, start, 0.0)))
    jrow = lax.broadcasted_iota(I32, (TILE_ROWS, MOE_TM), 0)
    onehot = jnp.zeros((TILE_ROWS, MOE_TM), F32)
    for lp in lps:
        onehot = jnp.where(jrow == lp.astype(I32), 1.0, onehot)
    xs = jnp.dot(onehot.astype(BF16), h_hi, preferred_element_type=F32)
    xsb_ref[...] = pltpu.pack_elementwise([xs[:, :PACKED_W], xs[:, PACKED_W:]], packed_dtype=BF16)


def _dispatch(x1, mod, l, rw, rb):
    upper = jnp.asarray(np.triu(np.ones((MOE_TM, MOE_TM)), 1), BF16)
    lower = jnp.asarray(np.tril(np.ones((N_EXPERTS, N_EXPERTS)), -1), F32)
    x_specs, x_args = _stream_specs(x1, MOE_TM)
    return pl.pallas_call(
        functools.partial(_dispatch_kernel, n_x=len(x_args)),
        grid=(N_TILES,),
        in_specs=x_specs + [_mod_spec(l, MOE_TM), _full((N_EXPERTS, D_MODEL)), _full((N_EXPERTS, 1)),
                            _full((MOE_TM, MOE_TM)), _full((N_EXPERTS, N_EXPERTS))],
        out_specs=[pl.BlockSpec((None, TILE_ROWS, PACKED_W), lambda i: (i, 0, 0)),
                   pl.BlockSpec((None, 2 * TOP_K, MOE_TM), lambda i: (i, 0, 0)),
                   pl.BlockSpec((None, N_EXPERTS, LANES), lambda i: (i, 0, 0))],
        out_shape=[jax.ShapeDtypeStruct((N_TILES, TILE_ROWS, PACKED_W), I32),
                   jax.ShapeDtypeStruct((N_TILES, 2 * TOP_K, MOE_TM), F32),
                   jax.ShapeDtypeStruct((N_TILES, N_EXPERTS, LANES), F32)],
        compiler_params=_cparams(("arbitrary",)),
        name="moe_dispatch",
    )(*x_args, mod, rw.T, rb.reshape(N_EXPERTS, 1), upper, lower)


def _plan_kernel(p_ref, s_ref, zbig_hbm, zsmall_hbm, iexp_ref, irows_ref, ibig_ref, ismall_ref, big_ref, small_ref,
                 sem):
    clears = [pltpu.make_async_copy(zbig_hbm, big_ref, sem), pltpu.make_async_copy(zsmall_hbm, small_ref, sem)]
    for cp in clears:
        cp.start()
    for cp in clears:
        cp.wait()

    def open_item(it, e, n_big, n_small):
        iexp_ref[it] = e
        ibig_ref[it] = n_big
        ismall_ref[it] = n_small

    def per_expert(e, carry):
        it, last_e, n_big, n_small = carry
        open_item(it, e, n_big, n_small)

        def per_tile(b, c):
            it, rows, n_big, n_small = c
            pb = p_ref[b * N_EXPERTS + e]
            full = rows + pb > ITEM_ROWS

            @pl.when(full)
            def _():
                irows_ref[it] = rows
                open_item(it + 1, e, n_big, n_small)

            it = jnp.where(full, it + 1, it)
            rows = jnp.where(full, 0, rows)
            word = (b << 22) | (s_ref[b * N_EXPERTS + e] << 11) | rows
            nb = pb >> BIG_SHIFT
            ns = (pb - nb * BIG) >> CHUNK_SHIFT
            for k in range(MOE_TM // BIG):
                big_ref[n_big + k] = word + k * (BIG << 11 | BIG)
            for k in range(BIG // CHUNK - 1):
                small_ref[n_small + k] = word + nb * (BIG << 11 | BIG) + k * (CHUNK << 11 | CHUNK)
            return it, rows + pb, n_big + nb, n_small + ns

        it, rows, n_big, n_small = lax.fori_loop(0, N_TILES, per_tile, (it, jnp.int32(0), n_big, n_small))
        irows_ref[it] = rows
        used = rows > 0
        return jnp.where(used, it + 1, it), jnp.where(used, e, last_e), n_big, n_small

    zero = jnp.int32(0)
    n_items, last_e, n_big, n_small = lax.fori_loop(0, N_EXPERTS, per_expert, (zero, zero, zero, zero))

    def idle(it, c):
        open_item(it, last_e, n_big, n_small)
        irows_ref[it] = 0
        return c

    lax.fori_loop(n_items, ITEM_TABLE, idle, 0)


def _plan(ptab, stab):
    smem = pl.BlockSpec(memory_space=pltpu.SMEM)
    table = jax.ShapeDtypeStruct((ITEM_TABLE,), I32)
    return pl.pallas_call(
        _plan_kernel,
        in_specs=[smem, smem, pl.BlockSpec(memory_space=pl.ANY), pl.BlockSpec(memory_space=pl.ANY)],
        out_specs=[smem] * 6,
        out_shape=[table] * 4 + [jax.ShapeDtypeStruct((BIG_LIST,), I32), jax.ShapeDtypeStruct((SMALL_LIST,), I32)],
        scratch_shapes=[pltpu.SemaphoreType.DMA(())],
        name="moe_plan",
    )(ptab, stab, jnp.zeros((BIG_LIST,), I32), jnp.zeros((SMALL_LIST,), I32))


def _unpack_halves(p):
    lo = pltpu.unpack_elementwise(p, index=0, packed_dtype=BF16, unpacked_dtype=F32)
    hi = pltpu.unpack_elementwise(p, index=1, packed_dtype=BF16, unpacked_dtype=F32)
    return lo.astype(BF16), hi.astype(BF16)


def _unpack_rows(p):
    return jnp.concatenate(_unpack_halves(p), axis=1)


def _pack_rows(y):
    return pltpu.pack_elementwise([y[:, :PACKED_W], y[:, PACKED_W:]], packed_dtype=BF16)


def _expert_kernel(iexp_ref, irows_ref, ibig_ref, ismall_ref, big_ref, small_ref, xsb_hbm, w1_hbm, w2_hbm,
                   b1_ref, b2_ref, ysb_hbm, xg_ref, g_ref, a_ref, yb_ref, w_ref, cnt_ref,
                   in_big, in_small, out_big, out_small, w_sem, *, layer):
    it = pl.program_id(0)
    rows = irows_ref[it]
    slot = it & 1

    def weight_copies(item, phase):
        ee = iexp_ref[item]
        copies = []
        for k in range(2):
            half = pl.ds(k * (D_MODEL // 2), D_MODEL // 2)
            if phase < 2:
                src = w1_hbm.at[layer, ee, half, pl.ds(phase * D_FF, D_FF)]
            else:
                src = w2_hbm.at[layer, ee, half, :]
            copies.append(pltpu.make_async_copy(src, w_ref.at[phase, half, :], w_sem.at[phase]))
        return copies

    def start_weights(item, phase):
        @pl.when(irows_ref[item] > 0)
        def _():
            for cp in weight_copies(item, phase):
                cp.start()

    def wait_weights(phase):
        for cp in weight_copies(it, phase):
            cp.wait()

    def aligned(v):
        return v if isinstance(v, int) else pl.multiple_of(v, CHUNK)

    def copy(to_vmem, buf, b, src, dst, size):
        hbm = (xsb_hbm if to_vmem else ysb_hbm).at[b, pl.ds(aligned(src), size), :]
        if to_vmem:
            sem = in_big if size == BIG else in_small
            return pltpu.make_async_copy(hbm, xg_ref.at[buf, pl.ds(aligned(dst), size), :], sem)
        sem = out_big if size == BIG else out_small
        return pltpu.make_async_copy(yb_ref.at[pl.ds(aligned(dst), size), :], hbm, sem)

    def start_segments(item, to_vmem, buf):
        def start_list(lst_ref, lo, hi, size):
            def body(i, z):
                word = lst_ref[i]
                copy(to_vmem, buf, word >> 22, (word >> 11) & 2047, word & 2047, size).start()
                return z

            lax.fori_loop(lo, hi, body, 0)
            return hi - lo

        return (start_list(big_ref, ibig_ref[item], ibig_ref[item + 1], BIG),
                start_list(small_ref, ismall_ref[item], ismall_ref[item + 1], CHUNK))

    def wait_segments(to_vmem, n_big, n_small):
        def wait_big(c, z):
            copy(to_vmem, 0, 0, 0, 0, BIG).wait()
            return z

        def wait_small(c, z):
            copy(to_vmem, 0, 0, 0, 0, CHUNK).wait()
            return z

        lax.fori_loop(0, n_big, wait_big, 0)
        lax.fori_loop(0, n_small, wait_small, 0)

    @pl.when(it == 0)
    def _init():
        xg_ref[...] = jnp.zeros_like(xg_ref)
        start_weights(0, 0)
        start_weights(0, 1)
        cnt_ref[0], cnt_ref[1] = start_segments(0, True, 0)
        cnt_ref[2] = 0
        cnt_ref[3] = 0

    def for_sub_tiles(fn):
        n_full = rows >> SUB_SHIFT
        rem_parts = (rows - (n_full << SUB_SHIFT) + (PART - 1)) >> PART_SHIFT

        def body(i, c):
            fn(pl.ds(pl.multiple_of(i * SUB, SUB), SUB))
            return c

        lax.fori_loop(0, n_full, body, 0)
        r0 = pl.multiple_of(n_full * SUB, SUB)
        for k in range(1, SUB // PART + 1):
            @pl.when(rem_parts == k)
            def _(k=k):
                fn(pl.ds(r0, k * PART))

    def matmul(lo, hi, phase):
        half = D_MODEL // 2
        return (jnp.dot(lo, w_ref[phase, :half, :].astype(BF16), preferred_element_type=F32)
                + jnp.dot(hi, w_ref[phase, half:, :].astype(BF16), preferred_element_type=F32))

    @pl.when(rows > 0)
    def _item():
        start_weights(it, 2)
        wait_segments(True, cnt_ref[0], cnt_ref[1])
        wait_weights(0)

        def gate(r):
            lo, hi = _unpack_halves(xg_ref[slot, r, :])
            gate = jnp.minimum(matmul(lo, hi, 0) + b1_ref[:, :D_FF], SWIGLU_LIMIT)
            g_ref[r, :] = gate * (0.5 + 0.5 * jnp.tanh((0.5 * SWIGLU_ALPHA) * gate))

        for_sub_tiles(gate)

        start_weights(it + 1, 0)
        cnt_ref[0], cnt_ref[1] = start_segments(it + 1, True, 1 - slot)
        wait_weights(1)

        def up_act(r):
            lo, hi = _unpack_halves(xg_ref[slot, r, :])
            up = matmul(lo, hi, 1) + b1_ref[:, D_FF:]
            up = jnp.clip(up, -SWIGLU_LIMIT, SWIGLU_LIMIT)
            a_ref[r, :] = (g_ref[r, :] * (up + 1.0)).astype(BF16)

        for_sub_tiles(up_act)

        start_weights(it + 1, 1)
        wait_weights(2)
        wait_segments(False, cnt_ref[2], cnt_ref[3])

        def down(r):
            y = matmul(a_ref[r, :D_FF // 2], a_ref[r, D_FF // 2:], 2) + b2_ref[...]
            yb_ref[r, :] = _pack_rows(y)

        for_sub_tiles(down)
        cnt_ref[2], cnt_ref[3] = start_segments(it, False, 0)

    @pl.when(it == MAX_ITEMS - 1)
    def _drain():
        wait_segments(False, cnt_ref[2], cnt_ref[3])


def _experts(iexp, irows, ibig, ismall, big, small, xsb, l, w1, b1, w2, b2):
    n_prefetch = 6
    any_spec = pl.BlockSpec(memory_space=pl.ANY)
    return pl.pallas_call(
        functools.partial(_expert_kernel, layer=l),
        grid_spec=pltpu.PrefetchScalarGridSpec(
            num_scalar_prefetch=n_prefetch,
            grid=(MAX_ITEMS,),
            in_specs=[any_spec, any_spec, any_spec,
                      pl.BlockSpec((None, None, 1, 2 * D_FF), lambda it, ie, *_: (l, ie[it], 0, 0)),
                      pl.BlockSpec((None, None, 1, D_MODEL), lambda it, ie, *_: (l, ie[it], 0, 0))],
            out_specs=pl.BlockSpec(memory_space=pl.ANY),
            scratch_shapes=[pltpu.VMEM((2, ITEM_ROWS, PACKED_W), I32),
                            pltpu.VMEM((ITEM_ROWS, D_FF), F32),
                            pltpu.VMEM((ITEM_ROWS, D_FF), BF16),
                            pltpu.VMEM((ITEM_ROWS, PACKED_W), I32),
                            pltpu.VMEM((3, D_MODEL, D_FF), F32),
                            pltpu.SMEM((4,), I32),
                            pltpu.SemaphoreType.DMA(()),
                            pltpu.SemaphoreType.DMA(()),
                            pltpu.SemaphoreType.DMA(()),
                            pltpu.SemaphoreType.DMA(()),
                            pltpu.SemaphoreType.DMA((3,))],
        ),
        out_shape=jax.ShapeDtypeStruct((N_TILES, TILE_ROWS, PACKED_W), I32),
        input_output_aliases={n_prefetch: 0},
        compiler_params=_cparams(("arbitrary",)),
        name="moe_experts",
    )(iexp, irows, ibig, ismall, big, small, xsb, w1, w2,
      b1.reshape(DEPTH, N_EXPERTS, 1, 2 * D_FF), b2.reshape(DEPTH, N_EXPERTS, 1, D_MODEL))


def _combine_kernel(*refs, n_x, n_out):
    ysb_ref, route_ref, eye_ref, mod_ref, g_ref, b_ref = refs[n_x:len(refs) - n_out]
    o_refs = refs[len(refs) - n_out:]
    rt = lax.dot_general(eye_ref[...], route_ref[...], (((1,), (1,)), ((), ())), precision=HIGHEST,
                         preferred_element_type=F32)
    lane = lax.broadcasted_iota(I32, (MOE_TM, TILE_ROWS), 1)
    c = jnp.zeros((MOE_TM, TILE_ROWS), F32)
    for k in range(TOP_K):
        c = jnp.where(lane == rt[:, k:k + 1].astype(I32), rt[:, TOP_K + k:TOP_K + k + 1], c)
    f = jnp.dot(c.astype(BF16), _unpack_rows(ysb_ref[...]), preferred_element_type=F32)
    x = _stream_tile(refs[:n_x], MOE_TM)
    out = _layer_norm(DEEPNORM_ALPHA * x + mod_ref[5:6, :] * f, g_ref[...], b_ref[...])
    if n_out == 1:
        o_refs[0][...] = out
    else:
        is_ctx = pl.program_id(0) < N_PROMPT // MOE_TM

        @pl.when(is_ctx)
        def _():
            o_refs[0][...] = out

        @pl.when(jnp.logical_not(is_ctx))
        def _():
            o_refs[1][...] = out


def _combine(x1, ysb, route, mod, l, g, b, split_out):
    x_specs, x_args = _stream_specs(x1, MOE_TM)
    if split_out:
        out_specs, _ = _stream_specs((None, None), MOE_TM)
        out_shape = [jax.ShapeDtypeStruct((N_PROMPT, D_MODEL), F32), jax.ShapeDtypeStruct((N_SAMPLE, D_MODEL), F32)]
    else:
        out_specs, _ = _stream_specs(None, MOE_TM)
        out_shape = [jax.ShapeDtypeStruct((N_TOK, D_MODEL), F32)]
    return pl.pallas_call(
        functools.partial(_combine_kernel, n_x=len(x_args), n_out=len(out_shape)),
        grid=(N_TILES,),
        in_specs=x_specs + [pl.BlockSpec((None, TILE_ROWS, PACKED_W), lambda i: (i, 0, 0)),
                            pl.BlockSpec((None, 2 * TOP_K, MOE_TM), lambda i: (i, 0, 0)),
                            _full((MOE_TM, MOE_TM)), _mod_spec(l, MOE_TM), _full((1, D_MODEL)),
                            _full((1, D_MODEL))],
        out_specs=out_specs,
        out_shape=out_shape,
        compiler_params=_cparams(("arbitrary",)),
        name="moe_combine",
    )(*x_args, ysb, route, jnp.eye(MOE_TM, dtype=F32), mod, g.reshape(1, D_MODEL), b.reshape(1, D_MODEL))


def _moe_layer(x1, mod, l, rw, rb, w1, b1, w2, b2, ln_g, ln_b, split_out=False):
    xsb, route, tab = _dispatch(x1, mod, l, rw, rb)
    ptab = tab[:, :, 1].astype(I32).reshape(N_SEG)
    stab = tab[:, :, 2].astype(I32).reshape(N_SEG)
    iexp, irows, ibig, ismall, big, small = _plan(ptab, stab)
    ysb = _experts(iexp, irows, ibig, ismall, big, small, xsb, l, w1, b1, w2, b2)
    return _combine(x1, ysb, route, mod, l, ln_g, ln_b, split_out)


def kernel(x_prompt, x_sample, cache_k_attn, cache_v_attn, cache_k_na, cache_v_na, c, c_ctx, w_mod, b_mod, ln1_g, ln1_b, ln2_g, ln2_b, conv_w_in, conv_w, conv_b, conv_w_out, attn_w_qkv, attn_q_norm, attn_k_norm, attn_w_o, na_w_qkv, na_rpb, na_w_o, router_w, router_b, moe_w1, moe_b1, moe_w2, moe_b2):
    x = (x_prompt.reshape(N_PROMPT, D_MODEL), x_sample.reshape(N_SAMPLE, D_MODEL))
    mod = _adaln_all(jnp.concatenate([c_ctx[None, :], c], axis=0), w_mod, b_mod)
    new_kv = {}
    for l in range(DEPTH):
        kind, j = l % 3, l // 3
        if kind == 0:
            x1 = _conv_layer(x, mod, l, conv_w_in[j], conv_w[j], conv_b[j], conv_w_out[j], ln1_g[l], ln1_b[l])
        elif kind == 1:
            gq = jnp.tile(attn_q_norm[j], N_HEADS).reshape(1, N_HEADS * HEAD_DIM)
            gk = jnp.tile(attn_k_norm[j], N_KV_HEADS).reshape(1, N_KV_HEADS * HEAD_DIM)
            norm_args = (gq, gk, _block_diag_mean())
            wk = N_KV_HEADS * HEAD_DIM
            xp1, nk, nv = _attn_call(x, mod, l, False, attn_w_qkv[j], attn_w_o[j], ln1_g[l], ln1_b[l],
                                     n_seq=BATCH, seq_len=SEQ, tq=SEQ, n_kv=N_KV_HEADS, norm_args=norm_args,
                                     emit_kv=True, name="gqa_prompt")
            new_kv["k_attn"] = nk.reshape(BATCH, 1, SEQ, N_KV_HEADS, HEAD_DIM)
            new_kv["v_attn"] = nv.reshape(BATCH, 1, SEQ, N_KV_HEADS, HEAD_DIM)
            (xs1,) = _attn_call(x, mod, l, True, attn_w_qkv[j], attn_w_o[j], ln1_g[l], ln1_b[l],
                                n_seq=DEC_BATCH, seq_len=DEC_SEQ, tq=128, n_kv=N_KV_HEADS, norm_args=norm_args,
                                rope_args=_rope_tables(),
                                ctx_args=(cache_k_attn[:, j].reshape(DEC_BATCH, PAST_LEN, wk),
                                          cache_v_attn[:, j].reshape(DEC_BATCH, PAST_LEN, wk)),
                                name="gqa_sample")
            x1 = (xp1, xs1)
        else:
            wk = N_HEADS * HEAD_DIM
            xp1, nk, nv = _attn_call(x, mod, l, False, na_w_qkv[j], na_w_o[j], ln1_g[l], ln1_b[l],
                                     n_seq=BATCH, seq_len=SEQ, tq=SEQ, n_kv=N_HEADS, emit_kv=True,
                                     head_rows=True, name="mha_prompt")
            new_kv["k_na"] = nk.reshape(BATCH, 1, SEQ, N_HEADS, HEAD_DIM)
            new_kv["v_na"] = nv.reshape(BATCH, 1, SEQ, N_HEADS, HEAD_DIM)
            (xs1,) = _attn_call(x, mod, l, True, na_w_qkv[j], na_w_o[j], ln1_g[l], ln1_b[l],
                                n_seq=DEC_BATCH, seq_len=DEC_SEQ, tq=2 * GRID_W, n_kv=N_HEADS,
                                ctx_args=(cache_k_na[:, j].reshape(DEC_BATCH, PAST_LEN, wk),
                                          cache_v_na[:, j].reshape(DEC_BATCH, PAST_LEN, wk)),
                                bias=_na_bias_table(na_rpb[j]), name="na_sample")
            x1 = (xp1, xs1)
        out = _moe_layer(x1, mod, l, router_w[l], router_b[l], moe_w1, moe_b1, moe_w2, moe_b2,
                         ln2_g[l], ln2_b[l], split_out=l == DEPTH - 1)
        x = out[0]
    y_prompt = out[0].reshape(BATCH, SEQ, D_MODEL)
    y_sample = out[1].reshape(DEC_BATCH, DEC_SEQ, D_MODEL)
    return (y_prompt, y_sample, new_kv["k_attn"], new_kv["v_attn"], new_kv["k_na"], new_kv["v_na"])
```
